```python
import math
import jax, jax.numpy as jnp
from jax import lax
import numpy as np

D_MODEL = 2048
BATCH = 16
SEQ = 256
DEPTH = 2
DEC_BATCH = 4
DEC_SEQ = 1024
PAST_LEN = 256

GRID_W = 64
ROPE_BASE = 10000.0
Q_BLOCK = 128
EPS = 1e-6
N_EVEN = (DEPTH + 1) // 2
N_ODD = DEPTH // 2
MIX_W = D_MODEL
POOL_W = MIX_W // 2
POOL_WINDOWS = (2, 4, 8, 16)
POOL_G = POOL_W // len(POOL_WINDOWS)
DIFF_W = MIX_W - POOL_W
DIFF_DH = 64
DIFF_HEADS = DIFF_W // (2 * DIFF_DH)
DIFF_SCALE = DIFF_DH ** -0.5
MLA_HEADS = 16
Q_LORA = 512
KV_LORA = 512
QK_NOPE = 128
QK_ROPE = 64
V_DIM = 128
MLA_SCALE = (QK_NOPE + QK_ROPE) ** -0.5
N_EXPERTS = 64
TOP_K = 6
N_GROUPS = 8
TOPK_GROUPS = 4
D_EXPERT = 512
ROUTED_SCALE = 2.5

kernel_name = 'hybrid_diffusion_pool_diffattn_mla_moe_step'


def _rmsnorm(x, g):
    xf = x.astype(jnp.float32)
    y = xf * lax.rsqrt(jnp.mean(xf * xf, axis=-1, keepdims=True) + EPS)
    return (y * g.astype(jnp.float32)).astype(x.dtype)


def _adaln_params(cond, w, b):
    m = jax.nn.silu(cond) @ w + b
    return jnp.split(m, 6, axis=-1)


def _modulate(h, shift, scale):
    return h * (1 + scale) + shift


def _axial_rope_table(n_tok, rot_dim):
    rows = n_tok // GRID_W
    row = jnp.repeat(jnp.arange(rows, dtype=jnp.float32), GRID_W)
    col = jnp.tile(jnp.arange(GRID_W, dtype=jnp.float32), rows)
    quarter = rot_dim // 4
    inv = ROPE_BASE ** (-jnp.arange(quarter, dtype=jnp.float32) / quarter)
    ang = jnp.stack([row[:, None] * inv, col[:, None] * inv], axis=1)
    return jnp.cos(ang)[:, None], jnp.sin(ang)[:, None]


def _rope2d(x, cos, sin):
    xs = x.reshape(x.shape[:-1] + (2, 2, x.shape[-1] // 4))
    x1, x2 = xs[..., 0, :], xs[..., 1, :]
    c = cos.astype(x.dtype)
    s = sin.astype(x.dtype)
    out = jnp.stack([x1 * c - x2 * s, x2 * c + x1 * s], axis=-2)
    return out.reshape(x.shape)


def _map_query_blocks(fn, *qs):
    B, Lq = qs[0].shape[:2]
    nb = Lq // Q_BLOCK
    blocks = tuple(jnp.moveaxis(q.reshape((B, nb, Q_BLOCK) + q.shape[2:]), 1, 0) for q in qs)
    out = lax.map(lambda blk: fn(*blk), blocks)
    return jnp.moveaxis(out, 0, 1).reshape((B, Lq) + out.shape[3:])


def _multiscale_pool(u, w_grp, scale):
    B, L, _ = u.shape
    uf = u.astype(jnp.float32)
    cs = jnp.concatenate([jnp.zeros_like(uf[:, :1]), jnp.cumsum(uf, axis=1)], axis=1)
    t = jnp.arange(L)
    diffs = []
    for g, w in enumerate(POOL_WINDOWS):
        lo = jnp.maximum(t - w // 2, 0)
        hi = jnp.minimum(t + w // 2, L)
        sl = slice(g * POOL_G, (g + 1) * POOL_G)
        mean = (cs[:, hi, sl] - cs[:, lo, sl]) / (hi - lo).astype(jnp.float32)[None, :, None]
        diffs.append(mean - uf[..., sl])
    d = jnp.stack(diffs, axis=2).astype(u.dtype)
    y = jnp.einsum('blgc,gce->blge', d, w_grp).reshape(B, L, POOL_W)
    return y * scale


def _diff_attn_block(qb, k, v, lam):
    s = jnp.einsum('bqhcd,bkhcd->bchqk', qb, k).astype(jnp.float32) * DIFF_SCALE
    p = jax.nn.softmax(s, axis=-1)
    w = (p[:, 0] - lam * p[:, 1]).astype(v.dtype)
    return jnp.einsum('bhqk,bkhd->bqhd', w, v)


def _pool_diff_mixer(h, rope, ctx, lam_init, w_in, pool_w, pool_scale, lq1, lk1, lq2, lk2, subln_g, w_out):
    B, L, _ = h.shape
    proj = h @ w_in
    u = proj[..., :POOL_W]
    q = proj[..., POOL_W:POOL_W + DIFF_W].reshape(B, L, 2 * DIFF_HEADS, DIFF_DH)
    k = proj[..., POOL_W + DIFF_W:POOL_W + 2 * DIFF_W].reshape(B, L, DIFF_HEADS, 2 * DIFF_DH)
    v = proj[..., POOL_W + 2 * DIFF_W:].reshape(B, L, DIFF_HEADS, 2 * DIFF_DH)
    new_ctx = (k, v)
    if rope is not None:
        q = _rope2d(q, *rope)
        k = _rope2d(k.reshape(B, L, 2 * DIFF_HEADS, DIFF_DH), *rope).reshape(B, L, DIFF_HEADS, 2 * DIFF_DH)
    if ctx is not None:
        k = jnp.concatenate([ctx[0], k], axis=1)
        v = jnp.concatenate([ctx[1], v], axis=1)
    Lk = k.shape[1]
    q = q.reshape(B, L, DIFF_HEADS, 2, DIFF_DH)
    k = k.reshape(B, Lk, DIFF_HEADS, 2, DIFF_DH)
    lam = (jnp.exp(jnp.sum(lq1.astype(jnp.float32) * lk1.astype(jnp.float32)))
           - jnp.exp(jnp.sum(lq2.astype(jnp.float32) * lk2.astype(jnp.float32))) + lam_init)
    o = _map_query_blocks(lambda qb: _diff_attn_block(qb, k, v, lam), q)
    o = _rmsnorm(o, subln_g) * (1.0 - lam_init)
    y_pool = _multiscale_pool(u, pool_w, pool_scale)
    y = jnp.concatenate([y_pool, o.reshape(B, L, DIFF_W)], axis=-1) @ w_out
    return y, new_ctx


def _mla_attn_block(qn, qp, kn, kp, v):
    s = (jnp.einsum('bqhd,bkhd->bhqk', qn, kn) + jnp.einsum('bqhr,bkr->bhqk', qp, kp)).astype(jnp.float32) * MLA_SCALE
    p = jax.nn.softmax(s, axis=-1).astype(v.dtype)
    return jnp.einsum('bhqk,bkhd->bqhd', p, v)


def _mla_mixer(h, rope, ctx, w_dq, q_norm_g, w_uq, w_dkv, kv_norm_g, w_ukv, w_o):
    B, L, _ = h.shape
    q = (_rmsnorm(h @ w_dq, q_norm_g) @ w_uq).reshape(B, L, MLA_HEADS, QK_NOPE + QK_ROPE)
    q_nope, q_pe = q[..., :QK_NOPE], q[..., QK_NOPE:]
    kv_a = h @ w_dkv
    c_kv = _rmsnorm(kv_a[..., :KV_LORA], kv_norm_g)
    k_pe = kv_a[..., KV_LORA:]
    new_ctx = (c_kv, k_pe)
    if rope is not None:
        q_pe = _rope2d(q_pe, *rope)
        k_pe = _rope2d(k_pe[:, :, None, :], *rope)[:, :, 0, :]
    if ctx is not None:
        c_kv = jnp.concatenate([ctx[0], c_kv], axis=1)
        k_pe = jnp.concatenate([ctx[1], k_pe], axis=1)
    Lk = c_kv.shape[1]
    kv = (c_kv @ w_ukv).reshape(B, Lk, MLA_HEADS, QK_NOPE + V_DIM)
    k_nope, v = kv[..., :QK_NOPE], kv[..., QK_NOPE:]
    o = _map_query_blocks(lambda a, b: _mla_attn_block(a, b, k_nope, k_pe, v), q_nope, q_pe)
    return o.reshape(B, L, MLA_HEADS * V_DIM) @ w_o, new_ctx


def _moe(h, router_w, router_bias, w_gate, w_up, w_down, ws_gate, ws_up, ws_down):
    B, L, D = h.shape
    x = h.reshape(-1, D)
    T = x.shape[0]
    scores = jax.nn.sigmoid((x @ router_w).astype(jnp.float32))
    choice = scores + router_bias.astype(jnp.float32)
    grp = choice.reshape(T, N_GROUPS, N_EXPERTS // N_GROUPS)
    grp_score = lax.top_k(grp, 2)[0].sum(-1)
    _, gidx = lax.top_k(grp_score, TOPK_GROUPS)
    gmask = jax.nn.one_hot(gidx, N_GROUPS, dtype=jnp.float32).sum(1) > 0
    emask = jnp.repeat(gmask, N_EXPERTS // N_GROUPS, axis=1)
    masked = jnp.where(emask, choice, -jnp.inf)
    _, eidx = lax.top_k(masked, TOP_K)
    wsel = jnp.take_along_axis(scores, eidx, axis=1)
    wsel = wsel / jnp.sum(wsel, axis=-1, keepdims=True) * ROUTED_SCALE
    gates = jnp.sum(jax.nn.one_hot(eidx, N_EXPERTS, dtype=jnp.float32) * wsel[..., None], axis=1)

    def body(acc, p):
        wg, wu, wd, g = p
        y = (jax.nn.silu(x @ wg) * (x @ wu)) @ wd
        return acc + g[:, None].astype(x.dtype) * y, None

    routed, _ = lax.scan(body, jnp.zeros_like(x), (w_gate, w_up, w_down, gates.T))
    shared = (jax.nn.silu(x @ ws_gate) * (x @ ws_up)) @ ws_down
    return (routed + shared).reshape(B, L, D)


def _normal(k, shape, std):
    return jax.random.normal(k, shape, jnp.float32) * std


def setup_inputs(seed: int = 0) -> dict:
    key = jax.random.key(seed)
    ks = iter(jax.random.split(key, 48))
    D = D_MODEL
    sd = D ** -0.5
    inp = {}
    inp['x_prompt'] = _normal(next(ks), (BATCH, SEQ, D), 1.0)
    inp['x_sample'] = _normal(next(ks), (DEC_BATCH, DEC_SEQ, D), 1.0)
    inp['cache_diff_k'] = _normal(next(ks), (DEC_BATCH, N_EVEN, PAST_LEN, DIFF_HEADS, 2 * DIFF_DH), 1.0)
    inp['cache_diff_v'] = _normal(next(ks), (DEC_BATCH, N_EVEN, PAST_LEN, DIFF_HEADS, 2 * DIFF_DH), 1.0)
    inp['cache_mla_ckv'] = _normal(next(ks), (DEC_BATCH, N_ODD, PAST_LEN, KV_LORA), 1.0)
    inp['cache_mla_kpe'] = _normal(next(ks), (DEC_BATCH, N_ODD, PAST_LEN, QK_ROPE), 1.0)
    inp['c'] = _normal(next(ks), (DEC_BATCH, D), 1.0)
    inp['c_ctx'] = _normal(next(ks), (D,), 1.0)
    inp['ada_w'] = _normal(next(ks), (DEPTH, D, 6 * D), 0.5 * sd)
    inp['ada_b'] = _normal(next(ks), (DEPTH, 6 * D), 0.01)
    inp['norm_mix_g'] = 1.0 + _normal(next(ks), (DEPTH, D), 0.01)
    inp['norm_ffn_g'] = 1.0 + _normal(next(ks), (DEPTH, D), 0.01)
    inp['final_norm_g'] = 1.0 + _normal(next(ks), (D,), 0.01)
    inp['diff_w_in'] = _normal(next(ks), (N_EVEN, D, POOL_W + 3 * DIFF_W), sd)
    inp['pool_w'] = _normal(next(ks), (N_EVEN, len(POOL_WINDOWS), POOL_G, POOL_G), POOL_G ** -0.5)
    inp['pool_scale'] = 1.0 + _normal(next(ks), (N_EVEN, POOL_W), 0.1)
    inp['diff_lambda_q1'] = _normal(next(ks), (N_EVEN, DIFF_DH), 0.1)
    inp['diff_lambda_k1'] = _normal(next(ks), (N_EVEN, DIFF_DH), 0.1)
    inp['diff_lambda_q2'] = _normal(next(ks), (N_EVEN, DIFF_DH), 0.1)
    inp['diff_lambda_k2'] = _normal(next(ks), (N_EVEN, DIFF_DH), 0.1)
    inp['diff_subln_g'] = 1.0 + _normal(next(ks), (N_EVEN, 2 * DIFF_DH), 0.01)
    inp['even_w_out'] = _normal(next(ks), (N_EVEN, MIX_W, D), MIX_W ** -0.5)
    inp['mla_w_dq'] = _normal(next(ks), (N_ODD, D, Q_LORA), sd)
    inp['mla_q_norm_g'] = 1.0 + _normal(next(ks), (N_ODD, Q_LORA), 0.01)
    inp['mla_w_uq'] = _normal(next(ks), (N_ODD, Q_LORA, MLA_HEADS * (QK_NOPE + QK_ROPE)), Q_LORA ** -0.5)
    inp['mla_w_dkv'] = _normal(next(ks), (N_ODD, D, KV_LORA + QK_ROPE), sd)
    inp['mla_kv_norm_g'] = 1.0 + _normal(next(ks), (N_ODD, KV_LORA), 0.01)
    inp['mla_w_ukv'] = _normal(next(ks), (N_ODD, KV_LORA, MLA_HEADS * (QK_NOPE + V_DIM)), KV_LORA ** -0.5)
    inp['mla_w_o'] = _normal(next(ks), (N_ODD, MLA_HEADS * V_DIM, D), (MLA_HEADS * V_DIM) ** -0.5)
    inp['router_w'] = _normal(next(ks), (DEPTH, D, N_EXPERTS), sd)
    inp['router_bias'] = _normal(next(ks), (DEPTH, N_EXPERTS), 0.01)
    inp['expert_w_gate'] = _normal(next(ks), (DEPTH, N_EXPERTS, D, D_EXPERT), sd)
    inp['expert_w_up'] = _normal(next(ks), (DEPTH, N_EXPERTS, D, D_EXPERT), sd)
    inp['expert_w_down'] = _normal(next(ks), (DEPTH, N_EXPERTS, D_EXPERT, D), D_EXPERT ** -0.5)
    inp['shared_w_gate'] = _normal(next(ks), (DEPTH, D, D_EXPERT), sd)
    inp['shared_w_up'] = _normal(next(ks), (DEPTH, D, D_EXPERT), sd)
    inp['shared_w_down'] = _normal(next(ks), (DEPTH, D_EXPERT, D), D_EXPERT ** -0.5)
    return inp


def reference(x_prompt, x_sample, cache_diff_k, cache_diff_v, cache_mla_ckv, cache_mla_kpe, c, c_ctx,
              ada_w, ada_b, norm_mix_g, norm_ffn_g, final_norm_g,
              diff_w_in, pool_w, pool_scale, diff_lambda_q1, diff_lambda_k1, diff_lambda_q2, diff_lambda_k2,
              diff_subln_g, even_w_out,
              mla_w_dq, mla_q_norm_g, mla_w_uq, mla_w_dkv, mla_kv_norm_g, mla_w_ukv, mla_w_o,
              router_w, router_bias, expert_w_gate, expert_w_up, expert_w_down,
              shared_w_gate, shared_w_up, shared_w_down):
    lat_len = x_sample.shape[1]
    rope_diff = _axial_rope_table(lat_len, DIFF_DH)
    rope_mla = _axial_rope_table(lat_len, QK_ROPE)
    xp, xs = x_prompt, x_sample
    new_dk, new_dv, new_ckv, new_kpe = [], [], [], []
    for i in range(DEPTH):
        mp = _adaln_params(c_ctx[None, None, :], ada_w[i], ada_b[i])
        ms = _adaln_params(c[:, None, :], ada_w[i], ada_b[i])
        hp = _modulate(_rmsnorm(xp, norm_mix_g[i]), mp[0], mp[1])
        hs = _modulate(_rmsnorm(xs, norm_mix_g[i]), ms[0], ms[1])
        if i % 2 == 0:
            j = i // 2
            lam_init = 0.8 - 0.6 * math.exp(-0.3 * i)
            ew = (diff_w_in[j], pool_w[j], pool_scale[j], diff_lambda_q1[j], diff_lambda_k1[j],
                  diff_lambda_q2[j], diff_lambda_k2[j], diff_subln_g[j], even_w_out[j])
            op, (kc, vc) = _pool_diff_mixer(hp, None, None, lam_init, *ew)
            os_, _ = _pool_diff_mixer(hs, rope_diff, (cache_diff_k[:, j], cache_diff_v[:, j]), lam_init, *ew)
            new_dk.append(kc)
            new_dv.append(vc)
        else:
            j = i // 2
            mw = (mla_w_dq[j], mla_q_norm_g[j], mla_w_uq[j], mla_w_dkv[j], mla_kv_norm_g[j], mla_w_ukv[j], mla_w_o[j])
            op, (ckv, kpe) = _mla_mixer(hp, None, None, *mw)
            os_, _ = _mla_mixer(hs, rope_mla, (cache_mla_ckv[:, j], cache_mla_kpe[:, j]), *mw)
            new_ckv.append(ckv)
            new_kpe.append(kpe)
        xp = xp + mp[2] * op
        xs = xs + ms[2] * os_
        fw = (router_w[i], router_bias[i], expert_w_gate[i], expert_w_up[i], expert_w_down[i],
              shared_w_gate[i], shared_w_up[i], shared_w_down[i])
        hp = _modulate(_rmsnorm(xp, norm_ffn_g[i]), mp[3], mp[4])
        hs = _modulate(_rmsnorm(xs, norm_ffn_g[i]), ms[3], ms[4])
        xp = xp + mp[5] * _moe(hp, *fw)
        xs = xs + ms[5] * _moe(hs, *fw)
    y_prompt = _rmsnorm(xp, final_norm_g)
    y_sample = _rmsnorm(xs, final_norm_g)
    return (y_prompt, y_sample, jnp.stack(new_dk, axis=1), jnp.stack(new_dv, axis=1),
            jnp.stack(new_ckv, axis=1), jnp.stack(new_kpe, axis=1))
```

```python
import functools
import math

import jax
import jax.numpy as jnp
from jax import lax
from jax.experimental import pallas as pl
from jax.experimental.pallas import tpu as pltpu

F32 = jnp.float32
BF16 = jnp.bfloat16

D = 2048
N_PROMPT_SEQ = 16
PROMPT_LEN = 256
N_LAT_SEQ = 4
LAT_LEN = 1024
PAST_LEN = 256
T_PROMPT = N_PROMPT_SEQ * PROMPT_LEN
T_LAT = N_LAT_SEQ * LAT_LEN
T = T_PROMPT + T_LAT
UNIT = 1024
N_UNITS = T // UNIT
N_PROMPT_UNITS = T_PROMPT // UNIT
SEQ_PER_PROMPT_UNIT = UNIT // PROMPT_LEN
N_GROUPS_PAD = 8

GRID_W = 64
ROPE_BASE = 10000.0
EPS = 1e-6
POOL_W = 1024
POOL_WINDOWS = (2, 4, 8, 16)
POOL_G = 256
POOL_HALO = 8
DIFF_W = 1024
DIFF_DH = 64
DIFF_HEADS = 8
DIFF_SCALE = DIFF_DH ** -0.5
MLA_HEADS = 16
Q_LORA = 512
KV_LORA = 512
QK_NOPE = 128
QK_ROPE = 64
V_DIM = 128
MLA_SCALE = (QK_NOPE + QK_ROPE) ** -0.5
N_EXPERTS = 64
TOP_K = 6
N_EGROUPS = 8
EGROUP = N_EXPERTS // N_EGROUPS
TOPK_GROUPS = 4
D_EXPERT = 512
ROUTED_SCALE = 2.5

LANES = 128
ATT_QB = 256
ROUTE_TM = 256
GMM_TM = 256
N_PAIRS = T * TOP_K
N_ROW_TILES = N_PAIRS // GMM_TM
N_VISITS = N_ROW_TILES + N_EXPERTS
VMEM_LIMIT = 56 * 1024 * 1024

_NT = (((1,), (1,)), ((), ()))


def _cparams(n_axes, vmem=VMEM_LIMIT):
    return pltpu.CompilerParams(dimension_semantics=("arbitrary",) * n_axes, vmem_limit_bytes=vmem)


def _group_of_tile(i, tm):
    n_p = T_PROMPT // tm
    per_seq = LAT_LEN // tm
    return jnp.where(i < n_p, 0, 1 + (i - n_p) // per_seq)


def _sigmoid(x):
    return 1.0 / (1.0 + jnp.exp(-x))


def _rms(x, g):
    return x * lax.rsqrt(jnp.mean(x * x, axis=-1, keepdims=True) + EPS) * g


def _ada_kernel(c_ref, w_ref, b_ref, o_ref):
    c = c_ref[...]
    s = c * _sigmoid(c)
    o_ref[...] = jnp.dot(s, w_ref[...], preferred_element_type=F32,
                         precision=lax.Precision.HIGHEST) + b_ref[...]


def _ada_params(cond, ada_w, ada_b):
    depth, _, n = ada_w.shape
    tn = 1024
    out = pl.pallas_call(
        _ada_kernel,
        grid=(depth, n // tn),
        in_specs=[
            pl.BlockSpec((N_GROUPS_PAD, D), lambda l, j: (0, 0)),
            pl.BlockSpec((None, D, tn), lambda l, j: (l, 0, j)),
            pl.BlockSpec((None, 1, tn), lambda l, j: (l, 0, j)),
        ],
        out_specs=pl.BlockSpec((None, N_GROUPS_PAD, tn), lambda l, j: (l, 0, j)),
        out_shape=jax.ShapeDtypeStruct((depth, N_GROUPS_PAD, n), F32),
        compiler_params=_cparams(2),
        name="ada_params",
    )(cond, ada_w, ada_b.reshape(depth, 1, n))
    return out.reshape(depth, N_GROUPS_PAD, 1, n)


def _mod_spec(layer, chunk, tm, grid_rank, row_axis):
    def index_map(*ids):
        return (layer, _group_of_tile(ids[row_axis], tm), 0, chunk)
    del grid_rank
    return pl.BlockSpec((None, None, 1, D), index_map)


def _norm_mod_kernel(x_ref, g_ref, sh_ref, sc_ref, o_ref):
    y = _rms(x_ref[...], g_ref[...])
    o_ref[...] = (y * (1.0 + sc_ref[...]) + sh_ref[...]).astype(o_ref.dtype)


def _norm_mod(x, g, mods, layer, shift_chunk, scale_chunk):
    tm = 512
    return pl.pallas_call(
        _norm_mod_kernel,
        grid=(T // tm,),
        in_specs=[
            pl.BlockSpec((tm, D), lambda i: (i, 0)),
            pl.BlockSpec((1, D), lambda i: (0, 0)),
            _mod_spec(layer, shift_chunk, tm, 1, 0),
            _mod_spec(layer, scale_chunk, tm, 1, 0),
        ],
        out_specs=pl.BlockSpec((tm, D), lambda i: (i, 0)),
        out_shape=jax.ShapeDtypeStruct((T, D), BF16),
        compiler_params=_cparams(1),
        name="norm_mod",
    )(x, g.reshape(1, D), mods, mods)


def _mm_kernel(*refs, n_a, epilogue):
    a_refs = refs[:n_a]
    w_refs = refs[n_a:2 * n_a]
    pos = 2 * n_a
    if epilogue == "rms":
        g_ref = refs[pos]
        pos += 1
    elif epilogue == "residual":
        res_ref, gate_ref = refs[pos], refs[pos + 1]
        pos += 2
    o_ref = refs[pos]
    w_scr = refs[pos + 1:]

    @pl.when(pl.program_id(1) == 0)
    def _():
        for w_ref, scr in zip(w_refs, w_scr):
            scr[...] = w_ref[...].astype(BF16)

    acc = None
    for a_ref, scr in zip(a_refs, w_scr):
        part = jnp.dot(a_ref[...].astype(BF16), scr[...], preferred_element_type=F32)
        acc = part if acc is None else acc + part
    if epilogue == "rms":
        acc = _rms(acc, g_ref[...])
    elif epilogue == "residual":
        acc = res_ref[...] + gate_ref[...] * acc
    o_ref[...] = acc.astype(o_ref.dtype)


def _matmul(a_list, w, w_row_blocks, n_out, out_dtype, *, tm=512, tn=None, epilogue=None,
            gain=None, residual=None, mods=None, layer=None, gate_chunk=None, name="matmul"):
    m = a_list[0].shape[0]
    tn = n_out if tn is None else tn
    n_a = len(a_list)
    in_specs, args = [], []
    for a in a_list:
        k = a.shape[1]
        in_specs.append(pl.BlockSpec((tm, k), lambda j, i: (i, 0)))
        args.append(a)
    scratch = []
    for a, rb in zip(a_list, w_row_blocks):
        k = a.shape[1]
        in_specs.append(pl.BlockSpec((k, tn), lambda j, i, rb=rb: (rb, j)))
        args.append(w)
        scratch.append(pltpu.VMEM((k, tn), BF16))
    if epilogue == "rms":
        assert tn == n_out
        in_specs.append(pl.BlockSpec((1, tn), lambda j, i: (0, 0)))
        args.append(gain.reshape(1, n_out))
    elif epilogue == "residual":
        in_specs.append(pl.BlockSpec((tm, tn), lambda j, i: (i, j)))
        args.append(residual)
        in_specs.append(pl.BlockSpec(
            (None, None, 1, tn),
            lambda j, i: (layer, _group_of_tile(i, tm), 0, gate_chunk * (D // tn) + j)))
        args.append(mods)
    return pl.pallas_call(
        functools.partial(_mm_kernel, n_a=n_a, epilogue=epilogue),
        grid=(n_out // tn, m // tm),
        in_specs=in_specs,
        out_specs=pl.BlockSpec((tm, tn), lambda j, i: (i, j)),
        out_shape=jax.ShapeDtypeStruct((m, n_out), out_dtype),
        scratch_shapes=scratch,
        compiler_params=_cparams(2),
        name=name,
    )(*args)


def _pool_seq(u_ref, row0, seq_len, pw_ref, scale_ref, o_ref, pad_ref):
    t = lax.broadcasted_iota(jnp.int32, (seq_len, 1), 0)
    zeros = jnp.zeros((POOL_HALO, POOL_G), F32)
    for g, w in enumerate(POOL_WINDOWS):
        cols = slice(g * POOL_G, (g + 1) * POOL_G)
        ug = u_ref[row0:row0 + seq_len, cols]
        pad_ref[0:POOL_HALO, :] = zeros
        pad_ref[POOL_HALO:POOL_HALO + seq_len, :] = ug
        pad_ref[POOL_HALO + seq_len:2 * POOL_HALO + seq_len, :] = zeros
        total = None
        for off in range(-(w // 2), w // 2):
            part = pad_ref[POOL_HALO + off:POOL_HALO + off + seq_len, :]
            total = part if total is None else total + part
        cnt = (jnp.minimum(t + w // 2, seq_len) - jnp.maximum(t - w // 2, 0)).astype(F32)
        d = (total / cnt - ug).astype(BF16)
        y = jnp.dot(d, pw_ref[g].astype(BF16), preferred_element_type=F32)
        o_ref[row0:row0 + seq_len, cols] = (y * scale_ref[:, cols]).astype(o_ref.dtype)


def _pool_kernel(u_ref, pw_ref, scale_ref, o_ref, pad_ref):
    unit = pl.program_id(0)

    @pl.when(unit < N_PROMPT_UNITS)
    def _():
        for s in range(SEQ_PER_PROMPT_UNIT):
            _pool_seq(u_ref, s * PROMPT_LEN, PROMPT_LEN, pw_ref, scale_ref, o_ref, pad_ref)

    @pl.when(unit >= N_PROMPT_UNITS)
    def _():
        _pool_seq(u_ref, 0, LAT_LEN, pw_ref, scale_ref, o_ref, pad_ref)


def _pool(proj, pool_w, pool_scale):
    return pl.pallas_call(
        _pool_kernel,
        grid=(N_UNITS,),
        in_specs=[
            pl.BlockSpec((UNIT, POOL_W), lambda u: (u, 0)),
            pl.BlockSpec((len(POOL_WINDOWS), POOL_G, POOL_G), lambda u: (0, 0, 0)),
            pl.BlockSpec((1, POOL_W), lambda u: (0, 0)),
        ],
        out_specs=pl.BlockSpec((UNIT, POOL_W), lambda u: (u, 0)),
        out_shape=jax.ShapeDtypeStruct((T, POOL_W), BF16),
        scratch_shapes=[pltpu.VMEM((LAT_LEN + 2 * POOL_HALO, POOL_G), F32)],
        compiler_params=_cparams(1),
        name="pool",
    )(proj, pool_w, pool_scale.reshape(1, POOL_W))


def _rope_tables():
    pos = jnp.arange(LAT_LEN)
    row = (pos // GRID_W).astype(F32)
    col = (pos % GRID_W).astype(F32)
    quarter = QK_ROPE // 4
    inv = ROPE_BASE ** (-jnp.arange(quarter, dtype=F32) / quarter)
    lane = jnp.arange(LANES)
    axis = (lane % QK_ROPE) // (QK_ROPE // 2)
    freq = inv[lane % quarter]
    p = jnp.where(axis[None, :] == 0, row[:, None], col[:, None])
    ang = p * freq[None, :]
    sign = jnp.where((lane % (QK_ROPE // 2)) < quarter, -1.0, 1.0).astype(F32)
    return jnp.cos(ang), jnp.sin(ang) * sign[None, :]


def _rope(x, cos, sin_signed):
    quarter = QK_ROPE // 4
    lane = lax.broadcasted_iota(jnp.int32, x.shape, 1)
    first_half = (lane % (QK_ROPE // 2)) < quarter
    partner = jnp.where(first_half, pltpu.roll(x, LANES - quarter, 1), pltpu.roll(x, quarter, 1))
    return x * cos + partner * sin_signed


def _diff_core(q, k, v, lam, g, lam_init):
    lane = lax.broadcasted_iota(jnp.int32, q.shape, 1)
    low = lane < DIFF_DH
    q1 = jnp.where(low, q, 0.0).astype(BF16)
    q2 = jnp.where(low, 0.0, q).astype(BF16)

    def probs(qm):
        s = lax.dot_general(qm, k, _NT, preferred_element_type=F32)
        e = jnp.exp(s - jnp.max(s, axis=-1, keepdims=True))
        return e * (1.0 / jnp.sum(e, axis=-1, keepdims=True))

    w = (probs(q1) - lam * probs(q2)).astype(BF16)
    o = jnp.dot(w, v, preferred_element_type=F32)
    return _rms(o, g) * (1.0 - lam_init)


def _diff_attn_kernel(lp_ref, q_ref, k_ref, v_ref, ck_ref, cv_ref, cos_ref, sin_ref, g_ref, o_ref,
                      *, lam_init):
    unit = pl.program_id(0)
    lp = lp_ref[...]
    lam = (jnp.exp(jnp.sum(lp[0:1] * lp[1:2], axis=-1, keepdims=True))
           - jnp.exp(jnp.sum(lp[2:3] * lp[3:4], axis=-1, keepdims=True)) + lam_init)
    g = g_ref[...]

    @pl.when(unit < N_PROMPT_UNITS)
    def _():
        for s in range(SEQ_PER_PROMPT_UNIT):
            rows = slice(s * PROMPT_LEN, (s + 1) * PROMPT_LEN)
            q = q_ref[rows, :] * DIFF_SCALE
            k = k_ref[rows, :].astype(BF16)
            v = v_ref[rows, :].astype(BF16)
            o_ref[rows, :] = _diff_core(q, k, v, lam, g, lam_init).astype(o_ref.dtype)

    @pl.when(unit >= N_PROMPT_UNITS)
    def _():
        cos, sin = cos_ref[...], sin_ref[...]
        k = jnp.concatenate([ck_ref[...].astype(BF16),
                             _rope(k_ref[...], cos, sin).astype(BF16)], axis=0)
        v = jnp.concatenate([cv_ref[...].astype(BF16), v_ref[...].astype(BF16)], axis=0)
        for b in range(LAT_LEN // ATT_QB):
            rows = slice(b * ATT_QB, (b + 1) * ATT_QB)
            q = _rope(q_ref[rows, :], cos[rows], sin[rows]) * DIFF_SCALE
            o_ref[rows, :] = _diff_core(q, k, v, lam, g, lam_init).astype(o_ref.dtype)


def _diff_attn(proj, cache_k, cache_v, lam_params, subln_g, cos, sin, lam_init):
    hd = 2 * DIFF_DH
    q0, k0, v0 = POOL_W // hd, (POOL_W + DIFF_W) // hd, (POOL_W + 2 * DIFF_W) // hd

    def ctx_map(u, h):
        return (jnp.maximum(u - N_PROMPT_UNITS, 0), h)

    return pl.pallas_call(
        functools.partial(_diff_attn_kernel, lam_init=lam_init),
        grid=(N_UNITS, DIFF_HEADS),
        in_specs=[
            pl.BlockSpec((4, DIFF_DH), lambda u, h: (0, 0)),
            pl.BlockSpec((UNIT, hd), lambda u, h: (u, q0 + h)),
            pl.BlockSpec((UNIT, hd), lambda u, h: (u, k0 + h)),
            pl.BlockSpec((UNIT, hd), lambda u, h: (u, v0 + h)),
            pl.BlockSpec((PAST_LEN, hd), ctx_map),
            pl.BlockSpec((PAST_LEN, hd), ctx_map),
            pl.BlockSpec((LAT_LEN, LANES), lambda u, h: (0, 0)),
            pl.BlockSpec((LAT_LEN, LANES), lambda u, h: (0, 0)),
            pl.BlockSpec((1, hd), lambda u, h: (0, 0)),
        ],
        out_specs=pl.BlockSpec((UNIT, hd), lambda u, h: (u, h)),
        out_shape=jax.ShapeDtypeStruct((T, DIFF_W), BF16),
        compiler_params=_cparams(2),
        name="diff_attn",
    )(lam_params, proj, proj, proj, cache_k, cache_v, cos, sin, subln_g.reshape(1, hd))


def _mla_core(qc, kc, v):
    s = lax.dot_general(qc, kc, _NT, preferred_element_type=F32) * MLA_SCALE
    e = jnp.exp(s - jnp.max(s, axis=-1, keepdims=True))
    p = (e * (1.0 / jnp.sum(e, axis=-1, keepdims=True))).astype(BF16)
    return jnp.dot(p, v, preferred_element_type=F32)


def _mla_attn_kernel(qn_ref, qp_ref, kn_ref, v_ref, kp_ref, ckn_ref, cv_ref, ckp_ref, cos_ref, sin_ref,
                     o_ref):
    unit = pl.program_id(0)
    head = pl.program_id(1)
    lane = lax.broadcasted_iota(jnp.int32, (ATT_QB, LANES), 1)
    mine = (lane // QK_ROPE) == (head % 2)

    @pl.when(unit < N_PROMPT_UNITS)
    def _():
        for s in range(SEQ_PER_PROMPT_UNIT):
            rows = slice(s * PROMPT_LEN, (s + 1) * PROMPT_LEN)
            qp = jnp.where(mine, qp_ref[rows, :], 0.0).astype(BF16)
            qc = jnp.concatenate([qn_ref[rows, :], qp], axis=1)
            kc = jnp.concatenate([kn_ref[rows, :], kp_ref[rows, :].astype(BF16)], axis=1)
            o_ref[rows, :] = _mla_core(qc, kc, v_ref[rows, :]).astype(o_ref.dtype)

    @pl.when(unit >= N_PROMPT_UNITS)
    def _():
        cos, sin = cos_ref[...], sin_ref[...]
        k_ctx = jnp.concatenate([ckn_ref[...], ckp_ref[...].astype(BF16)], axis=1)
        k_new = jnp.concatenate([kn_ref[...], _rope(kp_ref[...], cos, sin).astype(BF16)], axis=1)
        kc = jnp.concatenate([k_ctx, k_new], axis=0)
        v = jnp.concatenate([cv_ref[...], v_ref[...]], axis=0)
        for b in range(LAT_LEN // ATT_QB):
            rows = slice(b * ATT_QB, (b + 1) * ATT_QB)
            qp = jnp.where(mine, _rope(qp_ref[rows, :], cos[rows], sin[rows]), 0.0).astype(BF16)
            qc = jnp.concatenate([qn_ref[rows, :], qp], axis=1)
            o_ref[rows, :] = _mla_core(qc, kc, v).astype(o_ref.dtype)


def _mla_attn(q_nope, q_pe, kv, kp_dup, kv_ctx, kp_ctx_dup, cos, sin):
    def ctx_row(u):
        return jnp.maximum(u - N_PROMPT_UNITS, 0)

    return pl.pallas_call(
        _mla_attn_kernel,
        grid=(N_UNITS, MLA_HEADS),
        in_specs=[
            pl.BlockSpec((UNIT, QK_NOPE), lambda u, h: (u, h)),
            pl.BlockSpec((UNIT, LANES), lambda u, h: (u, h // 2)),
            pl.BlockSpec((UNIT, QK_NOPE), lambda u, h: (u, 2 * h)),
            pl.BlockSpec((UNIT, V_DIM), lambda u, h: (u, 2 * h + 1)),
            pl.BlockSpec((UNIT, LANES), lambda u, h: (u, 0)),
            pl.BlockSpec((PAST_LEN, QK_NOPE), lambda u, h: (ctx_row(u), 2 * h)),
            pl.BlockSpec((PAST_LEN, V_DIM), lambda u, h: (ctx_row(u), 2 * h + 1)),
            pl.BlockSpec((PAST_LEN, LANES), lambda u, h: (ctx_row(u), 0)),
            pl.BlockSpec((LAT_LEN, LANES), lambda u, h: (0, 0)),
            pl.BlockSpec((LAT_LEN, LANES), lambda u, h: (0, 0)),
        ],
        out_specs=pl.BlockSpec((UNIT, V_DIM), lambda u, h: (u, h)),
        out_shape=jax.ShapeDtypeStruct((T, MLA_HEADS * V_DIM), BF16),
        compiler_params=_cparams(2),
        name="mla_attn",
    )(q_nope, q_pe, kv, kv, kp_dup, kv_ctx, kv_ctx, kp_ctx_dup, cos, sin)


def _route_kernel(x_ref, g_ref, sh_ref, sc_ref, rw_ref, rb_ref,
                  h_ref, eidx_ref, rank_ref, wts_ref, cnt_ref, carry_ref):
    tm = ROUTE_TM

    @pl.when(pl.program_id(0) == 0)
    def _():
        carry_ref[...] = jnp.zeros_like(carry_ref)

    h = _rms(x_ref[...], g_ref[...]) * (1.0 + sc_ref[...]) + sh_ref[...]
    h_ref[...] = h
    logits = lax.dot_general(rw_ref[...], h, _NT, preferred_element_type=F32,
                             precision=lax.Precision.HIGHEST)
    scores = _sigmoid(logits)
    choice = scores + rb_ref[...]
    neg = -jnp.inf

    def take_max(vals, iota, n):
        m = jnp.max(vals, axis=0, keepdims=True)
        idx = jnp.min(jnp.where(vals == m, iota, n), axis=0, keepdims=True)
        return iota == idx, m, idx

    iota8 = lax.broadcasted_iota(jnp.int32, (8, tm), 0)

    def stack_rows(rows, dtype):
        out = jnp.zeros((8, tm), dtype)
        for k, r in enumerate(rows):
            out = jnp.where(iota8 == k, r.astype(dtype), out)
        return out

    gscore = []
    for gi in range(N_EGROUPS):
        grp = choice[gi * EGROUP:(gi + 1) * EGROUP]
        oh, m1, _ = take_max(grp, iota8, EGROUP)
        m2 = jnp.max(jnp.where(oh, neg, grp), axis=0, keepdims=True)
        gscore.append(m1 + m2)
    cur = stack_rows(gscore, F32)
    gsel = jnp.zeros((N_EGROUPS, tm), F32)
    for _ in range(TOPK_GROUPS):
        oh, _, _ = take_max(cur, iota8, N_EGROUPS)
        gsel = jnp.where(oh, 1.0, gsel)
        cur = jnp.where(oh, neg, cur)
    emask = jnp.concatenate(
        [jnp.broadcast_to(gsel[gi:gi + 1], (EGROUP, tm)) for gi in range(N_EGROUPS)], axis=0)
    masked = jnp.where(emask > 0.0, choice, neg)

    iota_e = lax.broadcasted_iota(jnp.int32, (N_EXPERTS, tm), 0)
    onehots, idxs, wsel = [], [], []
    sel = jnp.zeros((N_EXPERTS, tm), F32)
    for _ in range(TOP_K):
        oh, _, idx = take_max(masked, iota_e, N_EXPERTS)
        onehots.append(oh)
        idxs.append(idx)
        wsel.append(jnp.sum(jnp.where(oh, scores, 0.0), axis=0, keepdims=True))
        masked = jnp.where(oh, neg, masked)
        sel = jnp.where(oh, 1.0, sel)
    wsum = wsel[0]
    for w in wsel[1:]:
        wsum = wsum + w

    before = jnp.where(lax.broadcasted_iota(jnp.int32, (tm, tm), 0)
                       < lax.broadcasted_iota(jnp.int32, (tm, tm), 1), 1.0, 0.0).astype(BF16)
    rank_all = carry_ref[...] + jnp.dot(sel.astype(BF16), before, preferred_element_type=F32)
    carry_ref[...] = carry_ref[...] + jnp.sum(sel, axis=1, keepdims=True)
    cnt_ref[...] = carry_ref[...]

    ranks = [jnp.sum(jnp.where(oh, rank_all, 0.0), axis=0, keepdims=True) for oh in onehots]
    eidx_ref[...] = stack_rows(idxs, jnp.int32)
    rank_ref[...] = stack_rows(ranks, jnp.int32)
    wts_ref[...] = stack_rows([w / wsum * ROUTED_SCALE for w in wsel], F32)


def _route(x, g, mods, layer, router_w, router_bias):
    tm = ROUTE_TM
    return pl.pallas_call(
        _route_kernel,
        grid=(T // tm,),
        in_specs=[
            pl.BlockSpec((tm, D), lambda i: (i, 0)),
            pl.BlockSpec((1, D), lambda i: (0, 0)),
            _mod_spec(layer, 3, tm, 1, 0),
            _mod_spec(layer, 4, tm, 1, 0),
            pl.BlockSpec((N_EXPERTS, D), lambda i: (0, 0)),
            pl.BlockSpec((N_EXPERTS, 1), lambda i: (0, 0)),
        ],
        out_specs=[
            pl.BlockSpec((tm, D), lambda i: (i, 0)),
            pl.BlockSpec((8, tm), lambda i: (0, i)),
            pl.BlockSpec((8, tm), lambda i: (0, i)),
            pl.BlockSpec((8, tm), lambda i: (0, i)),
            pl.BlockSpec((N_EXPERTS, 1), lambda i: (0, 0)),
        ],
        out_shape=[
            jax.ShapeDtypeStruct((T, D), F32),
            jax.ShapeDtypeStruct((8, T), jnp.int32),
            jax.ShapeDtypeStruct((8, T), jnp.int32),
            jax.ShapeDtypeStruct((8, T), F32),
            jax.ShapeDtypeStruct((N_EXPERTS, 1), F32),
        ],
        scratch_shapes=[pltpu.VMEM((N_EXPERTS, 1), F32)],
        compiler_params=_cparams(1),
        name="route",
    )(x, g.reshape(1, D), mods, mods, router_w.T, router_bias.reshape(N_EXPERTS, 1))


def _row_copy(src, src_row, dst, dst_row, sem):
    return pltpu.make_async_copy(src.at[pl.ds(src_row, 1), :], dst.at[pl.ds(dst_row, 1), :], sem)


def _pair_slot(starts_ref, er_ref, j, k):
    e = er_ref[0, 0, k * ROUTE_TM + j]
    r = er_ref[0, 0, (TOP_K + k) * ROUTE_TM + j]
    return starts_ref[e] + r


def _dispatch_kernel(starts_ref, er_ref, h_ref, xs_ref, sem):
    def issue(j, carry):
        for k in range(TOP_K):
            _row_copy(h_ref, j, xs_ref, _pair_slot(starts_ref, er_ref, j, k), sem).start()
        return carry

    def drain(j, carry):
        for k in range(TOP_K):
            _row_copy(h_ref, 0, xs_ref, 0, sem).wait()
        return carry

    lax.fori_loop(0, ROUTE_TM, issue, 0)
    lax.fori_loop(0, ROUTE_TM, drain, 0)


def _dispatch(starts, er, h):
    tm = ROUTE_TM
    return pl.pallas_call(
        _dispatch_kernel,
        grid_spec=pltpu.PrefetchScalarGridSpec(
            num_scalar_prefetch=1,
            grid=(T // tm,),
            in_specs=[
                pl.BlockSpec((1, 1, 2 * TOP_K * tm), lambda i, s: (i, 0, 0), memory_space=pltpu.SMEM),
                pl.BlockSpec((tm, D), lambda i, s: (i, 0)),
            ],
            out_specs=pl.BlockSpec(memory_space=pl.ANY),
            scratch_shapes=[pltpu.SemaphoreType.DMA(())],
        ),
        out_shape=jax.ShapeDtypeStruct((N_PAIRS, D), F32),
        compiler_params=_cparams(1),
        name="dispatch",
    )(starts, er, h)


def _combine_kernel(starts_ref, er_ref, x_ref, sh_ref, w_ref, gate_ref, fg_ref, ys_ref, o_ref,
                    buf_ref, sem, *, final):
    def issue(j, carry):
        for k in range(TOP_K):
            _row_copy(ys_ref, _pair_slot(starts_ref, er_ref, j, k), buf_ref.at[k], j, sem).start()
        return carry

    def drain(j, carry):
        for k in range(TOP_K):
            _row_copy(ys_ref, 0, buf_ref.at[k], 0, sem).wait()
        return carry

    lax.fori_loop(0, ROUTE_TM, issue, 0)
    lax.fori_loop(0, ROUTE_TM, drain, 0)
    w = w_ref[...]
    acc = None
    for k in range(TOP_K):
        term = w[:, k:k + 1] * buf_ref[k]
        acc = term if acc is None else acc + term
    xn = x_ref[...] + gate_ref[...] * (acc + sh_ref[...])
    if final:
        xn = _rms(xn, fg_ref[...])
    o_ref[...] = xn


def _combine(starts, er, x, shared, wts_t, mods, layer, final_g, ys, final):
    tm = ROUTE_TM
    return pl.pallas_call(
        functools.partial(_combine_kernel, final=final),
        grid_spec=pltpu.PrefetchScalarGridSpec(
            num_scalar_prefetch=1,
            grid=(T // tm,),
            in_specs=[
                pl.BlockSpec((1, 1, 2 * TOP_K * tm), lambda i, s: (i, 0, 0), memory_space=pltpu.SMEM),
                pl.BlockSpec((tm, D), lambda i, s: (i, 0)),
                pl.BlockSpec((tm, D), lambda i, s: (i, 0)),
                pl.BlockSpec((tm, 8), lambda i, s: (i, 0)),
                pl.BlockSpec((None, None, 1, D), lambda i, s: (layer, _group_of_tile(i, tm), 0, 5)),
                pl.BlockSpec((1, D), lambda i, s: (0, 0)),
                pl.BlockSpec(memory_space=pl.ANY),
            ],
            out_specs=pl.BlockSpec((tm, D), lambda i, s: (i, 0)),
            scratch_shapes=[pltpu.VMEM((TOP_K, tm, D), F32), pltpu.SemaphoreType.DMA(())],
        ),
        out_shape=jax.ShapeDtypeStruct((T, D), F32),
        compiler_params=_cparams(1),
        name="combine",
    )(starts, er, x, shared, wts_t, mods, final_g.reshape(1, D), ys)


def _gmm_kernel(e_ref, t_ref, lo_ref, hi_ref, xs_ref, wg_ref, wu_ref, wd_ref, ys_ref, wgu_scr, wd_scr):
    v = pl.program_id(0)
    prev = jnp.maximum(v - 1, 0)
    new_expert = (v == 0) | (e_ref[v] != e_ref[prev])
    new_tile = (v == 0) | (t_ref[v] != t_ref[prev])
    lo, hi = lo_ref[v], hi_ref[v]

    @pl.when(new_expert)
    def _():
        wgu_scr[:, :D_EXPERT] = wg_ref[...].astype(BF16)
        wgu_scr[:, D_EXPERT:] = wu_ref[...].astype(BF16)
        wd_scr[...] = wd_ref[...].astype(BF16)

    @pl.when(hi > lo)
    def _():
        x = xs_ref[...].astype(BF16)
        gu = jnp.dot(x, wgu_scr[...], preferred_element_type=F32)
        gate, up = gu[:, :D_EXPERT], gu[:, D_EXPERT:]
        act = (gate * _sigmoid(gate) * up).astype(BF16)
        y = jnp.dot(act, wd_scr[...], preferred_element_type=F32)
        row = lax.broadcasted_iota(jnp.int32, (GMM_TM, 1), 0)
        mine = (row >= lo) & (row < hi)

        @pl.when(new_tile)
        def _():
            ys_ref[...] = jnp.where(mine, y, 0.0)

        @pl.when(jnp.logical_not(new_tile))
        def _():
            ys_ref[...] = jnp.where(mine, y, ys_ref[...])


def _gmm(sched, xs, w_gate, w_up, w_down):
    n_visits = sched[0].shape[0]
    return pl.pallas_call(
        _gmm_kernel,
        grid_spec=pltpu.PrefetchScalarGridSpec(
            num_scalar_prefetch=4,
            grid=(n_visits,),
            in_specs=[
                pl.BlockSpec((GMM_TM, D), lambda v, e, t, lo, hi: (t[v], 0)),
                pl.BlockSpec((None, D, D_EXPERT), lambda v, e, t, lo, hi: (e[v], 0, 0)),
                pl.BlockSpec((None, D, D_EXPERT), lambda v, e, t, lo, hi: (e[v], 0, 0)),
                pl.BlockSpec((None, D_EXPERT, D), lambda v, e, t, lo, hi: (e[v], 0, 0)),
            ],
            out_specs=pl.BlockSpec((GMM_TM, D), lambda v, e, t, lo, hi: (t[v], 0)),
            scratch_shapes=[pltpu.VMEM((D, 2 * D_EXPERT), BF16), pltpu.VMEM((D_EXPERT, D), BF16)],
        ),
        out_shape=jax.ShapeDtypeStruct(xs.shape, F32),
        compiler_params=_cparams(1),
        name="experts",
    )(*sched, xs, w_gate, w_up, w_down)


def _visit_schedule(counts):
    tm = GMM_TM
    ends = jnp.cumsum(counts)
    starts = ends - counts
    first_tile = starts // tm
    last_tile = jnp.maximum(ends - 1, 0) // tm
    n_vis = jnp.where(counts > 0, last_tile - first_tile + 1, 0)
    vis_end = jnp.cumsum(n_vis)
    vis_start = vis_end - n_vis
    total = vis_end[-1]
    v = jnp.minimum(jnp.arange(N_VISITS, dtype=jnp.int32), total - 1)
    e = jnp.searchsorted(vis_end, v, side="right").astype(jnp.int32)
    tile = first_tile[e] + (v - vis_start[e])
    lo = jnp.clip(starts[e] - tile * tm, 0, tm)
    hi = jnp.clip(ends[e] - tile * tm, 0, tm)
    hi = jnp.where(jnp.arange(N_VISITS) < total, hi, lo)
    return starts.astype(jnp.int32), (e, tile.astype(jnp.int32), lo.astype(jnp.int32), hi.astype(jnp.int32))


def _moe(x, mods, layer, norm_g, router_w, router_bias, w_gate, w_up, w_down,
         ws_gate, ws_up, ws_down, final_g, final):
    tm = ROUTE_TM
    h, eidx, rank, wts, counts = _route(x, norm_g, mods, layer, router_w, router_bias)
    starts, sched = _visit_schedule(counts.reshape(N_EXPERTS).astype(jnp.int32))
    er = jnp.concatenate([eidx[:TOP_K], rank[:TOP_K]], axis=0)
    er = er.reshape(2 * TOP_K, T // tm, tm).transpose(1, 0, 2).reshape(T // tm, 1, 2 * TOP_K * tm)
    xs = _dispatch(starts, er, h)
    ys = _gmm(sched, xs, w_gate, w_up, w_down)
    n_t = T // GMM_TM
    all_rows = (jnp.zeros((n_t,), jnp.int32), jnp.arange(n_t, dtype=jnp.int32),
                jnp.zeros((n_t,), jnp.int32), jnp.full((n_t,), GMM_TM, jnp.int32))
    shared = _gmm(all_rows, h, ws_gate[None], ws_up[None], ws_down[None])
    return _combine(starts, er, x, shared, wts.T, mods, layer, final_g, ys, final)


def _pool_diff_layer(x, mods, layer, j, norm_g, cache_k, cache_v, cos, sin,
                     diff_w_in, pool_w, pool_scale, lq1, lk1, lq2, lk2, subln_g, w_out):
    lam_init = 0.8 - 0.6 * math.exp(-0.3 * layer)
    h = _norm_mod(x, norm_g, mods, layer, 0, 1)
    proj = _matmul([h], diff_w_in[j], [0], POOL_W + 3 * DIFF_W, F32, tn=1024, name="diff_in_proj")
    y_pool = _pool(proj, pool_w[j], pool_scale[j])
    lam_params = jnp.stack([lq1[j], lk1[j], lq2[j], lk2[j]])
    ck = cache_k[:, j].reshape(N_LAT_SEQ * PAST_LEN, DIFF_W)
    cv = cache_v[:, j].reshape(N_LAT_SEQ * PAST_LEN, DIFF_W)
    o = _diff_attn(proj, ck, cv, lam_params, subln_g[j], cos, sin, lam_init)
    x = _matmul([y_pool, o], w_out[j], [0, 1], D, F32, tn=1024, epilogue="residual",
                residual=x, mods=mods, layer=layer, gate_chunk=2, name="diff_out_proj")
    k_new = proj[:T_PROMPT, POOL_W + DIFF_W:POOL_W + 2 * DIFF_W]
    v_new = proj[:T_PROMPT, POOL_W + 2 * DIFF_W:]
    shape = (N_PROMPT_SEQ, PROMPT_LEN, DIFF_HEADS, 2 * DIFF_DH)
    return x, k_new.reshape(shape), v_new.reshape(shape)


def _mla_layer(x, mods, layer, j, norm_g, cache_ckv, cache_kpe, cos, sin,
               w_dq, q_norm_g, w_uq, w_dkv, kv_norm_g, w_ukv, w_o):
    h = _norm_mod(x, norm_g, mods, layer, 0, 1)
    cq = _matmul([h], w_dq[j], [0], Q_LORA, BF16, epilogue="rms", gain=q_norm_g[j], name="mla_dq")
    ckv = _matmul([h], w_dkv[j], [0], KV_LORA, F32, epilogue="rms", gain=kv_norm_g[j], name="mla_dkv")
    w_kpe = w_dkv[j][:, KV_LORA:]
    kp_dup = _matmul([h], jnp.concatenate([w_kpe, w_kpe], axis=1), [0], LANES, F32, name="mla_kpe")
    w_uq3 = w_uq[j].reshape(Q_LORA, MLA_HEADS, QK_NOPE + QK_ROPE)
    w_uq_nope = w_uq3[:, :, :QK_NOPE].reshape(Q_LORA, MLA_HEADS * QK_NOPE)
    w_uq_pe = w_uq3[:, :, QK_NOPE:].reshape(Q_LORA, MLA_HEADS * QK_ROPE)
    q_nope = _matmul([cq], w_uq_nope, [0], MLA_HEADS * QK_NOPE, BF16, tn=1024, name="mla_uq_nope")
    q_pe = _matmul([cq], w_uq_pe, [0], MLA_HEADS * QK_ROPE, F32, tn=1024, name="mla_uq_pe")
    n_kv = MLA_HEADS * (QK_NOPE + V_DIM)
    kv = _matmul([ckv], w_ukv[j], [0], n_kv, BF16, tn=1024, name="mla_ukv")
    ckv_ctx = cache_ckv[:, j].reshape(N_LAT_SEQ * PAST_LEN, KV_LORA)
    kv_ctx = _matmul([ckv_ctx], w_ukv[j], [0], n_kv, BF16, tn=1024, name="mla_ukv_ctx")
    kpe_ctx = cache_kpe[:, j].reshape(N_LAT_SEQ * PAST_LEN, QK_ROPE)
    kp_ctx_dup = jnp.concatenate([kpe_ctx, kpe_ctx], axis=1)
    o = _mla_attn(q_nope, q_pe, kv, kp_dup, kv_ctx, kp_ctx_dup, cos, sin)
    x = _matmul([o], w_o[j], [0], D, F32, tn=1024, epilogue="residual",
                residual=x, mods=mods, layer=layer, gate_chunk=2, name="mla_out_proj")
    new_ckv = ckv[:T_PROMPT].reshape(N_PROMPT_SEQ, PROMPT_LEN, KV_LORA)
    new_kpe = kp_dup[:T_PROMPT, :QK_ROPE].reshape(N_PROMPT_SEQ, PROMPT_LEN, QK_ROPE)
    return x, new_ckv, new_kpe


def kernel(x_prompt, x_sample, cache_diff_k, cache_diff_v, cache_mla_ckv, cache_mla_kpe, c, c_ctx,
           ada_w, ada_b, norm_mix_g, norm_ffn_g, final_norm_g,
           diff_w_in, pool_w, pool_scale, diff_lambda_q1, diff_lambda_k1, diff_lambda_q2, diff_lambda_k2,
           diff_subln_g, even_w_out,
           mla_w_dq, mla_q_norm_g, mla_w_uq, mla_w_dkv, mla_kv_norm_g, mla_w_ukv, mla_w_o,
           router_w, router_bias, expert_w_gate, expert_w_up, expert_w_down,
           shared_w_gate, shared_w_up, shared_w_down):
    depth = ada_w.shape[0]
    x = jnp.concatenate([x_prompt.reshape(T_PROMPT, D), x_sample.reshape(T_LAT, D)], axis=0)
    cond = jnp.concatenate(
        [c_ctx[None, :], c, jnp.zeros((N_GROUPS_PAD - 1 - N_LAT_SEQ, D), F32)], axis=0)
    mods = _ada_params(cond, ada_w, ada_b)
    cos, sin = _rope_tables()

    new_dk, new_dv, new_ckv, new_kpe = [], [], [], []
    for i in range(depth):
        j = i // 2
        if i % 2 == 0:
            x, k_new, v_new = _pool_diff_layer(
                x, mods, i, j, norm_mix_g[i], cache_diff_k, cache_diff_v, cos, sin,
                diff_w_in, pool_w, pool_scale, diff_lambda_q1, diff_lambda_k1, diff_lambda_q2,
                diff_lambda_k2, diff_subln_g, even_w_out)
            new_dk.append(k_new)
            new_dv.append(v_new)
        else:
            x, ckv, kpe = _mla_layer(
                x, mods, i, j, norm_mix_g[i], cache_mla_ckv, cache_mla_kpe, cos, sin,
                mla_w_dq, mla_q_norm_g, mla_w_uq, mla_w_dkv, mla_kv_norm_g, mla_w_ukv, mla_w_o)
            new_ckv.append(ckv)
            new_kpe.append(kpe)
        x = _moe(x, mods, i, norm_ffn_g[i], router_w[i], router_bias[i],
                 expert_w_gate[i], expert_w_up[i], expert_w_down[i],
                 shared_w_gate[i], shared_w_up[i], shared_w_down[i],
                 final_norm_g, final=(i == depth - 1))
    y_prompt = x[:T_PROMPT].reshape(N_PROMPT_SEQ, PROMPT_LEN, D)
    y_sample = x[T_PROMPT:].reshape(N_LAT_SEQ, LAT_LEN, D)
    return (y_prompt, y_sample, jnp.stack(new_dk, axis=1), jnp.stack(new_dv, axis=1),
            jnp.stack(new_ckv, axis=1), jnp.stack(new_kpe, axis=1))
```

```python
import functools
import math

import jax
import jax.numpy as jnp
from jax import lax
from jax.experimental import pallas as pl
from jax.experimental.pallas import tpu as pltpu

F32 = jnp.float32
BF16 = jnp.bfloat16

D = 2048
N_PROMPT_SEQ = 16
PROMPT_LEN = 256
N_LAT_SEQ = 4
LAT_LEN = 1024
PAST_LEN = 256
T_PROMPT = N_PROMPT_SEQ * PROMPT_LEN
T_LAT = N_LAT_SEQ * LAT_LEN
T = T_PROMPT + T_LAT
UNIT = 1024
N_UNITS = T // UNIT
N_PROMPT_UNITS = T_PROMPT // UNIT
SEQ_PER_PROMPT_UNIT = UNIT // PROMPT_LEN
N_GROUPS_PAD = 8

GRID_W = 64
ROPE_BASE = 10000.0
EPS = 1e-6
POOL_W = 1024
POOL_WINDOWS = (2, 4, 8, 16)
POOL_G = 256
POOL_HALO = 8
DIFF_W = 1024
DIFF_DH = 64
DIFF_HEADS = 8
DIFF_SCALE = DIFF_DH ** -0.5
MLA_HEADS = 16
Q_LORA = 512
KV_LORA = 512
QK_NOPE = 128
QK_ROPE = 64
V_DIM = 128
MLA_SCALE = (QK_NOPE + QK_ROPE) ** -0.5
N_EXPERTS = 64
TOP_K = 6
N_EGROUPS = 8
EGROUP = N_EXPERTS // N_EGROUPS
TOPK_GROUPS = 4
D_EXPERT = 512
ROUTED_SCALE = 2.5

LANES = 128
ATT_QB = 256
ROUTE_TM = 256
GMM_TM = 256
N_PAIRS = T * TOP_K
N_ROW_TILES = N_PAIRS // GMM_TM
N_VISITS = N_ROW_TILES + N_EXPERTS
VMEM_LIMIT = 56 * 1024 * 1024

_NT = (((1,), (1,)), ((), ()))


def _cparams(n_axes, vmem=VMEM_LIMIT):
    return pltpu.CompilerParams(dimension_semantics=("arbitrary",) * n_axes, vmem_limit_bytes=vmem)


def _group_of_tile(i, tm):
    n_p = T_PROMPT // tm
    per_seq = LAT_LEN // tm
    return jnp.where(i < n_p, 0, 1 + (i - n_p) // per_seq)


def _sigmoid(x):
    return 1.0 / (1.0 + jnp.exp(-x))


def _rms(x, g):
    return x * lax.rsqrt(jnp.mean(x * x, axis=-1, keepdims=True) + EPS) * g


def _ada_kernel(c_ref, w_ref, b_ref, o_ref):
    c = c_ref[...]
    s = c * _sigmoid(c)
    o_ref[...] = jnp.dot(s, w_ref[...], preferred_element_type=F32,
                         precision=lax.Precision.HIGHEST) + b_ref[...]


def _ada_params(cond, ada_w, ada_b):
    depth, _, n = ada_w.shape
    tn = 1024
    out = pl.pallas_call(
        _ada_kernel,
        grid=(depth, n // tn),
        in_specs=[
            pl.BlockSpec((N_GROUPS_PAD, D), lambda l, j: (0, 0)),
            pl.BlockSpec((None, D, tn), lambda l, j: (l, 0, j)),
            pl.BlockSpec((None, 1, tn), lambda l, j: (l, 0, j)),
        ],
        out_specs=pl.BlockSpec((None, N_GROUPS_PAD, tn), lambda l, j: (l, 0, j)),
        out_shape=jax.ShapeDtypeStruct((depth, N_GROUPS_PAD, n), F32),
        compiler_params=_cparams(2),
        name="ada_params",
    )(cond, ada_w, ada_b.reshape(depth, 1, n))
    return out.reshape(depth, N_GROUPS_PAD, 1, n)


def _mod_spec(layer, chunk, tm, grid_rank, row_axis):
    def index_map(*ids):
        return (layer, _group_of_tile(ids[row_axis], tm), 0, chunk)
    del grid_rank
    return pl.BlockSpec((None, None, 1, D), index_map)


def _norm_mod_kernel(x_ref, g_ref, sh_ref, sc_ref, o_ref):
    y = _rms(x_ref[...], g_ref[...])
    o_ref[...] = (y * (1.0 + sc_ref[...]) + sh_ref[...]).astype(o_ref.dtype)


def _norm_mod(x, g, mods, layer, shift_chunk, scale_chunk):
    tm = 512
    return pl.pallas_call(
        _norm_mod_kernel,
        grid=(T // tm,),
        in_specs=[
            pl.BlockSpec((tm, D), lambda i: (i, 0)),
            pl.BlockSpec((1, D), lambda i: (0, 0)),
            _mod_spec(layer, shift_chunk, tm, 1, 0),
            _mod_spec(layer, scale_chunk, tm, 1, 0),
        ],
        out_specs=pl.BlockSpec((tm, D), lambda i: (i, 0)),
        out_shape=jax.ShapeDtypeStruct((T, D), BF16),
        compiler_params=_cparams(1),
        name="norm_mod",
    )(x, g.reshape(1, D), mods, mods)


def _mm_kernel(*refs, n_a, epilogue):
    a_refs = refs[:n_a]
    w_refs = refs[n_a:2 * n_a]
    pos = 2 * n_a
    if epilogue == "rms":
        g_ref = refs[pos]
        pos += 1
    elif epilogue == "residual":
        res_ref, gate_ref = refs[pos], refs[pos + 1]
        pos += 2
    o_ref = refs[pos]
    w_scr = refs[pos + 1:]

    @pl.when(pl.program_id(1) == 0)
    def _():
        for w_ref, scr in zip(w_refs, w_scr):
            scr[...] = w_ref[...].astype(BF16)

    acc = None
    for a_ref, scr in zip(a_refs, w_scr):
        part = jnp.dot(a_ref[...].astype(BF16), scr[...], preferred_element_type=F32)
        acc = part if acc is None else acc + part
    if epilogue == "rms":
        acc = _rms(acc, g_ref[...])
    elif epilogue == "residual":
        acc = res_ref[...] + gate_ref[...] * acc
    o_ref[...] = acc.astype(o_ref.dtype)


def _matmul(a_list, w, w_row_blocks, n_out, out_dtype, *, tm=512, tn=None, epilogue=None,
            gain=None, residual=None, mods=None, layer=None, gate_chunk=None, name="matmul"):
    m = a_list[0].shape[0]
    tn = n_out if tn is None else tn
    n_a = len(a_list)
    in_specs, args = [], []
    for a in a_list:
        k = a.shape[1]
        in_specs.append(pl.BlockSpec((tm, k), lambda j, i: (i, 0)))
        args.append(a)
    scratch = []
    for a, rb in zip(a_list, w_row_blocks):
        k = a.shape[1]
        in_specs.append(pl.BlockSpec((k, tn), lambda j, i, rb=rb: (rb, j)))
        args.append(w)
        scratch.append(pltpu.VMEM((k, tn), BF16))
    if epilogue == "rms":
        assert tn == n_out
        in_specs.append(pl.BlockSpec((1, tn), lambda j, i: (0, 0)))
        args.append(gain.reshape(1, n_out))
    elif epilogue == "residual":
        in_specs.append(pl.BlockSpec((tm, tn), lambda j, i: (i, j)))
        args.append(residual)
        in_specs.append(pl.BlockSpec(
            (None, None, 1, tn),
            lambda j, i: (layer, _group_of_tile(i, tm), 0, gate_chunk * (D // tn) + j)))
        args.append(mods)
    return pl.pallas_call(
        functools.partial(_mm_kernel, n_a=n_a, epilogue=epilogue),
        grid=(n_out // tn, m // tm),
        in_specs=in_specs,
        out_specs=pl.BlockSpec((tm, tn), lambda j, i: (i, j)),
        out_shape=jax.ShapeDtypeStruct((m, n_out), out_dtype),
        scratch_shapes=scratch,
        compiler_params=_cparams(2),
        name=name,
    )(*args)


def _pool_seq(u_ref, row0, seq_len, pw_ref, scale_ref, o_ref, pad_ref):
    t = lax.broadcasted_iota(jnp.int32, (seq_len, 1), 0)
    zeros = jnp.zeros((POOL_HALO, POOL_G), F32)
    for g, w in enumerate(POOL_WINDOWS):
        cols = slice(g * POOL_G, (g + 1) * POOL_G)
        ug = u_ref[row0:row0 + seq_len, cols]
        pad_ref[0:POOL_HALO, :] = zeros
        pad_ref[POOL_HALO:POOL_HALO + seq_len, :] = ug
        pad_ref[POOL_HALO + seq_len:2 * POOL_HALO + seq_len, :] = zeros
        total = None
        for off in range(-(w // 2), w // 2):
            part = pad_ref[POOL_HALO + off:POOL_HALO + off + seq_len, :]
            total = part if total is None else total + part
        cnt = (jnp.minimum(t + w // 2, seq_len) - jnp.maximum(t - w // 2, 0)).astype(F32)
        d = (total / cnt - ug).astype(BF16)
        y = jnp.dot(d, pw_ref[g].astype(BF16), preferred_element_type=F32)
        o_ref[row0:row0 + seq_len, cols] = (y * scale_ref[:, cols]).astype(o_ref.dtype)


def _pool_kernel(u_ref, pw_ref, scale_ref, o_ref, pad_ref):
    unit = pl.program_id(0)

    @pl.when(unit < N_PROMPT_UNITS)
    def _():
        for s in range(SEQ_PER_PROMPT_UNIT):
            _pool_seq(u_ref, s * PROMPT_LEN, PROMPT_LEN, pw_ref, scale_ref, o_ref, pad_ref)

    @pl.when(unit >= N_PROMPT_UNITS)
    def _():
        _pool_seq(u_ref, 0, LAT_LEN, pw_ref, scale_ref, o_ref, pad_ref)


def _pool(proj, pool_w, pool_scale):
    return pl.pallas_call(
        _pool_kernel,
        grid=(N_UNITS,),
        in_specs=[
            pl.BlockSpec((UNIT, POOL_W), lambda u: (u, 0)),
            pl.BlockSpec((len(POOL_WINDOWS), POOL_G, POOL_G), lambda u: (0, 0, 0)),
            pl.BlockSpec((1, POOL_W), lambda u: (0, 0)),
        ],
        out_specs=pl.BlockSpec((UNIT, POOL_W), lambda u: (u, 0)),
        out_shape=jax.ShapeDtypeStruct((T, POOL_W), BF16),
        scratch_shapes=[pltpu.VMEM((LAT_LEN + 2 * POOL_HALO, POOL_G), F32)],
        compiler_params=_cparams(1),
        name="pool",
    )(proj, pool_w, pool_scale.reshape(1, POOL_W))


def _rope_tables():
    pos = jnp.arange(LAT_LEN)
    row = (pos // GRID_W).astype(F32)
    col = (pos % GRID_W).astype(F32)
    quarter = QK_ROPE // 4
    inv = ROPE_BASE ** (-jnp.arange(quarter, dtype=F32) / quarter)
    lane = jnp.arange(LANES)
    axis = (lane % QK_ROPE) // (QK_ROPE // 2)
    freq = inv[lane % quarter]
    p = jnp.where(axis[None, :] == 0, row[:, None], col[:, None])
    ang = p * freq[None, :]
    sign = jnp.where((lane % (QK_ROPE // 2)) < quarter, -1.0, 1.0).astype(F32)
    return jnp.cos(ang), jnp.sin(ang) * sign[None, :]


def _rope(x, cos, sin_signed):
    quarter = QK_ROPE // 4
    lane = lax.broadcasted_iota(jnp.int32, x.shape, 1)
    first_half = (lane % (QK_ROPE // 2)) < quarter
    partner = jnp.where(first_half, pltpu.roll(x, LANES - quarter, 1), pltpu.roll(x, quarter, 1))
    return x * cos + partner * sin_signed


def _diff_core(q, k, v, lam, g, lam_init):
    lane = lax.broadcasted_iota(jnp.int32, q.shape, 1)
    low = lane < DIFF_DH
    q1 = jnp.where(low, q, 0.0).astype(BF16)
    q2 = jnp.where(low, 0.0, q).astype(BF16)

    def probs(qm):
        s = lax.dot_general(qm, k, _NT, preferred_element_type=F32)
        e = jnp.exp(s - jnp.max(s, axis=-1, keepdims=True))
        return e * (1.0 / jnp.sum(e, axis=-1, keepdims=True))

    w = (probs(q1) - lam * probs(q2)).astype(BF16)
    o = jnp.dot(w, v, preferred_element_type=F32)
    return _rms(o, g) * (1.0 - lam_init)


def _diff_attn_kernel(lp_ref, q_ref, k_ref, v_ref, ck_ref, cv_ref, cos_ref, sin_ref, g_ref, o_ref,
                      *, lam_init):
    unit = pl.program_id(0)
    lp = lp_ref[...]
    lam = (jnp.exp(jnp.sum(lp[0:1] * lp[1:2], axis=-1, keepdims=True))
           - jnp.exp(jnp.sum(lp[2:3] * lp[3:4], axis=-1, keepdims=True)) + lam_init)
    g = g_ref[...]

    @pl.when(unit < N_PROMPT_UNITS)
    def _():
        for s in range(SEQ_PER_PROMPT_UNIT):
            rows = slice(s * PROMPT_LEN, (s + 1) * PROMPT_LEN)
            q = q_ref[rows, :] * DIFF_SCALE
            k = k_ref[rows, :].astype(BF16)
            v = v_ref[rows, :].astype(BF16)
            o_ref[rows, :] = _diff_core(q, k, v, lam, g, lam_init).astype(o_ref.dtype)

    @pl.when(unit >= N_PROMPT_UNITS)
    def _():
        cos, sin = cos_ref[...], sin_ref[...]
        k = jnp.concatenate([ck_ref[...].astype(BF16),
                             _rope(k_ref[...], cos, sin).astype(BF16)], axis=0)
        v = jnp.concatenate([cv_ref[...].astype(BF16), v_ref[...].astype(BF16)], axis=0)
        for b in range(LAT_LEN // ATT_QB):
            rows = slice(b * ATT_QB, (b + 1) * ATT_QB)
            q = _rope(q_ref[rows, :], cos[rows], sin[rows]) * DIFF_SCALE
            o_ref[rows, :] = _diff_core(q, k, v, lam, g, lam_init).astype(o_ref.dtype)


def _diff_attn(proj, cache_k, cache_v, lam_params, subln_g, cos, sin, lam_init):
    hd = 2 * DIFF_DH
    q0, k0, v0 = POOL_W // hd, (POOL_W + DIFF_W) // hd, (POOL_W + 2 * DIFF_W) // hd

    def ctx_map(u, h):
        return (jnp.maximum(u - N_PROMPT_UNITS, 0), h)

    return pl.pallas_call(
        functools.partial(_diff_attn_kernel, lam_init=lam_init),
        grid=(N_UNITS, DIFF_HEADS),
        in_specs=[
            pl.BlockSpec((4, DIFF_DH), lambda u, h: (0, 0)),
            pl.BlockSpec((UNIT, hd), lambda u, h: (u, q0 + h)),
            pl.BlockSpec((UNIT, hd), lambda u, h: (u, k0 + h)),
            pl.BlockSpec((UNIT, hd), lambda u, h: (u, v0 + h)),
            pl.BlockSpec((PAST_LEN, hd), ctx_map),
            pl.BlockSpec((PAST_LEN, hd), ctx_map),
            pl.BlockSpec((LAT_LEN, LANES), lambda u, h: (0, 0)),
            pl.BlockSpec((LAT_LEN, LANES), lambda u, h: (0, 0)),
            pl.BlockSpec((1, hd), lambda u, h: (0, 0)),
        ],
        out_specs=pl.BlockSpec((UNIT, hd), lambda u, h: (u, h)),
        out_shape=jax.ShapeDtypeStruct((T, DIFF_W), BF16),
        compiler_params=_cparams(2),
        name="diff_attn",
    )(lam_params, proj, proj, proj, cache_k, cache_v, cos, sin, subln_g.reshape(1, hd))


def _mla_core(qc, kc, v):
    s = lax.dot_general(qc, kc, _NT, preferred_element_type=F32) * MLA_SCALE
    e = jnp.exp(s - jnp.max(s, axis=-1, keepdims=True))
    p = (e * (1.0 / jnp.sum(e, axis=-1, keepdims=True))).astype(BF16)
    return jnp.dot(p, v, preferred_element_type=F32)


def _mla_attn_kernel(qn_ref, qp_ref, kn_ref, v_ref, kp_ref, ckn_ref, cv_ref, ckp_ref, cos_ref, sin_ref,
                     o_ref):
    unit = pl.program_id(0)
    head = pl.program_id(1)
    lane = lax.broadcasted_iota(jnp.int32, (ATT_QB, LANES), 1)
    mine = (lane // QK_ROPE) == (head % 2)

    @pl.when(unit < N_PROMPT_UNITS)
    def _():
        for s in range(SEQ_PER_PROMPT_UNIT):
            rows = slice(s * PROMPT_LEN, (s + 1) * PROMPT_LEN)
            qp = jnp.where(mine, qp_ref[rows, :], 0.0).astype(BF16)
            qc = jnp.concatenate([qn_ref[rows, :], qp], axis=1)
            kc = jnp.concatenate([kn_ref[rows, :], kp_ref[rows, :].astype(BF16)], axis=1)
            o_ref[rows, :] = _mla_core(qc, kc, v_ref[rows, :]).astype(o_ref.dtype)

    @pl.when(unit >= N_PROMPT_UNITS)
    def _():
        cos, sin = cos_ref[...], sin_ref[...]
        k_ctx = jnp.concatenate([ckn_ref[...], ckp_ref[...].astype(BF16)], axis=1)
        k_new = jnp.concatenate([kn_ref[...], _rope(kp_ref[...], cos, sin).astype(BF16)], axis=1)
        kc = jnp.concatenate([k_ctx, k_new], axis=0)
        v = jnp.concatenate([cv_ref[...], v_ref[...]], axis=0)
        for b in range(LAT_LEN // ATT_QB):
            rows = slice(b * ATT_QB, (b + 1) * ATT_QB)
            qp = jnp.where(mine, _rope(qp_ref[rows, :], cos[rows], sin[rows]), 0.0).astype(BF16)
            qc = jnp.concatenate([qn_ref[rows, :], qp], axis=1)
            o_ref[rows, :] = _mla_core(qc, kc, v).astype(o_ref.dtype)


def _mla_attn(q_nope, q_pe, kv, kp_dup, kv_ctx, kp_ctx_dup, cos, sin):
    def ctx_row(u):
        return jnp.maximum(u - N_PROMPT_UNITS, 0)

    return pl.pallas_call(
        _mla_attn_kernel,
        grid=(N_UNITS, MLA_HEADS),
        in_specs=[
            pl.BlockSpec((UNIT, QK_NOPE), lambda u, h: (u, h)),
            pl.BlockSpec((UNIT, LANES), lambda u, h: (u, h // 2)),
            pl.BlockSpec((UNIT, QK_NOPE), lambda u, h: (u, 2 * h)),
            pl.BlockSpec((UNIT, V_DIM), lambda u, h: (u, 2 * h + 1)),
            pl.BlockSpec((UNIT, LANES), lambda u, h: (u, 0)),
            pl.BlockSpec((PAST_LEN, QK_NOPE), lambda u, h: (ctx_row(u), 2 * h)),
            pl.BlockSpec((PAST_LEN, V_DIM), lambda u, h: (ctx_row(u), 2 * h + 1)),
            pl.BlockSpec((PAST_LEN, LANES), lambda u, h: (ctx_row(u), 0)),
            pl.BlockSpec((LAT_LEN, LANES), lambda u, h: (0, 0)),
            pl.BlockSpec((LAT_LEN, LANES), lambda u, h: (0, 0)),
        ],
        out_specs=pl.BlockSpec((UNIT, V_DIM), lambda u, h: (u, h)),
        out_shape=jax.ShapeDtypeStruct((T, MLA_HEADS * V_DIM), BF16),
        compiler_params=_cparams(2),
        name="mla_attn",
    )(q_nope, q_pe, kv, kv, kp_dup, kv_ctx, kv_ctx, kp_ctx_dup, cos, sin)


def _route_kernel(x_ref, g_ref, sh_ref, sc_ref, rw_ref, rb_ref,
                  h_ref, eidx_ref, rank_ref, wts_ref, cnt_ref, carry_ref):
    tm = ROUTE_TM

    @pl.when(pl.program_id(0) == 0)
    def _():
        carry_ref[...] = jnp.zeros_like(carry_ref)

    h = _rms(x_ref[...], g_ref[...]) * (1.0 + sc_ref[...]) + sh_ref[...]
    h_ref[...] = h
    logits = lax.dot_general(rw_ref[...], h, _NT, preferred_element_type=F32,
                             precision=lax.Precision.HIGHEST)
    scores = _sigmoid(logits)
    choice = scores + rb_ref[...]
    neg = -jnp.inf

    def take_max(vals, iota, n):
        m = jnp.max(vals, axis=0, keepdims=True)
        idx = jnp.min(jnp.where(vals == m, iota, n), axis=0, keepdims=True)
        return iota == idx, m, idx

    iota8 = lax.broadcasted_iota(jnp.int32, (8, tm), 0)

    def stack_rows(rows, dtype):
        out = jnp.zeros((8, tm), dtype)
        for k, r in enumerate(rows):
            out = jnp.where(iota8 == k, r.astype(dtype), out)
        return out

    gscore = []
    for gi in range(N_EGROUPS):
        grp = choice[gi * EGROUP:(gi + 1) * EGROUP]
        oh, m1, _ = take_max(grp, iota8, EGROUP)
        m2 = jnp.max(jnp.where(oh, neg, grp), axis=0, keepdims=True)
        gscore.append(m1 + m2)
    cur = stack_rows(gscore, F32)
    gsel = jnp.zeros((N_EGROUPS, tm), F32)
    for _ in range(TOPK_GROUPS):
        oh, _, _ = take_max(cur, iota8, N_EGROUPS)
        gsel = jnp.where(oh, 1.0, gsel)
        cur = jnp.where(oh, neg, cur)
    emask = jnp.concatenate(
        [jnp.broadcast_to(gsel[gi:gi + 1], (EGROUP, tm)) for gi in range(N_EGROUPS)], axis=0)
    masked = jnp.where(emask > 0.0, choice, neg)

    iota_e = lax.broadcasted_iota(jnp.int32, (N_EXPERTS, tm), 0)
    onehots, idxs, wsel = [], [], []
    sel = jnp.zeros((N_EXPERTS, tm), F32)
    for _ in range(TOP_K):
        oh, _, idx = take_max(masked, iota_e, N_EXPERTS)
        onehots.append(oh)
        idxs.append(idx)
        wsel.append(jnp.sum(jnp.where(oh, scores, 0.0), axis=0, keepdims=True))
        masked = jnp.where(oh, neg, masked)
        sel = jnp.where(oh, 1.0, sel)
    wsum = wsel[0]
    for w in wsel[1:]:
        wsum = wsum + w

    before = jnp.where(lax.broadcasted_iota(jnp.int32, (tm, tm), 0)
                       < lax.broadcasted_iota(jnp.int32, (tm, tm), 1), 1.0, 0.0).astype(BF16)
    rank_all = carry_ref[...] + jnp.dot(sel.astype(BF16), before, preferred_element_type=F32)
    carry_ref[...] = carry_ref[...] + jnp.sum(sel, axis=1, keepdims=True)
    cnt_ref[...] = carry_ref[...]

    ranks = [jnp.sum(jnp.where(oh, rank_all, 0.0), axis=0, keepdims=True) for oh in onehots]
    eidx_ref[...] = stack_rows(idxs, jnp.int32)
    rank_ref[...] = stack_rows(ranks, jnp.int32)
    wts_ref[...] = stack_rows([w / wsum * ROUTED_SCALE for w in wsel], F32)


def _route(x, g, mods, layer, router_w, router_bias):
    tm = ROUTE_TM
    return pl.pallas_call(
        _route_kernel,
        grid=(T // tm,),
        in_specs=[
            pl.BlockSpec((tm, D), lambda i: (i, 0)),
            pl.BlockSpec((1, D), lambda i: (0, 0)),
            _mod_spec(layer, 3, tm, 1, 0),
            _mod_spec(layer, 4, tm, 1, 0),
            pl.BlockSpec((N_EXPERTS, D), lambda i: (0, 0)),
            pl.BlockSpec((N_EXPERTS, 1), lambda i: (0, 0)),
        ],
        out_specs=[
            pl.BlockSpec((tm, D), lambda i: (i, 0)),
            pl.BlockSpec((8, tm), lambda i: (0, i)),
            pl.BlockSpec((8, tm), lambda i: (0, i)),
            pl.BlockSpec((8, tm), lambda i: (0, i)),
            pl.BlockSpec((N_EXPERTS, 1), lambda i: (0, 0)),
        ],
        out_shape=[
            jax.ShapeDtypeStruct((T, D), F32),
            jax.ShapeDtypeStruct((8, T), jnp.int32),
            jax.ShapeDtypeStruct((8, T), jnp.int32),
            jax.ShapeDtypeStruct((8, T), F32),
            jax.ShapeDtypeStruct((N_EXPERTS, 1), F32),
        ],
        scratch_shapes=[pltpu.VMEM((N_EXPERTS, 1), F32)],
        compiler_params=_cparams(1),
        name="route",
    )(x, g.reshape(1, D), mods, mods, router_w.T, router_bias.reshape(N_EXPERTS, 1))


def _row_copy(src, src_row, dst, dst_row, sem):
    return pltpu.make_async_copy(src.at[pl.ds(src_row, 1), :], dst.at[pl.ds(dst_row, 1), :], sem)


def _pair_slot(starts_ref, er_ref, j, k):
    e = er_ref[0, 0, k * ROUTE_TM + j]
    r = er_ref[0, 0, (TOP_K + k) * ROUTE_TM + j]
    return starts_ref[e] + r


def _dispatch_kernel(starts_ref, er_ref, h_ref, xs_ref, sem):
    def issue(j, carry):
        for k in range(TOP_K):
            _row_copy(h_ref, j, xs_ref, _pair_slot(starts_ref, er_ref, j, k), sem).start()
        return carry

    def drain(j, carry):
        for k in range(TOP_K):
            _row_copy(h_ref, 0, xs_ref, 0, sem).wait()
        return carry

    lax.fori_loop(0, ROUTE_TM, issue, 0)
    lax.fori_loop(0, ROUTE_TM, drain, 0)


def _dispatch(starts, er, h):
    tm = ROUTE_TM
    return pl.pallas_call(
        _dispatch_kernel,
        grid_spec=pltpu.PrefetchScalarGridSpec(
            num_scalar_prefetch=1,
            grid=(T // tm,),
            in_specs=[
                pl.BlockSpec((1, 1, 2 * TOP_K * tm), lambda i, s: (i, 0, 0), memory_space=pltpu.SMEM),
                pl.BlockSpec((tm, D), lambda i, s: (i, 0)),
            ],
            out_specs=pl.BlockSpec(memory_space=pl.ANY),
            scratch_shapes=[pltpu.SemaphoreType.DMA(())],
        ),
        out_shape=jax.ShapeDtypeStruct((N_PAIRS, D), F32),
        compiler_params=_cparams(1),
        name="dispatch",
    )(starts, er, h)


def _combine_kernel(starts_ref, er_ref, x_ref, sh_ref, w_ref, gate_ref, fg_ref, ys_ref, o_ref,
                    buf_ref, sem, *, final):
    def issue(j, carry):
        for k in range(TOP_K):
            _row_copy(ys_ref, _pair_slot(starts_ref, er_ref, j, k), buf_ref.at[k], j, sem).start()
        return carry

    def drain(j, carry):
        for k in range(TOP_K):
            _row_copy(ys_ref, 0, buf_ref.at[k], 0, sem).wait()
        return carry

    lax.fori_loop(0, ROUTE_TM, issue, 0)
    lax.fori_loop(0, ROUTE_TM, drain, 0)
    w = w_ref[...]
    acc = None
    for k in range(TOP_K):
        term = w[:, k:k + 1] * buf_ref[k]
        acc = term if acc is None else acc + term
    xn = x_ref[...] + gate_ref[...] * (acc + sh_ref[...])
    if final:
        xn = _rms(xn, fg_ref[...])
    o_ref[...] = xn


def _combine(starts, er, x, shared, wts_t, mods, layer, final_g, ys, final):
    tm = ROUTE_TM
    return pl.pallas_call(
        functools.partial(_combine_kernel, final=final),
        grid_spec=pltpu.PrefetchScalarGridSpec(
            num_scalar_prefetch=1,
            grid=(T // tm,),
            in_specs=[
                pl.BlockSpec((1, 1, 2 * TOP_K * tm), lambda i, s: (i, 0, 0), memory_space=pltpu.SMEM),
                pl.BlockSpec((tm, D), lambda i, s: (i, 0)),
                pl.BlockSpec((tm, D), lambda i, s: (i, 0)),
                pl.BlockSpec((tm, 8), lambda i, s: (i, 0)),
                pl.BlockSpec((None, None, 1, D), lambda i, s: (layer, _group_of_tile(i, tm), 0, 5)),
                pl.BlockSpec((1, D), lambda i, s: (0, 0)),
                pl.BlockSpec(memory_space=pl.ANY),
            ],
            out_specs=pl.BlockSpec((tm, D), lambda i, s: (i, 0)),
            scratch_shapes=[pltpu.VMEM((TOP_K, tm, D), F32), pltpu.SemaphoreType.DMA(())],
        ),
        out_shape=jax.ShapeDtypeStruct((T, D), F32),
        compiler_params=_cparams(1),
        name="combine",
    )(starts, er, x, shared, wts_t, mods, final_g.reshape(1, D), ys)


def _gmm_kernel(e_ref, t_ref, lo_ref, hi_ref, xs_ref, wg_ref, wu_ref, wd_ref, ys_ref, wgu_scr, wd_scr):
    v = pl.program_id(0)
    prev = jnp.maximum(v - 1, 0)
    new_expert = (v == 0) | (e_ref[v] != e_ref[prev])
    new_tile = (v == 0) | (t_ref[v] != t_ref[prev])
    lo, hi = lo_ref[v], hi_ref[v]

    @pl.when(new_expert)
    def _():
        wgu_scr[:, :D_EXPERT] = wg_ref[...].astype(BF16)
        wgu_scr[:, D_EXPERT:] = wu_ref[...].astype(BF16)
        wd_scr[...] = wd_ref[...].astype(BF16)

    @pl.when(hi > lo)
    def _():
        x = xs_ref[...].astype(BF16)
        gu = jnp.dot(x, wgu_scr[...], preferred_element_type=F32)
        gate, up = gu[:, :D_EXPERT], gu[:, D_EXPERT:]
        act = (gate * _sigmoid(gate) * up).astype(BF16)
        y = jnp.dot(act, wd_scr[...], preferred_element_type=F32)
        row = lax.broadcasted_iota(jnp.int32, (GMM_TM, 1), 0)
        mine = (row >= lo) & (row < hi)

        @pl.when(new_tile)
        def _():
            ys_ref[...] = jnp.where(mine, y, 0.0)

        @pl.when(jnp.logical_not(new_tile))
        def _():
            ys_ref[...] = jnp.where(mine, y, ys_ref[...])


def _gmm(sched, xs, layer, w_gate, w_up, w_down):
    n_visits = sched[0].shape[0]
    return pl.pallas_call(
        _gmm_kernel,
        grid_spec=pltpu.PrefetchScalarGridSpec(
            num_scalar_prefetch=4,
            grid=(n_visits,),
            in_specs=[
                pl.BlockSpec((GMM_TM, D), lambda v, e, t, lo, hi: (t[v], 0)),
                pl.BlockSpec((None, None, D, D_EXPERT), lambda v, e, t, lo, hi: (layer, e[v], 0, 0)),
                pl.BlockSpec((None, None, D, D_EXPERT), lambda v, e, t, lo, hi: (layer, e[v], 0, 0)),
                pl.BlockSpec((None, None, D_EXPERT, D), lambda v, e, t, lo, hi: (layer, e[v], 0, 0)),
            ],
            out_specs=pl.BlockSpec((GMM_TM, D), lambda v, e, t, lo, hi: (t[v], 0)),
            scratch_shapes=[pltpu.VMEM((D, 2 * D_EXPERT), BF16), pltpu.VMEM((D_EXPERT, D), BF16)],
        ),
        out_shape=jax.ShapeDtypeStruct(xs.shape, F32),
        compiler_params=_cparams(1),
        name="experts",
    )(*sched, xs, w_gate, w_up, w_down)


def _visit_schedule(counts):
    tm = GMM_TM
    ends = jnp.cumsum(counts)
    starts = ends - counts
    first_tile = starts // tm
    last_tile = jnp.maximum(ends - 1, 0) // tm
    n_vis = jnp.where(counts > 0, last_tile - first_tile + 1, 0)
    vis_end = jnp.cumsum(n_vis)
    vis_start = vis_end - n_vis
    total = vis_end[-1]
    v = jnp.minimum(jnp.arange(N_VISITS, dtype=jnp.int32), total - 1)
    e = jnp.sum((vis_end[None, :] <= v[:, None]).astype(jnp.int32), axis=1)
    tile = first_tile[e] + (v - vis_start[e])
    lo = jnp.clip(starts[e] - tile * tm, 0, tm)
    hi = jnp.clip(ends[e] - tile * tm, 0, tm)
    hi = jnp.where(jnp.arange(N_VISITS) < total, hi, lo)
    return starts.astype(jnp.int32), (e, tile.astype(jnp.int32), lo.astype(jnp.int32), hi.astype(jnp.int32))


def _moe(x, mods, layer, norm_g, router_w, router_bias, w_gate, w_up, w_down,
         ws_gate, ws_up, ws_down, final_g, final):
    tm = ROUTE_TM
    h, eidx, rank, wts, counts = _route(x, norm_g, mods, layer, router_w, router_bias)
    starts, sched = _visit_schedule(counts.reshape(N_EXPERTS).astype(jnp.int32))
    er = jnp.concatenate([eidx[:TOP_K], rank[:TOP_K]], axis=0)
    er = er.reshape(2 * TOP_K, T // tm, tm).transpose(1, 0, 2).reshape(T // tm, 1, 2 * TOP_K * tm)
    xs = _dispatch(starts, er, h)
    ys = _gmm(sched, xs, layer, w_gate, w_up, w_down)
    n_t = T // GMM_TM
    all_rows = (jnp.zeros((n_t,), jnp.int32), jnp.arange(n_t, dtype=jnp.int32),
                jnp.zeros((n_t,), jnp.int32), jnp.full((n_t,), GMM_TM, jnp.int32))
    shared = _gmm(all_rows, h, layer, ws_gate[:, None], ws_up[:, None], ws_down[:, None])
    return _combine(starts, er, x, shared, wts.T, mods, layer, final_g, ys, final)


def _pool_diff_layer(x, mods, layer, j, norm_g, cache_k, cache_v, cos, sin,
                     diff_w_in, pool_w, pool_scale, lq1, lk1, lq2, lk2, subln_g, w_out):
    lam_init = 0.8 - 0.6 * math.exp(-0.3 * layer)
    h = _norm_mod(x, norm_g, mods, layer, 0, 1)
    proj = _matmul([h], diff_w_in[j], [0], POOL_W + 3 * DIFF_W, F32, tn=1024, name="diff_in_proj")
    y_pool = _pool(proj, pool_w[j], pool_scale[j])
    lam_params = jnp.stack([lq1[j], lk1[j], lq2[j], lk2[j]])
    ck = cache_k[:, j].reshape(N_LAT_SEQ * PAST_LEN, DIFF_W)
    cv = cache_v[:, j].reshape(N_LAT_SEQ * PAST_LEN, DIFF_W)
    o = _diff_attn(proj, ck, cv, lam_params, subln_g[j], cos, sin, lam_init)
    x = _matmul([y_pool, o], w_out[j], [0, 1], D, F32, tn=1024, epilogue="residual",
                residual=x, mods=mods, layer=layer, gate_chunk=2, name="diff_out_proj")
    k_new = proj[:T_PROMPT, POOL_W + DIFF_W:POOL_W + 2 * DIFF_W]
    v_new = proj[:T_PROMPT, POOL_W + 2 * DIFF_W:]
    shape = (N_PROMPT_SEQ, PROMPT_LEN, DIFF_HEADS, 2 * DIFF_DH)
    return x, k_new.reshape(shape), v_new.reshape(shape)


def _mla_layer(x, mods, layer, j, norm_g, cache_ckv, cache_kpe, cos, sin,
               w_dq, q_norm_g, w_uq, w_dkv, kv_norm_g, w_ukv, w_o):
    h = _norm_mod(x, norm_g, mods, layer, 0, 1)
    cq = _matmul([h], w_dq[j], [0], Q_LORA, BF16, epilogue="rms", gain=q_norm_g[j], name="mla_dq")
    ckv = _matmul([h], w_dkv[j], [0], KV_LORA, F32, epilogue="rms", gain=kv_norm_g[j], name="mla_dkv")
    w_kpe = w_dkv[j][:, KV_LORA:]
    kp_dup = _matmul([h], jnp.concatenate([w_kpe, w_kpe], axis=1), [0], LANES, F32, name="mla_kpe")
    w_uq3 = w_uq[j].reshape(Q_LORA, MLA_HEADS, QK_NOPE + QK_ROPE)
    w_uq_nope = w_uq3[:, :, :QK_NOPE].reshape(Q_LORA, MLA_HEADS * QK_NOPE)
    w_uq_pe = w_uq3[:, :, QK_NOPE:].reshape(Q_LORA, MLA_HEADS * QK_ROPE)
    q_nope = _matmul([cq], w_uq_nope, [0], MLA_HEADS * QK_NOPE, BF16, tn=1024, name="mla_uq_nope")
    q_pe = _matmul([cq], w_uq_pe, [0], MLA_HEADS * QK_ROPE, F32, tn=1024, name="mla_uq_pe")
    n_kv = MLA_HEADS * (QK_NOPE + V_DIM)
    kv = _matmul([ckv], w_ukv[j], [0], n_kv, BF16, tn=1024, name="mla_ukv")
    ckv_ctx = cache_ckv[:, j].reshape(N_LAT_SEQ * PAST_LEN, KV_LORA)
    kv_ctx = _matmul([ckv_ctx], w_ukv[j], [0], n_kv, BF16, tn=1024, name="mla_ukv_ctx")
    kpe_ctx = cache_kpe[:, j].reshape(N_LAT_SEQ * PAST_LEN, QK_ROPE)
    kp_ctx_dup = jnp.concatenate([kpe_ctx, kpe_ctx], axis=1)
    o = _mla_attn(q_nope, q_pe, kv, kp_dup, kv_ctx, kp_ctx_dup, cos, sin)
    x = _matmul([o], w_o[j], [0], D, F32, tn=1024, epilogue="residual",
                residual=x, mods=mods, layer=layer, gate_chunk=2, name="mla_out_proj")
    new_ckv = ckv[:T_PROMPT].reshape(N_PROMPT_SEQ, PROMPT_LEN, KV_LORA)
    new_kpe = kp_dup[:T_PROMPT, :QK_ROPE].reshape(N_PROMPT_SEQ, PROMPT_LEN, QK_ROPE)
    return x, new_ckv, new_kpe


def kernel(x_prompt, x_sample, cache_diff_k, cache_diff_v, cache_mla_ckv, cache_mla_kpe, c, c_ctx,
           ada_w, ada_b, norm_mix_g, norm_ffn_g, final_norm_g,
           diff_w_in, pool_w, pool_scale, diff_lambda_q1, diff_lambda_k1, diff_lambda_q2, diff_lambda_k2,
           diff_subln_g, even_w_out,
           mla_w_dq, mla_q_norm_g, mla_w_uq, mla_w_dkv, mla_kv_norm_g, mla_w_ukv, mla_w_o,
           router_w, router_bias, expert_w_gate, expert_w_up, expert_w_down,
           shared_w_gate, shared_w_up, shared_w_down):
    depth = ada_w.shape[0]
    x = jnp.concatenate([x_prompt.reshape(T_PROMPT, D), x_sample.reshape(T_LAT, D)], axis=0)
    cond = jnp.concatenate(
        [c_ctx[None, :], c, jnp.zeros((N_GROUPS_PAD - 1 - N_LAT_SEQ, D), F32)], axis=0)
    mods = _ada_params(cond, ada_w, ada_b)
    cos, sin = _rope_tables()

    new_dk, new_dv, new_ckv, new_kpe = [], [], [], []
    for i in range(depth):
        j = i // 2
        if i % 2 == 0:
            x, k_new, v_new = _pool_diff_layer(
                x, mods, i, j, norm_mix_g[i], cache_diff_k, cache_diff_v, cos, sin,
                diff_w_in, pool_w, pool_scale, diff_lambda_q1, diff_lambda_k1, diff_lambda_q2,
                diff_lambda_k2, diff_subln_g, even_w_out)
            new_dk.append(k_new)
            new_dv.append(v_new)
        else:
            x, ckv, kpe = _mla_layer(
                x, mods, i, j, norm_mix_g[i], cache_mla_ckv, cache_mla_kpe, cos, sin,
                mla_w_dq, mla_q_norm_g, mla_w_uq, mla_w_dkv, mla_kv_norm_g, mla_w_ukv, mla_w_o)
            new_ckv.append(ckv)
            new_kpe.append(kpe)
        x = _moe(x, mods, i, norm_ffn_g[i], router_w[i], router_bias[i],
                 expert_w_gate, expert_w_up, expert_w_down,
                 shared_w_gate, shared_w_up, shared_w_down,
                 final_norm_g, final=(i == depth - 1))
    y_prompt = x[:T_PROMPT].reshape(N_PROMPT_SEQ, PROMPT_LEN, D)
    y_sample = x[T_PROMPT:].reshape(N_LAT_SEQ, LAT_LEN, D)
    return (y_prompt, y_sample, jnp.stack(new_dk, axis=1), jnp.stack(new_dv, axis=1),
            jnp.stack(new_ckv, axis=1), jnp.stack(new_kpe, axis=1))
```

```python
import functools
import math

import jax
import jax.numpy as jnp
from jax import lax
from jax.experimental import pallas as pl
from jax.experimental.pallas import tpu as pltpu

F32 = jnp.float32
BF16 = jnp.bfloat16

D = 2048
N_PROMPT_SEQ = 16
PROMPT_LEN = 256
N_LAT_SEQ = 4
LAT_LEN = 1024
PAST_LEN = 256
T_PROMPT = N_PROMPT_SEQ * PROMPT_LEN
T_LAT = N_LAT_SEQ * LAT_LEN
T = T_PROMPT + T_LAT
UNIT = 1024
N_UNITS = T // UNIT
N_PROMPT_UNITS = T_PROMPT // UNIT
SEQ_PER_PROMPT_UNIT = UNIT // PROMPT_LEN
N_GROUPS_PAD = 8

GRID_W = 64
ROPE_BASE = 10000.0
EPS = 1e-6
POOL_W = 1024
POOL_WINDOWS = (2, 4, 8, 16)
POOL_G = 256
POOL_HALO = 8
DIFF_W = 1024
DIFF_DH = 64
DIFF_HEADS = 8
DIFF_SCALE = DIFF_DH ** -0.5
MLA_HEADS = 16
Q_LORA = 512
KV_LORA = 512
QK_NOPE = 128
QK_ROPE = 64
V_DIM = 128
MLA_SCALE = (QK_NOPE + QK_ROPE) ** -0.5
N_EXPERTS = 64
TOP_K = 6
N_EGROUPS = 8
EGROUP = N_EXPERTS // N_EGROUPS
TOPK_GROUPS = 4
D_EXPERT = 512
ROUTED_SCALE = 2.5

LANES = 128
ATT_QB = 256
ROUTE_TM = 256
COMBINE_TM = 128
GMM_TM = 256
N_PAIRS = T * TOP_K
N_ROW_TILES = N_PAIRS // GMM_TM
N_VISITS = N_ROW_TILES + N_EXPERTS
VMEM_LIMIT = 56 * 1024 * 1024

_NT = (((1,), (1,)), ((), ()))


def _cparams(n_axes, vmem=VMEM_LIMIT):
    return pltpu.CompilerParams(dimension_semantics=("arbitrary",) * n_axes, vmem_limit_bytes=vmem)


def _group_of_tile(i, tm):
    n_p = T_PROMPT // tm
    per_seq = LAT_LEN // tm
    return jnp.where(i < n_p, 0, 1 + (i - n_p) // per_seq)


def _sigmoid(x):
    return 1.0 / (1.0 + jnp.exp(-x))


def _rms(x, g):
    return x * lax.rsqrt(jnp.mean(x * x, axis=-1, keepdims=True) + EPS) * g


def _ada_kernel(c_ref, w_ref, b_ref, o_ref):
    c = c_ref[...]
    s = c * _sigmoid(c)
    o_ref[...] = jnp.dot(s, w_ref[...], preferred_element_type=F32,
                         precision=lax.Precision.HIGHEST) + b_ref[...]


def _ada_params(cond, ada_w, ada_b):
    depth, _, n = ada_w.shape
    tn = 1024
    out = pl.pallas_call(
        _ada_kernel,
        grid=(depth, n // tn),
        in_specs=[
            pl.BlockSpec((N_GROUPS_PAD, D), lambda l, j: (0, 0)),
            pl.BlockSpec((None, D, tn), lambda l, j: (l, 0, j)),
            pl.BlockSpec((None, 1, tn), lambda l, j: (l, 0, j)),
        ],
        out_specs=pl.BlockSpec((None, N_GROUPS_PAD, tn), lambda l, j: (l, 0, j)),
        out_shape=jax.ShapeDtypeStruct((depth, N_GROUPS_PAD, n), F32),
        compiler_params=_cparams(2),
        name="ada_params",
    )(cond, ada_w, ada_b.reshape(depth, 1, n))
    return out.reshape(depth, N_GROUPS_PAD, 1, n)


def _mod_spec(layer, chunk, tm, grid_rank, row_axis):
    def index_map(*ids):
        return (layer, _group_of_tile(ids[row_axis], tm), 0, chunk)
    del grid_rank
    return pl.BlockSpec((None, None, 1, D), index_map)


def _norm_mod_kernel(x_ref, g_ref, sh_ref, sc_ref, o_ref):
    y = _rms(x_ref[...], g_ref[...])
    o_ref[...] = (y * (1.0 + sc_ref[...]) + sh_ref[...]).astype(o_ref.dtype)


def _norm_mod(x, g, mods, layer, shift_chunk, scale_chunk):
    tm = 512
    return pl.pallas_call(
        _norm_mod_kernel,
        grid=(T // tm,),
        in_specs=[
            pl.BlockSpec((tm, D), lambda i: (i, 0)),
            pl.BlockSpec((1, D), lambda i: (0, 0)),
            _mod_spec(layer, shift_chunk, tm, 1, 0),
            _mod_spec(layer, scale_chunk, tm, 1, 0),
        ],
        out_specs=pl.BlockSpec((tm, D), lambda i: (i, 0)),
        out_shape=jax.ShapeDtypeStruct((T, D), BF16),
        compiler_params=_cparams(1),
        name="norm_mod",
    )(x, g.reshape(1, D), mods, mods)


def _mm_kernel(*refs, n_a, epilogue):
    a_refs = refs[:n_a]
    w_refs = refs[n_a:2 * n_a]
    pos = 2 * n_a
    if epilogue == "rms":
        g_ref = refs[pos]
        pos += 1
    elif epilogue == "residual":
        res_ref, gate_ref = refs[pos], refs[pos + 1]
        pos += 2
    o_ref = refs[pos]
    w_scr = refs[pos + 1:]

    @pl.when(pl.program_id(1) == 0)
    def _():
        for w_ref, scr in zip(w_refs, w_scr):
            scr[...] = w_ref[...].astype(BF16)

    acc = None
    for a_ref, scr in zip(a_refs, w_scr):
        part = jnp.dot(a_ref[...].astype(BF16), scr[...], preferred_element_type=F32)
        acc = part if acc is None else acc + part
    if epilogue == "rms":
        acc = _rms(acc, g_ref[...])
    elif epilogue == "residual":
        acc = res_ref[...] + gate_ref[...] * acc
    o_ref[...] = acc.astype(o_ref.dtype)


def _matmul(a_list, w, w_row_blocks, n_out, out_dtype, *, tm=512, tn=None, epilogue=None,
            gain=None, residual=None, mods=None, layer=None, gate_chunk=None, name="matmul"):
    m = a_list[0].shape[0]
    tn = n_out if tn is None else tn
    n_a = len(a_list)
    in_specs, args = [], []
    for a in a_list:
        k = a.shape[1]
        in_specs.append(pl.BlockSpec((tm, k), lambda j, i: (i, 0)))
        args.append(a)
    scratch = []
    for a, rb in zip(a_list, w_row_blocks):
        k = a.shape[1]
        in_specs.append(pl.BlockSpec((k, tn), lambda j, i, rb=rb: (rb, j)))
        args.append(w)
        scratch.append(pltpu.VMEM((k, tn), BF16))
    if epilogue == "rms":
        assert tn == n_out
        in_specs.append(pl.BlockSpec((1, tn), lambda j, i: (0, 0)))
        args.append(gain.reshape(1, n_out))
    elif epilogue == "residual":
        in_specs.append(pl.BlockSpec((tm, tn), lambda j, i: (i, j)))
        args.append(residual)
        in_specs.append(pl.BlockSpec(
            (None, None, 1, tn),
            lambda j, i: (layer, _group_of_tile(i, tm), 0, gate_chunk * (D // tn) + j)))
        args.append(mods)
    return pl.pallas_call(
        functools.partial(_mm_kernel, n_a=n_a, epilogue=epilogue),
        grid=(n_out // tn, m // tm),
        in_specs=in_specs,
        out_specs=pl.BlockSpec((tm, tn), lambda j, i: (i, j)),
        out_shape=jax.ShapeDtypeStruct((m, n_out), out_dtype),
        scratch_shapes=scratch,
        compiler_params=_cparams(2),
        name=name,
    )(*args)


def _pool_seq(u_ref, row0, seq_len, pw_ref, scale_ref, o_ref, pad_ref):
    t = lax.broadcasted_iota(jnp.int32, (seq_len, 1), 0)
    zeros = jnp.zeros((POOL_HALO, POOL_G), F32)
    for g, w in enumerate(POOL_WINDOWS):
        cols = slice(g * POOL_G, (g + 1) * POOL_G)
        ug = u_ref[row0:row0 + seq_len, cols]
        pad_ref[0:POOL_HALO, :] = zeros
        pad_ref[POOL_HALO:POOL_HALO + seq_len, :] = ug
        pad_ref[POOL_HALO + seq_len:2 * POOL_HALO + seq_len, :] = zeros
        total = None
        for off in range(-(w // 2), w // 2):
            part = pad_ref[POOL_HALO + off:POOL_HALO + off + seq_len, :]
            total = part if total is None else total + part
        cnt = (jnp.minimum(t + w // 2, seq_len) - jnp.maximum(t - w // 2, 0)).astype(F32)
        d = (total / cnt - ug).astype(BF16)
        y = jnp.dot(d, pw_ref[g].astype(BF16), preferred_element_type=F32)
        o_ref[row0:row0 + seq_len, cols] = (y * scale_ref[:, cols]).astype(o_ref.dtype)


def _pool_kernel(u_ref, pw_ref, scale_ref, o_ref, pad_ref):
    unit = pl.program_id(0)

    @pl.when(unit < N_PROMPT_UNITS)
    def _():
        for s in range(SEQ_PER_PROMPT_UNIT):
            _pool_seq(u_ref, s * PROMPT_LEN, PROMPT_LEN, pw_ref, scale_ref, o_ref, pad_ref)

    @pl.when(unit >= N_PROMPT_UNITS)
    def _():
        _pool_seq(u_ref, 0, LAT_LEN, pw_ref, scale_ref, o_ref, pad_ref)


def _pool(proj, pool_w, pool_scale):
    return pl.pallas_call(
        _pool_kernel,
        grid=(N_UNITS,),
        in_specs=[
            pl.BlockSpec((UNIT, POOL_W), lambda u: (u, 0)),
            pl.BlockSpec((len(POOL_WINDOWS), POOL_G, POOL_G), lambda u: (0, 0, 0)),
            pl.BlockSpec((1, POOL_W), lambda u: (0, 0)),
        ],
        out_specs=pl.BlockSpec((UNIT, POOL_W), lambda u: (u, 0)),
        out_shape=jax.ShapeDtypeStruct((T, POOL_W), BF16),
        scratch_shapes=[pltpu.VMEM((LAT_LEN + 2 * POOL_HALO, POOL_G), F32)],
        compiler_params=_cparams(1),
        name="pool",
    )(proj, pool_w, pool_scale.reshape(1, POOL_W))


def _rope_tables():
    pos = jnp.arange(LAT_LEN)
    row = (pos // GRID_W).astype(F32)
    col = (pos % GRID_W).astype(F32)
    quarter = QK_ROPE // 4
    inv = ROPE_BASE ** (-jnp.arange(quarter, dtype=F32) / quarter)
    lane = jnp.arange(LANES)
    axis = (lane % QK_ROPE) // (QK_ROPE // 2)
    freq = inv[lane % quarter]
    p = jnp.where(axis[None, :] == 0, row[:, None], col[:, None])
    ang = p * freq[None, :]
    sign = jnp.where((lane % (QK_ROPE // 2)) < quarter, -1.0, 1.0).astype(F32)
    return jnp.cos(ang), jnp.sin(ang) * sign[None, :]


def _rope(x, cos, sin_signed):
    quarter = QK_ROPE // 4
    lane = lax.broadcasted_iota(jnp.int32, x.shape, 1)
    first_half = (lane % (QK_ROPE // 2)) < quarter
    partner = jnp.where(first_half, pltpu.roll(x, LANES - quarter, 1), pltpu.roll(x, quarter, 1))
    return x * cos + partner * sin_signed


def _diff_core(q, k, v, lam, g, lam_init):
    lane = lax.broadcasted_iota(jnp.int32, q.shape, 1)
    low = lane < DIFF_DH
    q1 = jnp.where(low, q, 0.0).astype(BF16)
    q2 = jnp.where(low, 0.0, q).astype(BF16)

    def probs(qm):
        s = lax.dot_general(qm, k, _NT, preferred_element_type=F32)
        e = jnp.exp(s - jnp.max(s, axis=-1, keepdims=True))
        return e * (1.0 / jnp.sum(e, axis=-1, keepdims=True))

    w = (probs(q1) - lam * probs(q2)).astype(BF16)
    o = jnp.dot(w, v, preferred_element_type=F32)
    return _rms(o, g) * (1.0 - lam_init)


def _diff_attn_kernel(lp_ref, q_ref, k_ref, v_ref, ck_ref, cv_ref, cos_ref, sin_ref, g_ref, o_ref,
                      *, lam_init):
    unit = pl.program_id(0)
    lp = lp_ref[...]
    lam = (jnp.exp(jnp.sum(lp[0:1] * lp[1:2], axis=-1, keepdims=True))
           - jnp.exp(jnp.sum(lp[2:3] * lp[3:4], axis=-1, keepdims=True)) + lam_init)
    g = g_ref[...]

    @pl.when(unit < N_PROMPT_UNITS)
    def _():
        for s in range(SEQ_PER_PROMPT_UNIT):
            rows = slice(s * PROMPT_LEN, (s + 1) * PROMPT_LEN)
            q = q_ref[rows, :] * DIFF_SCALE
            k = k_ref[rows, :].astype(BF16)
            v = v_ref[rows, :].astype(BF16)
            o_ref[rows, :] = _diff_core(q, k, v, lam, g, lam_init).astype(o_ref.dtype)

    @pl.when(unit >= N_PROMPT_UNITS)
    def _():
        cos, sin = cos_ref[...], sin_ref[...]
        k = jnp.concatenate([ck_ref[...].astype(BF16),
                             _rope(k_ref[...], cos, sin).astype(BF16)], axis=0)
        v = jnp.concatenate([cv_ref[...].astype(BF16), v_ref[...].astype(BF16)], axis=0)
        for b in range(LAT_LEN // ATT_QB):
            rows = slice(b * ATT_QB, (b + 1) * ATT_QB)
            q = _rope(q_ref[rows, :], cos[rows], sin[rows]) * DIFF_SCALE
            o_ref[rows, :] = _diff_core(q, k, v, lam, g, lam_init).astype(o_ref.dtype)


def _diff_attn(proj, cache_k, cache_v, lam_params, subln_g, cos, sin, lam_init):
    hd = 2 * DIFF_DH
    q0, k0, v0 = POOL_W // hd, (POOL_W + DIFF_W) // hd, (POOL_W + 2 * DIFF_W) // hd

    def ctx_map(u, h):
        return (jnp.maximum(u - N_PROMPT_UNITS, 0), h)

    return pl.pallas_call(
        functools.partial(_diff_attn_kernel, lam_init=lam_init),
        grid=(N_UNITS, DIFF_HEADS),
        in_specs=[
            pl.BlockSpec((4, DIFF_DH), lambda u, h: (0, 0)),
            pl.BlockSpec((UNIT, hd), lambda u, h: (u, q0 + h)),
            pl.BlockSpec((UNIT, hd), lambda u, h: (u, k0 + h)),
            pl.BlockSpec((UNIT, hd), lambda u, h: (u, v0 + h)),
            pl.BlockSpec((PAST_LEN, hd), ctx_map),
            pl.BlockSpec((PAST_LEN, hd), ctx_map),
            pl.BlockSpec((LAT_LEN, LANES), lambda u, h: (0, 0)),
            pl.BlockSpec((LAT_LEN, LANES), lambda u, h: (0, 0)),
            pl.BlockSpec((1, hd), lambda u, h: (0, 0)),
        ],
        out_specs=pl.BlockSpec((UNIT, hd), lambda u, h: (u, h)),
        out_shape=jax.ShapeDtypeStruct((T, DIFF_W), BF16),
        compiler_params=_cparams(2),
        name="diff_attn",
    )(lam_params, proj, proj, proj, cache_k, cache_v, cos, sin, subln_g.reshape(1, hd))


def _mla_core(qc, kc, v):
    s = lax.dot_general(qc, kc, _NT, preferred_element_type=F32) * MLA_SCALE
    e = jnp.exp(s - jnp.max(s, axis=-1, keepdims=True))
    p = (e * (1.0 / jnp.sum(e, axis=-1, keepdims=True))).astype(BF16)
    return jnp.dot(p, v, preferred_element_type=F32)


def _mla_attn_kernel(qn_ref, qp_ref, kn_ref, v_ref, kp_ref, ckn_ref, cv_ref, ckp_ref, cos_ref, sin_ref,
                     o_ref):
    unit = pl.program_id(0)
    head = pl.program_id(1)
    lane = lax.broadcasted_iota(jnp.int32, (ATT_QB, LANES), 1)
    mine = (lane // QK_ROPE) == (head % 2)

    @pl.when(unit < N_PROMPT_UNITS)
    def _():
        for s in range(SEQ_PER_PROMPT_UNIT):
            rows = slice(s * PROMPT_LEN, (s + 1) * PROMPT_LEN)
            qp = jnp.where(mine, qp_ref[rows, :], 0.0).astype(BF16)
            qc = jnp.concatenate([qn_ref[rows, :], qp], axis=1)
            kc = jnp.concatenate([kn_ref[rows, :], kp_ref[rows, :].astype(BF16)], axis=1)
            o_ref[rows, :] = _mla_core(qc, kc, v_ref[rows, :]).astype(o_ref.dtype)

    @pl.when(unit >= N_PROMPT_UNITS)
    def _():
        cos, sin = cos_ref[...], sin_ref[...]
        k_ctx = jnp.concatenate([ckn_ref[...], ckp_ref[...].astype(BF16)], axis=1)
        k_new = jnp.concatenate([kn_ref[...], _rope(kp_ref[...], cos, sin).astype(BF16)], axis=1)
        kc = jnp.concatenate([k_ctx, k_new], axis=0)
        v = jnp.concatenate([cv_ref[...], v_ref[...]], axis=0)
        for b in range(LAT_LEN // ATT_QB):
            rows = slice(b * ATT_QB, (b + 1) * ATT_QB)
            qp = jnp.where(mine, _rope(qp_ref[rows, :], cos[rows], sin[rows]), 0.0).astype(BF16)
            qc = jnp.concatenate([qn_ref[rows, :], qp], axis=1)
            o_ref[rows, :] = _mla_core(qc, kc, v).astype(o_ref.dtype)


def _mla_attn(q_nope, q_pe, kv, kp_dup, kv_ctx, kp_ctx_dup, cos, sin):
    def ctx_row(u):
        return jnp.maximum(u - N_PROMPT_UNITS, 0)

    return pl.pallas_call(
        _mla_attn_kernel,
        grid=(N_UNITS, MLA_HEADS),
        in_specs=[
            pl.BlockSpec((UNIT, QK_NOPE), lambda u, h: (u, h)),
            pl.BlockSpec((UNIT, LANES), lambda u, h: (u, h // 2)),
            pl.BlockSpec((UNIT, QK_NOPE), lambda u, h: (u, 2 * h)),
            pl.BlockSpec((UNIT, V_DIM), lambda u, h: (u, 2 * h + 1)),
            pl.BlockSpec((UNIT, LANES), lambda u, h: (u, 0)),
            pl.BlockSpec((PAST_LEN, QK_NOPE), lambda u, h: (ctx_row(u), 2 * h)),
            pl.BlockSpec((PAST_LEN, V_DIM), lambda u, h: (ctx_row(u), 2 * h + 1)),
            pl.BlockSpec((PAST_LEN, LANES), lambda u, h: (ctx_row(u), 0)),
            pl.BlockSpec((LAT_LEN, LANES), lambda u, h: (0, 0)),
            pl.BlockSpec((LAT_LEN, LANES), lambda u, h: (0, 0)),
        ],
        out_specs=pl.BlockSpec((UNIT, V_DIM), lambda u, h: (u, h)),
        out_shape=jax.ShapeDtypeStruct((T, MLA_HEADS * V_DIM), BF16),
        compiler_params=_cparams(2),
        name="mla_attn",
    )(q_nope, q_pe, kv, kv, kp_dup, kv_ctx, kv_ctx, kp_ctx_dup, cos, sin)


def _route_kernel(x_ref, g_ref, sh_ref, sc_ref, rw_ref, rb_ref,
                  h_ref, eidx_ref, wts_ref, cnt_ref, carry_ref):
    tm = ROUTE_TM

    @pl.when(pl.program_id(0) == 0)
    def _():
        carry_ref[...] = jnp.zeros_like(carry_ref)

    h = _rms(x_ref[...], g_ref[...]) * (1.0 + sc_ref[...]) + sh_ref[...]
    h_ref[...] = h
    logits = lax.dot_general(rw_ref[...], h, _NT, preferred_element_type=F32,
                             precision=lax.Precision.HIGHEST)
    scores = _sigmoid(logits)
    choice = scores + rb_ref[...]
    neg = -jnp.inf

    def take_max(vals, iota, n):
        m = jnp.max(vals, axis=0, keepdims=True)
        idx = jnp.min(jnp.where(vals == m, iota, n), axis=0, keepdims=True)
        return iota == idx, m, idx

    iota8 = lax.broadcasted_iota(jnp.int32, (8, tm), 0)

    def stack_rows(rows, dtype):
        out = jnp.zeros((8, tm), dtype)
        for k, r in enumerate(rows):
            out = jnp.where(iota8 == k, r.astype(dtype), out)
        return out

    gscore = []
    for gi in range(N_EGROUPS):
        grp = choice[gi * EGROUP:(gi + 1) * EGROUP]
        oh, m1, _ = take_max(grp, iota8, EGROUP)
        m2 = jnp.max(jnp.where(oh, neg, grp), axis=0, keepdims=True)
        gscore.append(m1 + m2)
    cur = stack_rows(gscore, F32)
    gsel = jnp.zeros((N_EGROUPS, tm), F32)
    for _ in range(TOPK_GROUPS):
        oh, _, _ = take_max(cur, iota8, N_EGROUPS)
        gsel = jnp.where(oh, 1.0, gsel)
        cur = jnp.where(oh, neg, cur)
    emask = jnp.concatenate(
        [jnp.broadcast_to(gsel[gi:gi + 1], (EGROUP, tm)) for gi in range(N_EGROUPS)], axis=0)
    masked = jnp.where(emask > 0.0, choice, neg)

    iota_e = lax.broadcasted_iota(jnp.int32, (N_EXPERTS, tm), 0)
    idxs, wsel = [], []
    sel = jnp.zeros((N_EXPERTS, tm), F32)
    for _ in range(TOP_K):
        oh, _, idx = take_max(masked, iota_e, N_EXPERTS)
        idxs.append(idx)
        wsel.append(jnp.sum(jnp.where(oh, scores, 0.0), axis=0, keepdims=True))
        masked = jnp.where(oh, neg, masked)
        sel = jnp.where(oh, 1.0, sel)
    wsum = wsel[0]
    for w in wsel[1:]:
        wsum = wsum + w

    carry_ref[...] = carry_ref[...] + jnp.sum(sel, axis=1, keepdims=True)
    cnt_ref[...] = carry_ref[...]
    eidx_ref[...] = stack_rows(idxs, jnp.int32)
    wts_ref[...] = stack_rows([w / wsum * ROUTED_SCALE for w in wsel], F32)


def _route(x, g, mods, layer, router_w, router_bias):
    tm = ROUTE_TM
    return pl.pallas_call(
        _route_kernel,
        grid=(T // tm,),
        in_specs=[
            pl.BlockSpec((tm, D), lambda i: (i, 0)),
            pl.BlockSpec((1, D), lambda i: (0, 0)),
            _mod_spec(layer, 3, tm, 1, 0),
            _mod_spec(layer, 4, tm, 1, 0),
            pl.BlockSpec((N_EXPERTS, D), lambda i: (0, 0)),
            pl.BlockSpec((N_EXPERTS, 1), lambda i: (0, 0)),
        ],
        out_specs=[
            pl.BlockSpec((tm, D), lambda i: (i, 0)),
            pl.BlockSpec((8, tm), lambda i: (0, i)),
            pl.BlockSpec((8, tm), lambda i: (0, i)),
            pl.BlockSpec((N_EXPERTS, 1), lambda i: (0, 0)),
        ],
        out_shape=[
            jax.ShapeDtypeStruct((T, D), F32),
            jax.ShapeDtypeStruct((8, T), jnp.int32),
            jax.ShapeDtypeStruct((8, T), F32),
            jax.ShapeDtypeStruct((N_EXPERTS, 1), F32),
        ],
        scratch_shapes=[pltpu.VMEM((N_EXPERTS, 1), F32)],
        compiler_params=_cparams(1),
        name="route",
    )(x, g.reshape(1, D), mods, mods, router_w.T, router_bias.reshape(N_EXPERTS, 1))


def _row_copy(src, src_row, dst, dst_row, sem):
    return pltpu.make_async_copy(src.at[pl.ds(src_row, 1), :], dst.at[pl.ds(dst_row, 1), :], sem)


def _combine_kernel(x_ref, sh_ref, w_ref, gate_ref, fg_ref, y_ref, o_ref, *, final):
    w = w_ref[...]
    acc = sh_ref[...]
    for k in range(TOP_K):
        acc = acc + w[:, k:k + 1] * y_ref[k]
    xn = x_ref[...] + gate_ref[...] * acc
    if final:
        xn = _rms(xn, fg_ref[...])
    o_ref[...] = xn


def _combine(x, shared, wts_t, mods, layer, final_g, y, final):
    tm = COMBINE_TM
    return pl.pallas_call(
        functools.partial(_combine_kernel, final=final),
        grid=(T // tm,),
        in_specs=[
            pl.BlockSpec((tm, D), lambda i: (i, 0)),
            pl.BlockSpec((tm, D), lambda i: (i, 0)),
            pl.BlockSpec((tm, 8), lambda i: (i, 0)),
            pl.BlockSpec((None, None, 1, D), lambda i: (layer, _group_of_tile(i, tm), 0, 5)),
            pl.BlockSpec((1, D), lambda i: (0, 0)),
            pl.BlockSpec((TOP_K, tm, D), lambda i: (0, i, 0)),
        ],
        out_specs=pl.BlockSpec((tm, D), lambda i: (i, 0)),
        out_shape=jax.ShapeDtypeStruct((T, D), F32),
        compiler_params=_cparams(1),
        name="combine",
    )(x, shared, wts_t, mods, final_g.reshape(1, D), y.reshape(TOP_K, T, D))


def _swiglu(x, wgu_scr, wd_scr):
    gu = jnp.dot(x.astype(BF16), wgu_scr[...], preferred_element_type=F32)
    gate, up = gu[:, :D_EXPERT], gu[:, D_EXPERT:]
    act = (gate * _sigmoid(gate) * up).astype(BF16)
    return jnp.dot(act, wd_scr[...], preferred_element_type=F32)


def _load_expert_weights(wg_ref, wu_ref, wd_ref, wgu_scr, wd_scr):
    wgu_scr[:, :D_EXPERT] = wg_ref[...].astype(BF16)
    wgu_scr[:, D_EXPERT:] = wu_ref[...].astype(BF16)
    wd_scr[...] = wd_ref[...].astype(BF16)


def _shared_kernel(h_ref, wg_ref, wu_ref, wd_ref, o_ref, wgu_scr, wd_scr):
    @pl.when(pl.program_id(0) == 0)
    def _():
        _load_expert_weights(wg_ref, wu_ref, wd_ref, wgu_scr, wd_scr)

    o_ref[...] = _swiglu(h_ref[...], wgu_scr, wd_scr)


def _shared_expert(h, layer, w_gate, w_up, w_down):
    tm = 512
    return pl.pallas_call(
        _shared_kernel,
        grid=(T // tm,),
        in_specs=[
            pl.BlockSpec((tm, D), lambda i: (i, 0)),
            pl.BlockSpec((None, D, D_EXPERT), lambda i: (layer, 0, 0)),
            pl.BlockSpec((None, D, D_EXPERT), lambda i: (layer, 0, 0)),
            pl.BlockSpec((None, D_EXPERT, D), lambda i: (layer, 0, 0)),
        ],
        out_specs=pl.BlockSpec((tm, D), lambda i: (i, 0)),
        out_shape=jax.ShapeDtypeStruct((T, D), F32),
        scratch_shapes=[pltpu.VMEM((D, 2 * D_EXPERT), BF16), pltpu.VMEM((D_EXPERT, D), BF16)],
        compiler_params=_cparams(1),
        name="shared_expert",
    )(h, w_gate, w_up, w_down)


def _experts_kernel(e_ref, t_ref, lo_ref, hi_ref, inv_ref, inv_next_ref, h_ref, wg_ref, wu_ref, wd_ref,
                    y_ref, xbuf, ybuf, gsem, ssem, wgu_scr, wd_scr):
    del t_ref
    v = pl.program_id(0)
    n_v = pl.num_programs(0)
    slot = v % 2
    prev = jnp.maximum(v - 1, 0)
    nxt = jnp.minimum(v + 1, n_v - 1)
    new_expert = (v == 0) | (e_ref[v] != e_ref[prev])
    lo, hi = lo_ref[v], hi_ref[v]

    def start_gathers(idx_ref, first, last, dst_slot):
        def body(r, carry):
            token = idx_ref[0, 0, r] & (T - 1)
            _row_copy(h_ref, token, xbuf.at[dst_slot], r, gsem.at[dst_slot]).start()
            return carry
        lax.fori_loop(first, last, body, 0)

    def wait_rows(first, last, src, dst, sem):
        def body(r, carry):
            _row_copy(src, 0, dst, 0, sem).wait()
            return carry
        lax.fori_loop(first, last, body, 0)

    @pl.when(v == 0)
    def _():
        xbuf[...] = jnp.zeros_like(xbuf)
        start_gathers(inv_ref, lo, hi, 0)

    @pl.when(v + 1 < n_v)
    def _():
        start_gathers(inv_next_ref, lo_ref[nxt], hi_ref[nxt], 1 - slot)

    @pl.when(new_expert)
    def _():
        _load_expert_weights(wg_ref, wu_ref, wd_ref, wgu_scr, wd_scr)

    wait_rows(lo, hi, h_ref, xbuf.at[slot], gsem.at[slot])

    @pl.when(v >= 2)
    def _():
        before = jnp.maximum(v - 2, 0)
        wait_rows(lo_ref[before], hi_ref[before], ybuf.at[slot], y_ref, ssem.at[slot])

    @pl.when(hi > lo)
    def _():
        ybuf[slot] = _swiglu(xbuf[slot], wgu_scr, wd_scr)

        def body(r, carry):
            _row_copy(ybuf.at[slot], r, y_ref, inv_ref[0, 0, r], ssem.at[slot]).start()
            return carry
        lax.fori_loop(lo, hi, body, 0)

    @pl.when(v == n_v - 1)
    def _():
        wait_rows(lo, hi, ybuf.at[slot], y_ref, ssem.at[slot])

        @pl.when(v >= 1)
        def _():
            wait_rows(lo_ref[prev], hi_ref[prev], ybuf.at[1 - slot], y_ref, ssem.at[1 - slot])


def _experts(sched, inv, h, layer, w_gate, w_up, w_down):
    n_visits = sched[0].shape[0]

    def next_tile(v, e, t, lo, hi):
        return (t[jnp.minimum(v + 1, n_visits - 1)], 0, 0)

    return pl.pallas_call(
        _experts_kernel,
        grid_spec=pltpu.PrefetchScalarGridSpec(
            num_scalar_prefetch=4,
            grid=(n_visits,),
            in_specs=[
                pl.BlockSpec((1, 1, GMM_TM), lambda v, e, t, lo, hi: (t[v], 0, 0), memory_space=pltpu.SMEM),
                pl.BlockSpec((1, 1, GMM_TM), next_tile, memory_space=pltpu.SMEM),
                pl.BlockSpec(memory_space=pl.ANY),
                pl.BlockSpec((None, None, D, D_EXPERT), lambda v, e, t, lo, hi: (layer, e[v], 0, 0)),
                pl.BlockSpec((None, None, D, D_EXPERT), lambda v, e, t, lo, hi: (layer, e[v], 0, 0)),
                pl.BlockSpec((None, None, D_EXPERT, D), lambda v, e, t, lo, hi: (layer, e[v], 0, 0)),
            ],
            out_specs=pl.BlockSpec(memory_space=pl.ANY),
            scratch_shapes=[
                pltpu.VMEM((2, GMM_TM, D), F32),
                pltpu.VMEM((2, GMM_TM, D), F32),
                pltpu.SemaphoreType.DMA((2,)),
                pltpu.SemaphoreType.DMA((2,)),
                pltpu.VMEM((D, 2 * D_EXPERT), BF16),
                pltpu.VMEM((D_EXPERT, D), BF16),
            ],
        ),
        out_shape=jax.ShapeDtypeStruct((N_PAIRS, D), F32),
        compiler_params=_cparams(1),
        name="experts",
    )(*sched, inv, inv, h, w_gate, w_up, w_down)


def _visit_schedule(counts):
    tm = GMM_TM
    ends = jnp.cumsum(counts)
    starts = ends - counts
    first_tile = starts // tm
    last_tile = jnp.maximum(ends - 1, 0) // tm
    n_vis = jnp.where(counts > 0, last_tile - first_tile + 1, 0)
    vis_end = jnp.cumsum(n_vis)
    vis_start = vis_end - n_vis
    total = vis_end[-1]
    v = jnp.minimum(jnp.arange(N_VISITS, dtype=jnp.int32), total - 1)
    e = jnp.sum((vis_end[None, :] <= v[:, None]).astype(jnp.int32), axis=1)
    tile = first_tile[e] + (v - vis_start[e])
    lo = jnp.clip(starts[e] - tile * tm, 0, tm)
    hi = jnp.clip(ends[e] - tile * tm, 0, tm)
    hi = jnp.where(jnp.arange(N_VISITS) < total, hi, lo)
    return e, tile.astype(jnp.int32), lo.astype(jnp.int32), hi.astype(jnp.int32)


def _moe(x, mods, layer, norm_g, router_w, router_bias, w_gate, w_up, w_down,
         ws_gate, ws_up, ws_down, final_g, final):
    h, eidx, wts, counts = _route(x, norm_g, mods, layer, router_w, router_bias)
    sched = _visit_schedule(counts.reshape(N_EXPERTS).astype(jnp.int32))
    key = eidx[:TOP_K] * T + jnp.arange(T, dtype=jnp.int32)[None, :]
    inv = jnp.argsort(key.reshape(N_PAIRS)).astype(jnp.int32).reshape(N_ROW_TILES, 1, GMM_TM)
    y = _experts(sched, inv, h, layer, w_gate, w_up, w_down)
    shared = _shared_expert(h, layer, ws_gate, ws_up, ws_down)
    return _combine(x, shared, wts.T, mods, layer, final_g, y, final)


def _pool_diff_layer(x, mods, layer, j, norm_g, cache_k, cache_v, cos, sin,
                     diff_w_in, pool_w, pool_scale, lq1, lk1, lq2, lk2, subln_g, w_out):
    lam_init = 0.8 - 0.6 * math.exp(-0.3 * layer)
    h = _norm_mod(x, norm_g, mods, layer, 0, 1)
    proj = _matmul([h], diff_w_in[j], [0], POOL_W + 3 * DIFF_W, F32, tn=1024, name="diff_in_proj")
    y_pool = _pool(proj, pool_w[j], pool_scale[j])
    lam_params = jnp.stack([lq1[j], lk1[j], lq2[j], lk2[j]])
    ck = cache_k[:, j].reshape(N_LAT_SEQ * PAST_LEN, DIFF_W)
    cv = cache_v[:, j].reshape(N_LAT_SEQ * PAST_LEN, DIFF_W)
    o = _diff_attn(proj, ck, cv, lam_params, subln_g[j], cos, sin, lam_init)
    x = _matmul([y_pool, o], w_out[j], [0, 1], D, F32, tn=1024, epilogue="residual",
                residual=x, mods=mods, layer=layer, gate_chunk=2, name="diff_out_proj")
    k_new = proj[:T_PROMPT, POOL_W + DIFF_W:POOL_W + 2 * DIFF_W]
    v_new = proj[:T_PROMPT, POOL_W + 2 * DIFF_W:]
    shape = (N_PROMPT_SEQ, PROMPT_LEN, DIFF_HEADS, 2 * DIFF_DH)
    return x, k_new.reshape(shape), v_new.reshape(shape)


def _mla_layer(x, mods, layer, j, norm_g, cache_ckv, cache_kpe, cos, sin,
               w_dq, q_norm_g, w_uq, w_dkv, kv_norm_g, w_ukv, w_o):
    h = _norm_mod(x, norm_g, mods, layer, 0, 1)
    cq = _matmul([h], w_dq[j], [0], Q_LORA, BF16, epilogue="rms", gain=q_norm_g[j], name="mla_dq")
    ckv = _matmul([h], w_dkv[j], [0], KV_LORA, F32, epilogue="rms", gain=kv_norm_g[j], name="mla_dkv")
    w_kpe = w_dkv[j][:, KV_LORA:]
    kp_dup = _matmul([h], jnp.concatenate([w_kpe, w_kpe], axis=1), [0], LANES, F32, name="mla_kpe")
    w_uq3 = w_uq[j].reshape(Q_LORA, MLA_HEADS, QK_NOPE + QK_ROPE)
    w_uq_nope = w_uq3[:, :, :QK_NOPE].reshape(Q_LORA, MLA_HEADS * QK_NOPE)
    w_uq_pe = w_uq3[:, :, QK_NOPE:].reshape(Q_LORA, MLA_HEADS * QK_ROPE)
    q_nope = _matmul([cq], w_uq_nope, [0], MLA_HEADS * QK_NOPE, BF16, tn=1024, name="mla_uq_nope")
    q_pe = _matmul([cq], w_uq_pe, [0], MLA_HEADS * QK_ROPE, F32, tn=1024, name="mla_uq_pe")
    n_kv = MLA_HEADS * (QK_NOPE + V_DIM)
    kv = _matmul([ckv], w_ukv[j], [0], n_kv, BF16, tn=1024, name="mla_ukv")
    ckv_ctx = cache_ckv[:, j].reshape(N_LAT_SEQ * PAST_LEN, KV_LORA)
    kv_ctx = _matmul([ckv_ctx], w_ukv[j], [0], n_kv, BF16, tn=1024, name="mla_ukv_ctx")
    kpe_ctx = cache_kpe[:, j].reshape(N_LAT_SEQ * PAST_LEN, QK_ROPE)
    kp_ctx_dup = jnp.concatenate([kpe_ctx, kpe_ctx], axis=1)
    o = _mla_attn(q_nope, q_pe, kv, kp_dup, kv_ctx, kp_ctx_dup, cos, sin)
    x = _matmul([o], w_o[j], [0], D, F32, tn=1024, epilogue="residual",
                residual=x, mods=mods, layer=layer, gate_chunk=2, name="mla_out_proj")
    new_ckv = ckv[:T_PROMPT].reshape(N_PROMPT_SEQ, PROMPT_LEN, KV_LORA)
    new_kpe = kp_dup[:T_PROMPT, :QK_ROPE].reshape(N_PROMPT_SEQ, PROMPT_LEN, QK_ROPE)
    return x, new_ckv, new_kpe


def kernel(x_prompt, x_sample, cache_diff_k, cache_diff_v, cache_mla_ckv, cache_mla_kpe, c, c_ctx,
           ada_w, ada_b, norm_mix_g, norm_ffn_g, final_norm_g,
           diff_w_in, pool_w, pool_scale, diff_lambda_q1, diff_lambda_k1, diff_lambda_q2, diff_lambda_k2,
           diff_subln_g, even_w_out,
           mla_w_dq, mla_q_norm_g, mla_w_uq, mla_w_dkv, mla_kv_norm_g, mla_w_ukv, mla_w_o,
           router_w, router_bias, expert_w_gate, expert_w_up, expert_w_down,
           shared_w_gate, shared_w_up, shared_w_down):
    depth = ada_w.shape[0]
    x = jnp.concatenate([x_prompt.reshape(T_PROMPT, D), x_sample.reshape(T_LAT, D)], axis=0)
    cond = jnp.concatenate(
        [c_ctx[None, :], c, jnp.zeros((N_GROUPS_PAD - 1 - N_LAT_SEQ, D), F32)], axis=0)
    mods = _ada_params(cond, ada_w, ada_b)
    cos, sin = _rope_tables()

    new_dk, new_dv, new_ckv, new_kpe = [], [], [], []
    for i in range(depth):
        j = i // 2
        if i % 2 == 0:
            x, k_new, v_new = _pool_diff_layer(
                x, mods, i, j, norm_mix_g[i], cache_diff_k, cache_diff_v, cos, sin,
                diff_w_in, pool_w, pool_scale, diff_lambda_q1, diff_lambda_k1, diff_lambda_q2,
                diff_lambda_k2, diff_subln_g, even_w_out)
            new_dk.append(k_new)
            new_dv.append(v_new)
        else:
            x, ckv, kpe = _mla_layer(
                x, mods, i, j, norm_mix_g[i], cache_mla_ckv, cache_mla_kpe, cos, sin,
                mla_w_dq, mla_q_norm_g, mla_w_uq, mla_w_dkv, mla_kv_norm_g, mla_w_ukv, mla_w_o)
            new_ckv.append(ckv)
            new_kpe.append(kpe)
        x = _moe(x, mods, i, norm_ffn_g[i], router_w[i], router_bias[i],
                 expert_w_gate, expert_w_up, expert_w_down,
                 shared_w_gate, shared_w_up, shared_w_down,
                 final_norm_g, final=(i == depth - 1))
    y_prompt = x[:T_PROMPT].reshape(N_PROMPT_SEQ, PROMPT_LEN, D)
    y_sample = x[T_PROMPT:].reshape(N_LAT_SEQ, LAT_LEN, D)
    return (y_prompt, y_sample, jnp.stack(new_dk, axis=1), jnp.stack(new_dv, axis=1),
            jnp.stack(new_ckv, axis=1), jnp.stack(new_kpe, axis=1))
```

```python
import functools
import math

import jax
import jax.numpy as jnp
from jax import lax
from jax.experimental import pallas as pl
from jax.experimental.pallas import tpu as pltpu

F32 = jnp.float32
BF16 = jnp.bfloat16

D = 2048
N_PROMPT_SEQ = 16
PROMPT_LEN = 256
N_LAT_SEQ = 4
LAT_LEN = 1024
PAST_LEN = 256
T_PROMPT = N_PROMPT_SEQ * PROMPT_LEN
T_LAT = N_LAT_SEQ * LAT_LEN
T = T_PROMPT + T_LAT
UNIT = 1024
N_UNITS = T // UNIT
N_PROMPT_UNITS = T_PROMPT // UNIT
SEQ_PER_PROMPT_UNIT = UNIT // PROMPT_LEN
N_GROUPS_PAD = 8

GRID_W = 64
ROPE_BASE = 10000.0
EPS = 1e-6
POOL_W = 1024
POOL_WINDOWS = (2, 4, 8, 16)
POOL_G = 256
POOL_HALO = 8
DIFF_W = 1024
DIFF_DH = 64
DIFF_HEADS = 8
DIFF_SCALE = DIFF_DH ** -0.5
MLA_HEADS = 16
Q_LORA = 512
KV_LORA = 512
QK_NOPE = 128
QK_ROPE = 64
V_DIM = 128
MLA_SCALE = (QK_NOPE + QK_ROPE) ** -0.5
N_EXPERTS = 64
TOP_K = 6
N_EGROUPS = 8
EGROUP = N_EXPERTS // N_EGROUPS
TOPK_GROUPS = 4
D_EXPERT = 512
ROUTED_SCALE = 2.5

LANES = 128
ATT_QB = 256
ROUTE_TM = 256
COMBINE_TM = 128
GMM_TM = 256
ROW_GROUP = 16
N_PAIRS = T * TOP_K
N_ROW_TILES = N_PAIRS // GMM_TM
N_VISITS = N_ROW_TILES + N_EXPERTS
Y_ROWS = N_PAIRS + 2 * GMM_TM
VMEM_LIMIT = 56 * 1024 * 1024

_NT = (((1,), (1,)), ((), ()))


def _cparams(n_axes, vmem=VMEM_LIMIT):
    return pltpu.CompilerParams(dimension_semantics=("arbitrary",) * n_axes, vmem_limit_bytes=vmem)


def _group_of_tile(i, tm):
    n_p = T_PROMPT // tm
    per_seq = LAT_LEN // tm
    return jnp.where(i < n_p, 0, 1 + (i - n_p) // per_seq)


def _sigmoid(x):
    return 1.0 / (1.0 + jnp.exp(-x))


def _rms(x, g):
    return x * lax.rsqrt(jnp.mean(x * x, axis=-1, keepdims=True) + EPS) * g


def _ada_kernel(c_ref, w_ref, b_ref, o_ref):
    c = c_ref[...]
    s = c * _sigmoid(c)
    o_ref[...] = jnp.dot(s, w_ref[...], preferred_element_type=F32,
                         precision=lax.Precision.HIGHEST) + b_ref[...]


def _ada_params(cond, ada_w, ada_b):
    depth, _, n = ada_w.shape
    tn = 1024
    out = pl.pallas_call(
        _ada_kernel,
        grid=(depth, n // tn),
        in_specs=[
            pl.BlockSpec((N_GROUPS_PAD, D), lambda l, j: (0, 0)),
            pl.BlockSpec((None, D, tn), lambda l, j: (l, 0, j)),
            pl.BlockSpec((None, 1, tn), lambda l, j: (l, 0, j)),
        ],
        out_specs=pl.BlockSpec((None, N_GROUPS_PAD, tn), lambda l, j: (l, 0, j)),
        out_shape=jax.ShapeDtypeStruct((depth, N_GROUPS_PAD, n), F32),
        compiler_params=_cparams(2),
        name="ada_params",
    )(cond, ada_w, ada_b.reshape(depth, 1, n))
    return out.reshape(depth, N_GROUPS_PAD, 1, n)


def _mod_spec(layer, chunk, tm, grid_rank, row_axis):
    def index_map(*ids):
        return (layer, _group_of_tile(ids[row_axis], tm), 0, chunk)
    del grid_rank
    return pl.BlockSpec((None, None, 1, D), index_map)


def _norm_mod_kernel(x_ref, g_ref, sh_ref, sc_ref, o_ref):
    y = _rms(x_ref[...], g_ref[...])
    o_ref[...] = (y * (1.0 + sc_ref[...]) + sh_ref[...]).astype(o_ref.dtype)


def _norm_mod(x, g, mods, layer, shift_chunk, scale_chunk):
    tm = 512
    return pl.pallas_call(
        _norm_mod_kernel,
        grid=(T // tm,),
        in_specs=[
            pl.BlockSpec((tm, D), lambda i: (i, 0)),
            pl.BlockSpec((1, D), lambda i: (0, 0)),
            _mod_spec(layer, shift_chunk, tm, 1, 0),
            _mod_spec(layer, scale_chunk, tm, 1, 0),
        ],
        out_specs=pl.BlockSpec((tm, D), lambda i: (i, 0)),
        out_shape=jax.ShapeDtypeStruct((T, D), BF16),
        compiler_params=_cparams(1),
        name="norm_mod",
    )(x, g.reshape(1, D), mods, mods)


def _mm_kernel(*refs, n_a, epilogue):
    a_refs = refs[:n_a]
    w_refs = refs[n_a:2 * n_a]
    pos = 2 * n_a
    if epilogue == "rms":
        g_ref = refs[pos]
        pos += 1
    elif epilogue == "residual":
        res_ref, gate_ref = refs[pos], refs[pos + 1]
        pos += 2
    o_ref = refs[pos]
    w_scr = refs[pos + 1:]

    @pl.when(pl.program_id(1) == 0)
    def _():
        for w_ref, scr in zip(w_refs, w_scr):
            scr[...] = w_ref[...].astype(BF16)

    acc = None
    for a_ref, scr in zip(a_refs, w_scr):
        part = jnp.dot(a_ref[...].astype(BF16), scr[...], preferred_element_type=F32)
        acc = part if acc is None else acc + part
    if epilogue == "rms":
        acc = _rms(acc, g_ref[...])
    elif epilogue == "residual":
        acc = res_ref[...] + gate_ref[...] * acc
    o_ref[...] = acc.astype(o_ref.dtype)


def _matmul(a_list, w, w_row_blocks, n_out, out_dtype, *, tm=512, tn=None, epilogue=None,
            gain=None, residual=None, mods=None, layer=None, gate_chunk=None, name="matmul"):
    m = a_list[0].shape[0]
    tn = n_out if tn is None else tn
    n_a = len(a_list)
    in_specs, args = [], []
    for a in a_list:
        k = a.shape[1]
        in_specs.append(pl.BlockSpec((tm, k), lambda j, i: (i, 0)))
        args.append(a)
    scratch = []
    for a, rb in zip(a_list, w_row_blocks):
        k = a.shape[1]
        in_specs.append(pl.BlockSpec((k, tn), lambda j, i, rb=rb: (rb, j)))
        args.append(w)
        scratch.append(pltpu.VMEM((k, tn), BF16))
    if epilogue == "rms":
        assert tn == n_out
        in_specs.append(pl.BlockSpec((1, tn), lambda j, i: (0, 0)))
        args.append(gain.reshape(1, n_out))
    elif epilogue == "residual":
        in_specs.append(pl.BlockSpec((tm, tn), lambda j, i: (i, j)))
        args.append(residual)
        in_specs.append(pl.BlockSpec(
            (None, None, 1, tn),
            lambda j, i: (layer, _group_of_tile(i, tm), 0, gate_chunk * (D // tn) + j)))
        args.append(mods)
    return pl.pallas_call(
        functools.partial(_mm_kernel, n_a=n_a, epilogue=epilogue),
        grid=(n_out // tn, m // tm),
        in_specs=in_specs,
        out_specs=pl.BlockSpec((tm, tn), lambda j, i: (i, j)),
        out_shape=jax.ShapeDtypeStruct((m, n_out), out_dtype),
        scratch_shapes=scratch,
        compiler_params=_cparams(2),
        name=name,
    )(*args)


def _pool_seq(u_ref, row0, seq_len, pw_ref, scale_ref, o_ref, pad_ref):
    t = lax.broadcasted_iota(jnp.int32, (seq_len, 1), 0)
    zeros = jnp.zeros((POOL_HALO, POOL_G), F32)
    for g, w in enumerate(POOL_WINDOWS):
        cols = slice(g * POOL_G, (g + 1) * POOL_G)
        ug = u_ref[row0:row0 + seq_len, cols]
        pad_ref[0:POOL_HALO, :] = zeros
        pad_ref[POOL_HALO:POOL_HALO + seq_len, :] = ug
        pad_ref[POOL_HALO + seq_len:2 * POOL_HALO + seq_len, :] = zeros
        total = None
        for off in range(-(w // 2), w // 2):
            part = pad_ref[POOL_HALO + off:POOL_HALO + off + seq_len, :]
            total = part if total is None else total + part
        cnt = (jnp.minimum(t + w // 2, seq_len) - jnp.maximum(t - w // 2, 0)).astype(F32)
        d = (total / cnt - ug).astype(BF16)
        y = jnp.dot(d, pw_ref[g].astype(BF16), preferred_element_type=F32)
        o_ref[row0:row0 + seq_len, cols] = (y * scale_ref[:, cols]).astype(o_ref.dtype)


def _pool_kernel(u_ref, pw_ref, scale_ref, o_ref, pad_ref):
    unit = pl.program_id(0)

    @pl.when(unit < N_PROMPT_UNITS)
    def _():
        for s in range(SEQ_PER_PROMPT_UNIT):
            _pool_seq(u_ref, s * PROMPT_LEN, PROMPT_LEN, pw_ref, scale_ref, o_ref, pad_ref)

    @pl.when(unit >= N_PROMPT_UNITS)
    def _():
        _pool_seq(u_ref, 0, LAT_LEN, pw_ref, scale_ref, o_ref, pad_ref)


def _pool(proj, pool_w, pool_scale):
    return pl.pallas_call(
        _pool_kernel,
        grid=(N_UNITS,),
        in_specs=[
            pl.BlockSpec((UNIT, POOL_W), lambda u: (u, 0)),
            pl.BlockSpec((len(POOL_WINDOWS), POOL_G, POOL_G), lambda u: (0, 0, 0)),
            pl.BlockSpec((1, POOL_W), lambda u: (0, 0)),
        ],
        out_specs=pl.BlockSpec((UNIT, POOL_W), lambda u: (u, 0)),
        out_shape=jax.ShapeDtypeStruct((T, POOL_W), BF16),
        scratch_shapes=[pltpu.VMEM((LAT_LEN + 2 * POOL_HALO, POOL_G), F32)],
        compiler_params=_cparams(1),
        name="pool",
    )(proj, pool_w, pool_scale.reshape(1, POOL_W))


def _rope_tables():
    pos = jnp.arange(LAT_LEN)
    row = (pos // GRID_W).astype(F32)
    col = (pos % GRID_W).astype(F32)
    quarter = QK_ROPE // 4
    inv = ROPE_BASE ** (-jnp.arange(quarter, dtype=F32) / quarter)
    lane = jnp.arange(LANES)
    axis = (lane % QK_ROPE) // (QK_ROPE // 2)
    freq = inv[lane % quarter]
    p = jnp.where(axis[None, :] == 0, row[:, None], col[:, None])
    ang = p * freq[None, :]
    sign = jnp.where((lane % (QK_ROPE // 2)) < quarter, -1.0, 1.0).astype(F32)
    return jnp.cos(ang), jnp.sin(ang) * sign[None, :]


def _rope(x, cos, sin_signed):
    quarter = QK_ROPE // 4
    lane = lax.broadcasted_iota(jnp.int32, x.shape, 1)
    first_half = (lane % (QK_ROPE // 2)) < quarter
    partner = jnp.where(first_half, pltpu.roll(x, LANES - quarter, 1), pltpu.roll(x, quarter, 1))
    return x * cos + partner * sin_signed


def _diff_core(q, k, v, lam, g, lam_init):
    lane = lax.broadcasted_iota(jnp.int32, q.shape, 1)
    low = lane < DIFF_DH
    q1 = jnp.where(low, q, 0.0).astype(BF16)
    q2 = jnp.where(low, 0.0, q).astype(BF16)

    def probs(qm):
        s = lax.dot_general(qm, k, _NT, preferred_element_type=F32)
        e = jnp.exp(s - jnp.max(s, axis=-1, keepdims=True))
        return e * (1.0 / jnp.sum(e, axis=-1, keepdims=True))

    w = (probs(q1) - lam * probs(q2)).astype(BF16)
    o = jnp.dot(w, v, preferred_element_type=F32)
    return _rms(o, g) * (1.0 - lam_init)


def _diff_attn_kernel(lp_ref, q_ref, k_ref, v_ref, ck_ref, cv_ref, cos_ref, sin_ref, g_ref, o_ref,
                      *, lam_init):
    unit = pl.program_id(0)
    lp = lp_ref[...]
    lam = (jnp.exp(jnp.sum(lp[0:1] * lp[1:2], axis=-1, keepdims=True))
           - jnp.exp(jnp.sum(lp[2:3] * lp[3:4], axis=-1, keepdims=True)) + lam_init)
    g = g_ref[...]

    @pl.when(unit < N_PROMPT_UNITS)
    def _():
        for s in range(SEQ_PER_PROMPT_UNIT):
            rows = slice(s * PROMPT_LEN, (s + 1) * PROMPT_LEN)
            q = q_ref[rows, :] * DIFF_SCALE
            k = k_ref[rows, :].astype(BF16)
            v = v_ref[rows, :].astype(BF16)
            o_ref[rows, :] = _diff_core(q, k, v, lam, g, lam_init).astype(o_ref.dtype)

    @pl.when(unit >= N_PROMPT_UNITS)
    def _():
        cos, sin = cos_ref[...], sin_ref[...]
        k = jnp.concatenate([ck_ref[...].astype(BF16),
                             _rope(k_ref[...], cos, sin).astype(BF16)], axis=0)
        v = jnp.concatenate([cv_ref[...].astype(BF16), v_ref[...].astype(BF16)], axis=0)
        for b in range(LAT_LEN // ATT_QB):
            rows = slice(b * ATT_QB, (b + 1) * ATT_QB)
            q = _rope(q_ref[rows, :], cos[rows], sin[rows]) * DIFF_SCALE
            o_ref[rows, :] = _diff_core(q, k, v, lam, g, lam_init).astype(o_ref.dtype)


def _diff_attn(proj, cache_k, cache_v, lam_params, subln_g, cos, sin, lam_init):
    hd = 2 * DIFF_DH
    q0, k0, v0 = POOL_W // hd, (POOL_W + DIFF_W) // hd, (POOL_W + 2 * DIFF_W) // hd

    def ctx_map(u, h):
        return (jnp.maximum(u - N_PROMPT_UNITS, 0), h)

    return pl.pallas_call(
        functools.partial(_diff_attn_kernel, lam_init=lam_init),
        grid=(N_UNITS, DIFF_HEADS),
        in_specs=[
            pl.BlockSpec((4, DIFF_DH), lambda u, h: (0, 0)),
            pl.BlockSpec((UNIT, hd), lambda u, h: (u, q0 + h)),
            pl.BlockSpec((UNIT, hd), lambda u, h: (u, k0 + h)),
            pl.BlockSpec((UNIT, hd), lambda u, h: (u, v0 + h)),
            pl.BlockSpec((PAST_LEN, hd), ctx_map),
            pl.BlockSpec((PAST_LEN, hd), ctx_map),
            pl.BlockSpec((LAT_LEN, LANES), lambda u, h: (0, 0)),
            pl.BlockSpec((LAT_LEN, LANES), lambda u, h: (0, 0)),
            pl.BlockSpec((1, hd), lambda u, h: (0, 0)),
        ],
        out_specs=pl.BlockSpec((UNIT, hd), lambda u, h: (u, h)),
        out_shape=jax.ShapeDtypeStruct((T, DIFF_W), BF16),
        compiler_params=_cparams(2),
        name="diff_attn",
    )(lam_params, proj, proj, proj, cache_k, cache_v, cos, sin, subln_g.reshape(1, hd))


def _mla_core(qc, kc, v):
    s = lax.dot_general(qc, kc, _NT, preferred_element_type=F32) * MLA_SCALE
    e = jnp.exp(s - jnp.max(s, axis=-1, keepdims=True))
    p = (e * (1.0 / jnp.sum(e, axis=-1, keepdims=True))).astype(BF16)
    return jnp.dot(p, v, preferred_element_type=F32)


def _mla_attn_kernel(qn_ref, qp_ref, kn_ref, v_ref, kp_ref, ckn_ref, cv_ref, ckp_ref, cos_ref, sin_ref,
                     o_ref):
    unit = pl.program_id(0)
    head = pl.program_id(1)
    lane = lax.broadcasted_iota(jnp.int32, (ATT_QB, LANES), 1)
    mine = (lane // QK_ROPE) == (head % 2)

    @pl.when(unit < N_PROMPT_UNITS)
    def _():
        for s in range(SEQ_PER_PROMPT_UNIT):
            rows = slice(s * PROMPT_LEN, (s + 1) * PROMPT_LEN)
            qp = jnp.where(mine, qp_ref[rows, :], 0.0).astype(BF16)
            qc = jnp.concatenate([qn_ref[rows, :], qp], axis=1)
            kc = jnp.concatenate([kn_ref[rows, :], kp_ref[rows, :].astype(BF16)], axis=1)
            o_ref[rows, :] = _mla_core(qc, kc, v_ref[rows, :]).astype(o_ref.dtype)

    @pl.when(unit >= N_PROMPT_UNITS)
    def _():
        cos, sin = cos_ref[...], sin_ref[...]
        k_ctx = jnp.concatenate([ckn_ref[...], ckp_ref[...].astype(BF16)], axis=1)
        k_new = jnp.concatenate([kn_ref[...], _rope(kp_ref[...], cos, sin).astype(BF16)], axis=1)
        kc = jnp.concatenate([k_ctx, k_new], axis=0)
        v = jnp.concatenate([cv_ref[...], v_ref[...]], axis=0)
        for b in range(LAT_LEN // ATT_QB):
            rows = slice(b * ATT_QB, (b + 1) * ATT_QB)
            qp = jnp.where(mine, _rope(qp_ref[rows, :], cos[rows], sin[rows]), 0.0).astype(BF16)
            qc = jnp.concatenate([qn_ref[rows, :], qp], axis=1)
            o_ref[rows, :] = _mla_core(qc, kc, v).astype(o_ref.dtype)


def _mla_attn(q_nope, q_pe, kv, kp_dup, kv_ctx, kp_ctx_dup, cos, sin):
    def ctx_row(u):
        return jnp.maximum(u - N_PROMPT_UNITS, 0)

    return pl.pallas_call(
        _mla_attn_kernel,
        grid=(N_UNITS, MLA_HEADS),
        in_specs=[
            pl.BlockSpec((UNIT, QK_NOPE), lambda u, h: (u, h)),
            pl.BlockSpec((UNIT, LANES), lambda u, h: (u, h // 2)),
            pl.BlockSpec((UNIT, QK_NOPE), lambda u, h: (u, 2 * h)),
            pl.BlockSpec((UNIT, V_DIM), lambda u, h: (u, 2 * h + 1)),
            pl.BlockSpec((UNIT, LANES), lambda u, h: (u, 0)),
            pl.BlockSpec((PAST_LEN, QK_NOPE), lambda u, h: (ctx_row(u), 2 * h)),
            pl.BlockSpec((PAST_LEN, V_DIM), lambda u, h: (ctx_row(u), 2 * h + 1)),
            pl.BlockSpec((PAST_LEN, LANES), lambda u, h: (ctx_row(u), 0)),
            pl.BlockSpec((LAT_LEN, LANES), lambda u, h: (0, 0)),
            pl.BlockSpec((LAT_LEN, LANES), lambda u, h: (0, 0)),
        ],
        out_specs=pl.BlockSpec((UNIT, V_DIM), lambda u, h: (u, h)),
        out_shape=jax.ShapeDtypeStruct((T, MLA_HEADS * V_DIM), BF16),
        compiler_params=_cparams(2),
        name="mla_attn",
    )(q_nope, q_pe, kv, kv, kp_dup, kv_ctx, kv_ctx, kp_ctx_dup, cos, sin)


def _route_kernel(x_ref, g_ref, sh_ref, sc_ref, rw_ref, rb_ref,
                  h_ref, eidx_ref, wts_ref, cnt_ref, carry_ref):
    tm = ROUTE_TM

    @pl.when(pl.program_id(0) == 0)
    def _():
        carry_ref[...] = jnp.zeros_like(carry_ref)

    h = _rms(x_ref[...], g_ref[...]) * (1.0 + sc_ref[...]) + sh_ref[...]
    h_ref[...] = h
    logits = lax.dot_general(rw_ref[...], h, _NT, preferred_element_type=F32,
                             precision=lax.Precision.HIGHEST)
    scores = _sigmoid(logits)
    choice = scores + rb_ref[...]
    neg = -jnp.inf

    def take_max(vals, iota, n):
        m = jnp.max(vals, axis=0, keepdims=True)
        idx = jnp.min(jnp.where(vals == m, iota, n), axis=0, keepdims=True)
        return iota == idx, m, idx

    iota8 = lax.broadcasted_iota(jnp.int32, (8, tm), 0)

    def stack_rows(rows, dtype):
        out = jnp.zeros((8, tm), dtype)
        for k, r in enumerate(rows):
            out = jnp.where(iota8 == k, r.astype(dtype), out)
        return out

    gscore = []
    for gi in range(N_EGROUPS):
        grp = choice[gi * EGROUP:(gi + 1) * EGROUP]
        oh, m1, _ = take_max(grp, iota8, EGROUP)
        m2 = jnp.max(jnp.where(oh, neg, grp), axis=0, keepdims=True)
        gscore.append(m1 + m2)
    cur = stack_rows(gscore, F32)
    gsel = jnp.zeros((N_EGROUPS, tm), F32)
    for _ in range(TOPK_GROUPS):
        oh, _, _ = take_max(cur, iota8, N_EGROUPS)
        gsel = jnp.where(oh, 1.0, gsel)
        cur = jnp.where(oh, neg, cur)
    emask = jnp.concatenate(
        [jnp.broadcast_to(gsel[gi:gi + 1], (EGROUP, tm)) for gi in range(N_EGROUPS)], axis=0)
    masked = jnp.where(emask > 0.0, choice, neg)

    iota_e = lax.broadcasted_iota(jnp.int32, (N_EXPERTS, tm), 0)
    idxs, wsel = [], []
    sel = jnp.zeros((N_EXPERTS, tm), F32)
    for _ in range(TOP_K):
        oh, _, idx = take_max(masked, iota_e, N_EXPERTS)
        idxs.append(idx)
        wsel.append(jnp.sum(jnp.where(oh, scores, 0.0), axis=0, keepdims=True))
        masked = jnp.where(oh, neg, masked)
        sel = jnp.where(oh, 1.0, sel)
    wsum = wsel[0]
    for w in wsel[1:]:
        wsum = wsum + w

    carry_ref[...] = carry_ref[...] + jnp.sum(sel, axis=1, keepdims=True)
    cnt_ref[...] = carry_ref[...]
    eidx_ref[...] = stack_rows(idxs, jnp.int32)
    wts_ref[...] = stack_rows([w / wsum * ROUTED_SCALE for w in wsel], F32)


def _route(x, g, mods, layer, router_w, router_bias):
    tm = ROUTE_TM
    return pl.pallas_call(
        _route_kernel,
        grid=(T // tm,),
        in_specs=[
            pl.BlockSpec((tm, D), lambda i: (i, 0)),
            pl.BlockSpec((1, D), lambda i: (0, 0)),
            _mod_spec(layer, 3, tm, 1, 0),
            _mod_spec(layer, 4, tm, 1, 0),
            pl.BlockSpec((N_EXPERTS, D), lambda i: (0, 0)),
            pl.BlockSpec((N_EXPERTS, 1), lambda i: (0, 0)),
        ],
        out_specs=[
            pl.BlockSpec((tm, D), lambda i: (i, 0)),
            pl.BlockSpec((8, tm), lambda i: (0, i)),
            pl.BlockSpec((8, tm), lambda i: (0, i)),
            pl.BlockSpec((N_EXPERTS, 1), lambda i: (0, 0)),
        ],
        out_shape=[
            jax.ShapeDtypeStruct((T, D), F32),
            jax.ShapeDtypeStruct((8, T), jnp.int32),
            jax.ShapeDtypeStruct((8, T), F32),
            jax.ShapeDtypeStruct((N_EXPERTS, 1), F32),
        ],
        scratch_shapes=[pltpu.VMEM((N_EXPERTS, 1), F32)],
        compiler_params=_cparams(1),
        name="route",
    )(x, g.reshape(1, D), mods, mods, router_w.T, router_bias.reshape(N_EXPERTS, 1))


def _row_copy(src, src_row, dst, dst_row, sem):
    return pltpu.make_async_copy(src.at[pl.ds(src_row, 1), :], dst.at[pl.ds(dst_row, 1), :], sem)


def _combine_kernel(x_ref, sh_ref, w_ref, gate_ref, fg_ref, *rest, final):
    y_refs, o_ref = rest[:TOP_K], rest[TOP_K]
    w = w_ref[...]
    acc = sh_ref[...]
    for k in range(TOP_K):
        acc = acc + w[:, k:k + 1] * y_refs[k][...]
    xn = x_ref[...] + gate_ref[...] * acc
    if final:
        xn = _rms(xn, fg_ref[...])
    o_ref[...] = xn


def _combine(x, shared, wts_t, mods, layer, final_g, y, final):
    tm = COMBINE_TM
    return pl.pallas_call(
        functools.partial(_combine_kernel, final=final),
        grid=(T // tm,),
        in_specs=[
            pl.BlockSpec((tm, D), lambda i: (i, 0)),
            pl.BlockSpec((tm, D), lambda i: (i, 0)),
            pl.BlockSpec((tm, 8), lambda i: (i, 0)),
            pl.BlockSpec((None, None, 1, D), lambda i: (layer, _group_of_tile(i, tm), 0, 5)),
            pl.BlockSpec((1, D), lambda i: (0, 0)),
        ] + [pl.BlockSpec((tm, D), lambda i, k=k: (k * (T // tm) + i, 0)) for k in range(TOP_K)],
        out_specs=pl.BlockSpec((tm, D), lambda i: (i, 0)),
        out_shape=jax.ShapeDtypeStruct((T, D), F32),
        compiler_params=_cparams(1),
        name="combine",
    )(x, shared, wts_t, mods, final_g.reshape(1, D), *([y] * TOP_K))


def _swiglu(x, wgu_scr, wd_scr):
    gu = jnp.dot(x.astype(BF16), wgu_scr[...], preferred_element_type=F32)
    gate, up = gu[:, :D_EXPERT], gu[:, D_EXPERT:]
    act = (gate * _sigmoid(gate) * up).astype(BF16)
    return jnp.dot(act, wd_scr[...], preferred_element_type=F32)


def _load_expert_weights(wg_ref, wu_ref, wd_ref, wgu_scr, wd_scr):
    wgu_scr[:, :D_EXPERT] = wg_ref[...].astype(BF16)
    wgu_scr[:, D_EXPERT:] = wu_ref[...].astype(BF16)
    wd_scr[...] = wd_ref[...].astype(BF16)


def _shared_kernel(h_ref, wg_ref, wu_ref, wd_ref, o_ref, wgu_scr, wd_scr):
    @pl.when(pl.program_id(0) == 0)
    def _():
        _load_expert_weights(wg_ref, wu_ref, wd_ref, wgu_scr, wd_scr)

    o_ref[...] = _swiglu(h_ref[...], wgu_scr, wd_scr)


def _shared_expert(h, layer, w_gate, w_up, w_down):
    tm = 512
    return pl.pallas_call(
        _shared_kernel,
        grid=(T // tm,),
        in_specs=[
            pl.BlockSpec((tm, D), lambda i: (i, 0)),
            pl.BlockSpec((None, D, D_EXPERT), lambda i: (layer, 0, 0)),
            pl.BlockSpec((None, D, D_EXPERT), lambda i: (layer, 0, 0)),
            pl.BlockSpec((None, D_EXPERT, D), lambda i: (layer, 0, 0)),
        ],
        out_specs=pl.BlockSpec((tm, D), lambda i: (i, 0)),
        out_shape=jax.ShapeDtypeStruct((T, D), F32),
        scratch_shapes=[pltpu.VMEM((D, 2 * D_EXPERT), BF16), pltpu.VMEM((D_EXPERT, D), BF16)],
        compiler_params=_cparams(1),
        name="shared_expert",
    )(h, w_gate, w_up, w_down)


def _for_rows(first, last, fn):
    g = ROW_GROUP
    body_lo = jnp.minimum((first + g - 1) // g * g, last)
    body_hi = jnp.maximum(last // g * g, body_lo)

    def one(r, carry):
        fn(r)
        return carry

    def group(i, carry):
        for j in range(g):
            fn(i * g + j)
        return carry

    lax.fori_loop(first, body_lo, one, 0)
    lax.fori_loop(body_lo // g, body_hi // g, group, 0)
    lax.fori_loop(body_hi, last, one, 0)


def _experts_kernel(e_ref, inv_prev_ref, inv_next_ref, h_ref, wg_ref, wu_ref, wd_ref,
                    y_ref, xbuf0, xbuf1, ybuf0, ybuf1, gsem, ssem, wgu_scr, wd_scr):
    v = pl.program_id(0)
    n_v = pl.num_programs(0)
    new_expert = (v == 0) | (e_ref[v] != e_ref[jnp.maximum(v - 1, 0)])

    def gather(idx_ref, r, x_dst, sem):
        q = idx_ref[0, 0, r]
        token = jnp.where(q < N_PAIRS, q & (T - 1), 0)
        return _row_copy(h_ref, token, x_dst, r, sem)

    def scatter(idx_ref, r, y_src, trash_slot, sem, force_trash=False):
        q = idx_ref[0, 0, r]
        trash = N_PAIRS + trash_slot * GMM_TM + r
        row = jnp.where(q < N_PAIRS, q, trash)
        if force_trash is not False:
            row = jnp.where(force_trash, trash, row)
        return _row_copy(y_src, r, y_ref, row, sem)

    @pl.when(v == 0)
    def _():
        ybuf0[...] = jnp.zeros_like(ybuf0)
        ybuf1[...] = jnp.zeros_like(ybuf1)
        _for_rows(0, GMM_TM, lambda r: gather(inv_prev_ref, r, xbuf0, gsem.at[0]).start())
        _for_rows(0, GMM_TM, lambda r: scatter(inv_prev_ref, r, ybuf0, 0, ssem.at[0], True).start())

    @pl.when(new_expert)
    def _():
        _load_expert_weights(wg_ref, wu_ref, wd_ref, wgu_scr, wd_scr)

    def visit(slot, x_cur, x_nxt, y_cur, y_prv):
        for _ in range(GMM_TM):
            _row_copy(h_ref, 0, x_cur, 0, gsem.at[slot]).wait()
        for _ in range(GMM_TM):
            _row_copy(y_cur, 0, y_ref, 0, ssem.at[slot]).wait()
        for r in range(GMM_TM):
            gather(inv_next_ref, r, x_nxt, gsem.at[1 - slot]).start()
        for r in range(GMM_TM):
            scatter(inv_prev_ref, r, y_prv, 1 - slot, ssem.at[1 - slot], v == 0).start()
        y_cur[...] = _swiglu(x_cur[...], wgu_scr, wd_scr)

    @pl.when(v % 2 == 0)
    def _():
        visit(0, xbuf0, xbuf1, ybuf0, ybuf1)

    @pl.when(v % 2 == 1)
    def _():
        visit(1, xbuf1, xbuf0, ybuf1, ybuf0)

    @pl.when(v == n_v - 1)
    def _():
        _for_rows(0, GMM_TM, lambda r: scatter(inv_next_ref, r, ybuf1, 1, ssem.at[1]).start())
        _for_rows(0, GMM_TM, lambda r: _row_copy(h_ref, 0, xbuf0, 0, gsem.at[0]).wait())
        _for_rows(0, GMM_TM, lambda r: _row_copy(ybuf0, 0, y_ref, 0, ssem.at[0]).wait())
        _for_rows(0, GMM_TM, lambda r: _row_copy(ybuf1, 0, y_ref, 0, ssem.at[1]).wait())


def _experts(tile_expert, inv, h, layer, w_gate, w_up, w_down):
    assert N_VISITS % 2 == 0
    return pl.pallas_call(
        _experts_kernel,
        grid_spec=pltpu.PrefetchScalarGridSpec(
            num_scalar_prefetch=1,
            grid=(N_VISITS,),
            in_specs=[
                pl.BlockSpec((1, 1, GMM_TM), lambda v, e: (jnp.maximum(v - 1, 0), 0, 0),
                             memory_space=pltpu.SMEM),
                pl.BlockSpec((1, 1, GMM_TM), lambda v, e: (jnp.minimum(v + 1, N_VISITS - 1), 0, 0),
                             memory_space=pltpu.SMEM),
                pl.BlockSpec(memory_space=pl.ANY),
                pl.BlockSpec((None, None, D, D_EXPERT), lambda v, e: (layer, e[v], 0, 0)),
                pl.BlockSpec((None, None, D, D_EXPERT), lambda v, e: (layer, e[v], 0, 0)),
                pl.BlockSpec((None, None, D_EXPERT, D), lambda v, e: (layer, e[v], 0, 0)),
            ],
            out_specs=pl.BlockSpec(memory_space=pl.ANY),
            scratch_shapes=[
                pltpu.VMEM((GMM_TM, D), F32),
                pltpu.VMEM((GMM_TM, D), F32),
                pltpu.VMEM((GMM_TM, D), F32),
                pltpu.VMEM((GMM_TM, D), F32),
                pltpu.SemaphoreType.DMA((2,)),
                pltpu.SemaphoreType.DMA((2,)),
                pltpu.VMEM((D, 2 * D_EXPERT), BF16),
                pltpu.VMEM((D_EXPERT, D), BF16),
            ],
        ),
        out_shape=jax.ShapeDtypeStruct((Y_ROWS, D), F32),
        compiler_params=_cparams(1),
        name="experts",
    )(tile_expert, inv, inv, h, w_gate, w_up, w_down)


def _padded_order(eidx, counts):
    tm = GMM_TM
    tiles = (counts + tm - 1) // tm
    tile_end = jnp.cumsum(tiles)
    pad_end = jnp.cumsum(tiles * tm - counts)
    key_real = 2 * (eidx * T + jnp.arange(T, dtype=jnp.int32)[None, :]).reshape(N_PAIRS)
    d = jnp.arange(N_VISITS * tm - N_PAIRS, dtype=jnp.int32)
    pad_expert = jnp.sum((pad_end[None, :] <= d[:, None]).astype(jnp.int32), axis=1)
    key_pad = jnp.where(pad_expert < N_EXPERTS, 2 * (pad_expert + 1) * T - 1, jnp.iinfo(jnp.int32).max)
    inv = jnp.argsort(jnp.concatenate([key_real, key_pad])).astype(jnp.int32)
    v = jnp.minimum(jnp.arange(N_VISITS, dtype=jnp.int32), tile_end[-1] - 1)
    tile_expert = jnp.sum((tile_end[None, :] <= v[:, None]).astype(jnp.int32), axis=1)
    return inv, tile_expert


def _moe(x, mods, layer, norm_g, router_w, router_bias, w_gate, w_up, w_down,
         ws_gate, ws_up, ws_down, final_g, final):
    h, eidx, wts, counts = _route(x, norm_g, mods, layer, router_w, router_bias)
    inv, tile_expert = _padded_order(eidx[:TOP_K], counts.reshape(N_EXPERTS).astype(jnp.int32))
    y = _experts(tile_expert, inv.reshape(N_VISITS, 1, GMM_TM), h, layer, w_gate, w_up, w_down)
    shared = _shared_expert(h, layer, ws_gate, ws_up, ws_down)
    return _combine(x, shared, wts.T, mods, layer, final_g, y, final)


def _pool_diff_layer(x, mods, layer, j, norm_g, cache_k, cache_v, cos, sin,
                     diff_w_in, pool_w, pool_scale, lq1, lk1, lq2, lk2, subln_g, w_out):
    lam_init = 0.8 - 0.6 * math.exp(-0.3 * layer)
    h = _norm_mod(x, norm_g, mods, layer, 0, 1)
    proj = _matmul([h], diff_w_in[j], [0], POOL_W + 3 * DIFF_W, F32, tn=1024, name="diff_in_proj")
    y_pool = _pool(proj, pool_w[j], pool_scale[j])
    lam_params = jnp.stack([lq1[j], lk1[j], lq2[j], lk2[j]])
    ck = cache_k[:, j].reshape(N_LAT_SEQ * PAST_LEN, DIFF_W)
    cv = cache_v[:, j].reshape(N_LAT_SEQ * PAST_LEN, DIFF_W)
    o = _diff_attn(proj, ck, cv, lam_params, subln_g[j], cos, sin, lam_init)
    x = _matmul([y_pool, o], w_out[j], [0, 1], D, F32, tn=1024, epilogue="residual",
                residual=x, mods=mods, layer=layer, gate_chunk=2, name="diff_out_proj")
    k_new = proj[:T_PROMPT, POOL_W + DIFF_W:POOL_W + 2 * DIFF_W]
    v_new = proj[:T_PROMPT, POOL_W + 2 * DIFF_W:]
    shape = (N_PROMPT_SEQ, PROMPT_LEN, DIFF_HEADS, 2 * DIFF_DH)
    return x, k_new.reshape(shape), v_new.reshape(shape)


def _mla_layer(x, mods, layer, j, norm_g, cache_ckv, cache_kpe, cos, sin,
               w_dq, q_norm_g, w_uq, w_dkv, kv_norm_g, w_ukv, w_o):
    h = _norm_mod(x, norm_g, mods, layer, 0, 1)
    cq = _matmul([h], w_dq[j], [0], Q_LORA, BF16, epilogue="rms", gain=q_norm_g[j], name="mla_dq")
    ckv = _matmul([h], w_dkv[j], [0], KV_LORA, F32, epilogue="rms", gain=kv_norm_g[j], name="mla_dkv")
    w_kpe = w_dkv[j][:, KV_LORA:]
    kp_dup = _matmul([h], jnp.concatenate([w_kpe, w_kpe], axis=1), [0], LANES, F32, name="mla_kpe")
    w_uq3 = w_uq[j].reshape(Q_LORA, MLA_HEADS, QK_NOPE + QK_ROPE)
    w_uq_nope = w_uq3[:, :, :QK_NOPE].reshape(Q_LORA, MLA_HEADS * QK_NOPE)
    w_uq_pe = w_uq3[:, :, QK_NOPE:].reshape(Q_LORA, MLA_HEADS * QK_ROPE)
    q_nope = _matmul([cq], w_uq_nope, [0], MLA_HEADS * QK_NOPE, BF16, tn=1024, name="mla_uq_nope")
    q_pe = _matmul([cq], w_uq_pe, [0], MLA_HEADS * QK_ROPE, F32, tn=1024, name="mla_uq_pe")
    n_kv = MLA_HEADS * (QK_NOPE + V_DIM)
    kv = _matmul([ckv], w_ukv[j], [0], n_kv, BF16, tn=1024, name="mla_ukv")
    ckv_ctx = cache_ckv[:, j].reshape(N_LAT_SEQ * PAST_LEN, KV_LORA)
    kv_ctx = _matmul([ckv_ctx], w_ukv[j], [0], n_kv, BF16, tn=1024, name="mla_ukv_ctx")
    kpe_ctx = cache_kpe[:, j].reshape(N_LAT_SEQ * PAST_LEN, QK_ROPE)
    kp_ctx_dup = jnp.concatenate([kpe_ctx, kpe_ctx], axis=1)
    o = _mla_attn(q_nope, q_pe, kv, kp_dup, kv_ctx, kp_ctx_dup, cos, sin)
    x = _matmul([o], w_o[j], [0], D, F32, tn=1024, epilogue="residual",
                residual=x, mods=mods, layer=layer, gate_chunk=2, name="mla_out_proj")
    new_ckv = ckv[:T_PROMPT].reshape(N_PROMPT_SEQ, PROMPT_LEN, KV_LORA)
    new_kpe = kp_dup[:T_PROMPT, :QK_ROPE].reshape(N_PROMPT_SEQ, PROMPT_LEN, QK_ROPE)
    return x, new_ckv, new_kpe


def kernel(x_prompt, x_sample, cache_diff_k, cache_diff_v, cache_mla_ckv, cache_mla_kpe, c, c_ctx,
           ada_w, ada_b, norm_mix_g, norm_ffn_g, final_norm_g,
           diff_w_in, pool_w, pool_scale, diff_lambda_q1, diff_lambda_k1, diff_lambda_q2, diff_lambda_k2,
           diff_subln_g, even_w_out,
           mla_w_dq, mla_q_norm_g, mla_w_uq, mla_w_dkv, mla_kv_norm_g, mla_w_ukv, mla_w_o,
           router_w, router_bias, expert_w_gate, expert_w_up, expert_w_down,
           shared_w_gate, shared_w_up, shared_w_down):
    depth = ada_w.shape[0]
    x = jnp.concatenate([x_prompt.reshape(T_PROMPT, D), x_sample.reshape(T_LAT, D)], axis=0)
    cond = jnp.concatenate(
        [c_ctx[None, :], c, jnp.zeros((N_GROUPS_PAD - 1 - N_LAT_SEQ, D), F32)], axis=0)
    mods = _ada_params(cond, ada_w, ada_b)
    cos, sin = _rope_tables()

    new_dk, new_dv, new_ckv, new_kpe = [], [], [], []
    for i in range(depth):
        j = i // 2
        if i % 2 == 0:
            x, k_new, v_new = _pool_diff_layer(
                x, mods, i, j, norm_mix_g[i], cache_diff_k, cache_diff_v, cos, sin,
                diff_w_in, pool_w, pool_scale, diff_lambda_q1, diff_lambda_k1, diff_lambda_q2,
                diff_lambda_k2, diff_subln_g, even_w_out)
            new_dk.append(k_new)
            new_dv.append(v_new)
        else:
            x, ckv, kpe = _mla_layer(
                x, mods, i, j, norm_mix_g[i], cache_mla_ckv, cache_mla_kpe, cos, sin,
                mla_w_dq, mla_q_norm_g, mla_w_uq, mla_w_dkv, mla_kv_norm_g, mla_w_ukv, mla_w_o)
            new_ckv.append(ckv)
            new_kpe.append(kpe)
        x = _moe(x, mods, i, norm_ffn_g[i], router_w[i], router_bias[i],
                 expert_w_gate, expert_w_up, expert_w_down,
                 shared_w_gate, shared_w_up, shared_w_down,
                 final_norm_g, final=(i == depth - 1))
    y_prompt = x[:T_PROMPT].reshape(N_PROMPT_SEQ, PROMPT_LEN, D)
    y_sample = x[T_PROMPT:].reshape(N_LAT_SEQ, LAT_LEN, D)
    return (y_prompt, y_sample, jnp.stack(new_dk, axis=1), jnp.stack(new_dv, axis=1),
            jnp.stack(new_ckv, axis=1), jnp.stack(new_kpe, axis=1))
```

```python
import functools
import math

import jax
import jax.numpy as jnp
from jax import lax
from jax.experimental import pallas as pl
from jax.experimental.pallas import tpu as pltpu

F32 = jnp.float32
BF16 = jnp.bfloat16

D = 2048
N_PROMPT_SEQ = 16
PROMPT_LEN = 256
N_LAT_SEQ = 4
LAT_LEN = 1024
PAST_LEN = 256
T_PROMPT = N_PROMPT_SEQ * PROMPT_LEN
T_LAT = N_LAT_SEQ * LAT_LEN
T = T_PROMPT + T_LAT
UNIT = 1024
N_UNITS = T // UNIT
N_PROMPT_UNITS = T_PROMPT // UNIT
SEQ_PER_PROMPT_UNIT = UNIT // PROMPT_LEN
N_GROUPS_PAD = 8

GRID_W = 64
ROPE_BASE = 10000.0
EPS = 1e-6
POOL_W = 1024
POOL_WINDOWS = (2, 4, 8, 16)
POOL_G = 256
POOL_HALO = 8
DIFF_W = 1024
DIFF_DH = 64
DIFF_HEADS = 8
DIFF_SCALE = DIFF_DH ** -0.5
MLA_HEADS = 16
Q_LORA = 512
KV_LORA = 512
QK_NOPE = 128
QK_ROPE = 64
V_DIM = 128
MLA_SCALE = (QK_NOPE + QK_ROPE) ** -0.5
N_EXPERTS = 64
TOP_K = 6
N_EGROUPS = 8
EGROUP = N_EXPERTS // N_EGROUPS
TOPK_GROUPS = 4
D_EXPERT = 512
ROUTED_SCALE = 2.5

LANES = 128
ATT_QB = 256
ROUTE_TM = 256
COMBINE_TM = 128
GMM_TM = 256
N_PAIRS = T * TOP_K
N_ROW_TILES = N_PAIRS // GMM_TM
N_VISITS = N_ROW_TILES + N_EXPERTS
VMEM_LIMIT = 56 * 1024 * 1024

_NT = (((1,), (1,)), ((), ()))


def _cparams(n_axes, vmem=VMEM_LIMIT):
    return pltpu.CompilerParams(dimension_semantics=("arbitrary",) * n_axes, vmem_limit_bytes=vmem)


def _group_of_tile(i, tm):
    n_p = T_PROMPT // tm
    per_seq = LAT_LEN // tm
    return jnp.where(i < n_p, 0, 1 + (i - n_p) // per_seq)


def _sigmoid(x):
    return 1.0 / (1.0 + jnp.exp(-x))


def _rms(x, g):
    return x * lax.rsqrt(jnp.mean(x * x, axis=-1, keepdims=True) + EPS) * g


def _ada_kernel(c_ref, w_ref, b_ref, o_ref):
    c = c_ref[...]
    s = c * _sigmoid(c)
    o_ref[...] = jnp.dot(s, w_ref[...], preferred_element_type=F32,
                         precision=lax.Precision.HIGHEST) + b_ref[...]


def _ada_params(cond, ada_w, ada_b):
    depth, _, n = ada_w.shape
    tn = 1024
    out = pl.pallas_call(
        _ada_kernel,
        grid=(depth, n // tn),
        in_specs=[
            pl.BlockSpec((N_GROUPS_PAD, D), lambda l, j: (0, 0)),
            pl.BlockSpec((None, D, tn), lambda l, j: (l, 0, j)),
            pl.BlockSpec((None, 1, tn), lambda l, j: (l, 0, j)),
        ],
        out_specs=pl.BlockSpec((None, N_GROUPS_PAD, tn), lambda l, j: (l, 0, j)),
        out_shape=jax.ShapeDtypeStruct((depth, N_GROUPS_PAD, n), F32),
        compiler_params=_cparams(2),
        name="ada_params",
    )(cond, ada_w, ada_b.reshape(depth, 1, n))
    return out.reshape(depth, N_GROUPS_PAD, 1, n)


def _mod_spec(layer, chunk, tm, grid_rank, row_axis):
    def index_map(*ids):
        return (layer, _group_of_tile(ids[row_axis], tm), 0, chunk)
    del grid_rank
    return pl.BlockSpec((None, None, 1, D), index_map)


def _norm_mod_kernel(x_ref, g_ref, sh_ref, sc_ref, o_ref):
    y = _rms(x_ref[...], g_ref[...])
    o_ref[...] = (y * (1.0 + sc_ref[...]) + sh_ref[...]).astype(o_ref.dtype)


def _norm_mod(x, g, mods, layer, shift_chunk, scale_chunk):
    tm = 512
    return pl.pallas_call(
        _norm_mod_kernel,
        grid=(T // tm,),
        in_specs=[
            pl.BlockSpec((tm, D), lambda i: (i, 0)),
            pl.BlockSpec((1, D), lambda i: (0, 0)),
            _mod_spec(layer, shift_chunk, tm, 1, 0),
            _mod_spec(layer, scale_chunk, tm, 1, 0),
        ],
        out_specs=pl.BlockSpec((tm, D), lambda i: (i, 0)),
        out_shape=jax.ShapeDtypeStruct((T, D), BF16),
        compiler_params=_cparams(1),
        name="norm_mod",
    )(x, g.reshape(1, D), mods, mods)


def _mm_kernel(*refs, n_a, epilogue):
    a_refs = refs[:n_a]
    w_refs = refs[n_a:2 * n_a]
    pos = 2 * n_a
    if epilogue == "rms":
        g_ref = refs[pos]
        pos += 1
    elif epilogue == "residual":
        res_ref, gate_ref = refs[pos], refs[pos + 1]
        pos += 2
    o_ref = refs[pos]
    w_scr = refs[pos + 1:]

    @pl.when(pl.program_id(1) == 0)
    def _():
        for w_ref, scr in zip(w_refs, w_scr):
            scr[...] = w_ref[...].astype(BF16)

    acc = None
    for a_ref, scr in zip(a_refs, w_scr):
        part = jnp.dot(a_ref[...].astype(BF16), scr[...], preferred_element_type=F32)
        acc = part if acc is None else acc + part
    if epilogue == "rms":
        acc = _rms(acc, g_ref[...])
    elif epilogue == "residual":
        acc = res_ref[...] + gate_ref[...] * acc
    o_ref[...] = acc.astype(o_ref.dtype)


def _matmul(a_list, w, w_row_blocks, n_out, out_dtype, *, tm=512, tn=None, epilogue=None,
            gain=None, residual=None, mods=None, layer=None, gate_chunk=None, name="matmul"):
    m = a_list[0].shape[0]
    tn = n_out if tn is None else tn
    n_a = len(a_list)
    in_specs, args = [], []
    for a in a_list:
        k = a.shape[1]
        in_specs.append(pl.BlockSpec((tm, k), lambda j, i: (i, 0)))
        args.append(a)
    scratch = []
    for a, rb in zip(a_list, w_row_blocks):
        k = a.shape[1]
        in_specs.append(pl.BlockSpec((k, tn), lambda j, i, rb=rb: (rb, j)))
        args.append(w)
        scratch.append(pltpu.VMEM((k, tn), BF16))
    if epilogue == "rms":
        assert tn == n_out
        in_specs.append(pl.BlockSpec((1, tn), lambda j, i: (0, 0)))
        args.append(gain.reshape(1, n_out))
    elif epilogue == "residual":
        in_specs.append(pl.BlockSpec((tm, tn), lambda j, i: (i, j)))
        args.append(residual)
        in_specs.append(pl.BlockSpec(
            (None, None, 1, tn),
            lambda j, i: (layer, _group_of_tile(i, tm), 0, gate_chunk * (D // tn) + j)))
        args.append(mods)
    return pl.pallas_call(
        functools.partial(_mm_kernel, n_a=n_a, epilogue=epilogue),
        grid=(n_out // tn, m // tm),
        in_specs=in_specs,
        out_specs=pl.BlockSpec((tm, tn), lambda j, i: (i, j)),
        out_shape=jax.ShapeDtypeStruct((m, n_out), out_dtype),
        scratch_shapes=scratch,
        compiler_params=_cparams(2),
        name=name,
    )(*args)


def _pool_seq(u_ref, row0, seq_len, pw_ref, scale_ref, o_ref, pad_ref):
    t = lax.broadcasted_iota(jnp.int32, (seq_len, 1), 0)
    zeros = jnp.zeros((POOL_HALO, POOL_G), F32)
    for g, w in enumerate(POOL_WINDOWS):
        cols = slice(g * POOL_G, (g + 1) * POOL_G)
        ug = u_ref[row0:row0 + seq_len, cols]
        pad_ref[0:POOL_HALO, :] = zeros
        pad_ref[POOL_HALO:POOL_HALO + seq_len, :] = ug
        pad_ref[POOL_HALO + seq_len:2 * POOL_HALO + seq_len, :] = zeros
        total = None
        for off in range(-(w // 2), w // 2):
            part = pad_ref[POOL_HALO + off:POOL_HALO + off + seq_len, :]
            total = part if total is None else total + part
        cnt = (jnp.minimum(t + w // 2, seq_len) - jnp.maximum(t - w // 2, 0)).astype(F32)
        d = (total / cnt - ug).astype(BF16)
        y = jnp.dot(d, pw_ref[g].astype(BF16), preferred_element_type=F32)
        o_ref[row0:row0 + seq_len, cols] = (y * scale_ref[:, cols]).astype(o_ref.dtype)


def _pool_kernel(u_ref, pw_ref, scale_ref, o_ref, pad_ref):
    unit = pl.program_id(0)

    @pl.when(unit < N_PROMPT_UNITS)
    def _():
        for s in range(SEQ_PER_PROMPT_UNIT):
            _pool_seq(u_ref, s * PROMPT_LEN, PROMPT_LEN, pw_ref, scale_ref, o_ref, pad_ref)

    @pl.when(unit >= N_PROMPT_UNITS)
    def _():
        _pool_seq(u_ref, 0, LAT_LEN, pw_ref, scale_ref, o_ref, pad_ref)


def _pool(proj, pool_w, pool_scale):
    return pl.pallas_call(
        _pool_kernel,
        grid=(N_UNITS,),
        in_specs=[
            pl.BlockSpec((UNIT, POOL_W), lambda u: (u, 0)),
            pl.BlockSpec((len(POOL_WINDOWS), POOL_G, POOL_G), lambda u: (0, 0, 0)),
            pl.BlockSpec((1, POOL_W), lambda u: (0, 0)),
        ],
        out_specs=pl.BlockSpec((UNIT, POOL_W), lambda u: (u, 0)),
        out_shape=jax.ShapeDtypeStruct((T, POOL_W), BF16),
        scratch_shapes=[pltpu.VMEM((LAT_LEN + 2 * POOL_HALO, POOL_G), F32)],
        compiler_params=_cparams(1),
        name="pool",
    )(proj, pool_w, pool_scale.reshape(1, POOL_W))


def _rope_tables():
    pos = jnp.arange(LAT_LEN)
    row = (pos // GRID_W).astype(F32)
    col = (pos % GRID_W).astype(F32)
    quarter = QK_ROPE // 4
    inv = ROPE_BASE ** (-jnp.arange(quarter, dtype=F32) / quarter)
    lane = jnp.arange(LANES)
    axis = (lane % QK_ROPE) // (QK_ROPE // 2)
    freq = inv[lane % quarter]
    p = jnp.where(axis[None, :] == 0, row[:, None], col[:, None])
    ang = p * freq[None, :]
    sign = jnp.where((lane % (QK_ROPE // 2)) < quarter, -1.0, 1.0).astype(F32)
    return jnp.cos(ang), jnp.sin(ang) * sign[None, :]


def _rope(x, cos, sin_signed):
    quarter = QK_ROPE // 4
    lane = lax.broadcasted_iota(jnp.int32, x.shape, 1)
    first_half = (lane % (QK_ROPE // 2)) < quarter
    partner = jnp.where(first_half, pltpu.roll(x, LANES - quarter, 1), pltpu.roll(x, quarter, 1))
    return x * cos + partner * sin_signed


def _diff_core(q, k, v, lam, g, lam_init):
    lane = lax.broadcasted_iota(jnp.int32, q.shape, 1)
    low = lane < DIFF_DH
    q1 = jnp.where(low, q, 0.0).astype(BF16)
    q2 = jnp.where(low, 0.0, q).astype(BF16)

    def probs(qm):
        s = lax.dot_general(qm, k, _NT, preferred_element_type=F32)
        e = jnp.exp(s - jnp.max(s, axis=-1, keepdims=True))
        return e * (1.0 / jnp.sum(e, axis=-1, keepdims=True))

    w = (probs(q1) - lam * probs(q2)).astype(BF16)
    o = jnp.dot(w, v, preferred_element_type=F32)
    return _rms(o, g) * (1.0 - lam_init)


def _diff_attn_kernel(lp_ref, q_ref, k_ref, v_ref, ck_ref, cv_ref, cos_ref, sin_ref, g_ref, o_ref,
                      *, lam_init):
    unit = pl.program_id(0)
    lp = lp_ref[...]
    lam = (jnp.exp(jnp.sum(lp[0:1] * lp[1:2], axis=-1, keepdims=True))
           - jnp.exp(jnp.sum(lp[2:3] * lp[3:4], axis=-1, keepdims=True)) + lam_init)
    g = g_ref[...]

    @pl.when(unit < N_PROMPT_UNITS)
    def _():
        for s in range(SEQ_PER_PROMPT_UNIT):
            rows = slice(s * PROMPT_LEN, (s + 1) * PROMPT_LEN)
            q = q_ref[rows, :] * DIFF_SCALE
            k = k_ref[rows, :].astype(BF16)
            v = v_ref[rows, :].astype(BF16)
            o_ref[rows, :] = _diff_core(q, k, v, lam, g, lam_init).astype(o_ref.dtype)

    @pl.when(unit >= N_PROMPT_UNITS)
    def _():
        cos, sin = cos_ref[...], sin_ref[...]
        k = jnp.concatenate([ck_ref[...].astype(BF16),
                             _rope(k_ref[...], cos, sin).astype(BF16)], axis=0)
        v = jnp.concatenate([cv_ref[...].astype(BF16), v_ref[...].astype(BF16)], axis=0)
        for b in range(LAT_LEN // ATT_QB):
            rows = slice(b * ATT_QB, (b + 1) * ATT_QB)
            q = _rope(q_ref[rows, :], cos[rows], sin[rows]) * DIFF_SCALE
            o_ref[rows, :] = _diff_core(q, k, v, lam, g, lam_init).astype(o_ref.dtype)


def _diff_attn(proj, cache_k, cache_v, lam_params, subln_g, cos, sin, lam_init):
    hd = 2 * DIFF_DH
    q0, k0, v0 = POOL_W // hd, (POOL_W + DIFF_W) // hd, (POOL_W + 2 * DIFF_W) // hd

    def ctx_map(u, h):
        return (jnp.maximum(u - N_PROMPT_UNITS, 0), h)

    return pl.pallas_call(
        functools.partial(_diff_attn_kernel, lam_init=lam_init),
        grid=(N_UNITS, DIFF_HEADS),
        in_specs=[
            pl.BlockSpec((4, DIFF_DH), lambda u, h: (0, 0)),
            pl.BlockSpec((UNIT, hd), lambda u, h: (u, q0 + h)),
            pl.BlockSpec((UNIT, hd), lambda u, h: (u, k0 + h)),
            pl.BlockSpec((UNIT, hd), lambda u, h: (u, v0 + h)),
            pl.BlockSpec((PAST_LEN, hd), ctx_map),
            pl.BlockSpec((PAST_LEN, hd), ctx_map),
            pl.BlockSpec((LAT_LEN, LANES), lambda u, h: (0, 0)),
            pl.BlockSpec((LAT_LEN, LANES), lambda u, h: (0, 0)),
            pl.BlockSpec((1, hd), lambda u, h: (0, 0)),
        ],
        out_specs=pl.BlockSpec((UNIT, hd), lambda u, h: (u, h)),
        out_shape=jax.ShapeDtypeStruct((T, DIFF_W), BF16),
        compiler_params=_cparams(2),
        name="diff_attn",
    )(lam_params, proj, proj, proj, cache_k, cache_v, cos, sin, subln_g.reshape(1, hd))


def _mla_core(qc, kc, v):
    s = lax.dot_general(qc, kc, _NT, preferred_element_type=F32) * MLA_SCALE
    e = jnp.exp(s - jnp.max(s, axis=-1, keepdims=True))
    p = (e * (1.0 / jnp.sum(e, axis=-1, keepdims=True))).astype(BF16)
    return jnp.dot(p, v, preferred_element_type=F32)


def _mla_attn_kernel(qn_ref, qp_ref, kn_ref, v_ref, kp_ref, ckn_ref, cv_ref, ckp_ref, cos_ref, sin_ref,
                     o_ref):
    unit = pl.program_id(0)
    head = pl.program_id(1)
    lane = lax.broadcasted_iota(jnp.int32, (ATT_QB, LANES), 1)
    mine = (lane // QK_ROPE) == (head % 2)

    @pl.when(unit < N_PROMPT_UNITS)
    def _():
        for s in range(SEQ_PER_PROMPT_UNIT):
            rows = slice(s * PROMPT_LEN, (s + 1) * PROMPT_LEN)
            qp = jnp.where(mine, qp_ref[rows, :], 0.0).astype(BF16)
            qc = jnp.concatenate([qn_ref[rows, :], qp], axis=1)
            kc = jnp.concatenate([kn_ref[rows, :], kp_ref[rows, :].astype(BF16)], axis=1)
            o_ref[rows, :] = _mla_core(qc, kc, v_ref[rows, :]).astype(o_ref.dtype)

    @pl.when(unit >= N_PROMPT_UNITS)
    def _():
        cos, sin = cos_ref[...], sin_ref[...]
        k_ctx = jnp.concatenate([ckn_ref[...], ckp_ref[...].astype(BF16)], axis=1)
        k_new = jnp.concatenate([kn_ref[...], _rope(kp_ref[...], cos, sin).astype(BF16)], axis=1)
        kc = jnp.concatenate([k_ctx, k_new], axis=0)
        v = jnp.concatenate([cv_ref[...], v_ref[...]], axis=0)
        for b in range(LAT_LEN // ATT_QB):
            rows = slice(b * ATT_QB, (b + 1) * ATT_QB)
            qp = jnp.where(mine, _rope(qp_ref[rows, :], cos[rows], sin[rows]), 0.0).astype(BF16)
            qc = jnp.concatenate([qn_ref[rows, :], qp], axis=1)
            o_ref[rows, :] = _mla_core(qc, kc, v).astype(o_ref.dtype)


def _mla_attn(q_nope, q_pe, kv, kp_dup, kv_ctx, kp_ctx_dup, cos, sin):
    def ctx_row(u):
        return jnp.maximum(u - N_PROMPT_UNITS, 0)

    return pl.pallas_call(
        _mla_attn_kernel,
        grid=(N_UNITS, MLA_HEADS),
        in_specs=[
            pl.BlockSpec((UNIT, QK_NOPE), lambda u, h: (u, h)),
            pl.BlockSpec((UNIT, LANES), lambda u, h: (u, h // 2)),
            pl.BlockSpec((UNIT, QK_NOPE), lambda u, h: (u, 2 * h)),
            pl.BlockSpec((UNIT, V_DIM), lambda u, h: (u, 2 * h + 1)),
            pl.BlockSpec((UNIT, LANES), lambda u, h: (u, 0)),
            pl.BlockSpec((PAST_LEN, QK_NOPE), lambda u, h: (ctx_row(u), 2 * h)),
            pl.BlockSpec((PAST_LEN, V_DIM), lambda u, h: (ctx_row(u), 2 * h + 1)),
            pl.BlockSpec((PAST_LEN, LANES), lambda u, h: (ctx_row(u), 0)),
            pl.BlockSpec((LAT_LEN, LANES), lambda u, h: (0, 0)),
            pl.BlockSpec((LAT_LEN, LANES), lambda u, h: (0, 0)),
        ],
        out_specs=pl.BlockSpec((UNIT, V_DIM), lambda u, h: (u, h)),
        out_shape=jax.ShapeDtypeStruct((T, MLA_HEADS * V_DIM), BF16),
        compiler_params=_cparams(2),
        name="mla_attn",
    )(q_nope, q_pe, kv, kv, kp_dup, kv_ctx, kv_ctx, kp_ctx_dup, cos, sin)


def _route_kernel(x_ref, g_ref, sh_ref, sc_ref, rw_ref, rb_ref,
                  h_ref, eidx_ref, rank_ref, wts_ref, cnt_ref, carry_ref):
    tm = ROUTE_TM

    @pl.when(pl.program_id(0) == 0)
    def _():
        carry_ref[...] = jnp.zeros_like(carry_ref)

    h = _rms(x_ref[...], g_ref[...]) * (1.0 + sc_ref[...]) + sh_ref[...]
    h_ref[...] = h
    logits = lax.dot_general(rw_ref[...], h, _NT, preferred_element_type=F32,
                             precision=lax.Precision.HIGHEST)
    scores = _sigmoid(logits)
    choice = scores + rb_ref[...]
    neg = -jnp.inf

    def take_max(vals, iota, n):
        m = jnp.max(vals, axis=0, keepdims=True)
        idx = jnp.min(jnp.where(vals == m, iota, n), axis=0, keepdims=True)
        return iota == idx, m, idx

    iota8 = lax.broadcasted_iota(jnp.int32, (8, tm), 0)

    def stack_rows(rows, dtype):
        out = jnp.zeros((8, tm), dtype)
        for k, r in enumerate(rows):
            out = jnp.where(iota8 == k, r.astype(dtype), out)
        return out

    gscore = []
    for gi in range(N_EGROUPS):
        grp = choice[gi * EGROUP:(gi + 1) * EGROUP]
        oh, m1, _ = take_max(grp, iota8, EGROUP)
        m2 = jnp.max(jnp.where(oh, neg, grp), axis=0, keepdims=True)
        gscore.append(m1 + m2)
    cur = stack_rows(gscore, F32)
    gsel = jnp.zeros((N_EGROUPS, tm), F32)
    for _ in range(TOPK_GROUPS):
        oh, _, _ = take_max(cur, iota8, N_EGROUPS)
        gsel = jnp.where(oh, 1.0, gsel)
        cur = jnp.where(oh, neg, cur)
    emask = jnp.concatenate(
        [jnp.broadcast_to(gsel[gi:gi + 1], (EGROUP, tm)) for gi in range(N_EGROUPS)], axis=0)
    masked = jnp.where(emask > 0.0, choice, neg)

    iota_e = lax.broadcasted_iota(jnp.int32, (N_EXPERTS, tm), 0)
    onehots, idxs, wsel = [], [], []
    sel = jnp.zeros((N_EXPERTS, tm), F32)
    for _ in range(TOP_K):
        oh, _, idx = take_max(masked, iota_e, N_EXPERTS)
        onehots.append(oh)
        idxs.append(idx)
        wsel.append(jnp.sum(jnp.where(oh, scores, 0.0), axis=0, keepdims=True))
        masked = jnp.where(oh, neg, masked)
        sel = jnp.where(oh, 1.0, sel)
    wsum = wsel[0]
    for w in wsel[1:]:
        wsum = wsum + w

    before = jnp.where(lax.broadcasted_iota(jnp.int32, (tm, tm), 0)
                       < lax.broadcasted_iota(jnp.int32, (tm, tm), 1), 1.0, 0.0).astype(BF16)
    rank_all = carry_ref[...] + jnp.dot(sel.astype(BF16), before, preferred_element_type=F32)
    carry_ref[...] = carry_ref[...] + jnp.sum(sel, axis=1, keepdims=True)
    cnt_ref[...] = carry_ref[...]

    ranks = [jnp.sum(jnp.where(oh, rank_all, 0.0), axis=0, keepdims=True) for oh in onehots]
    eidx_ref[...] = stack_rows(idxs, jnp.int32)
    rank_ref[...] = stack_rows(ranks, jnp.int32)
    wts_ref[...] = stack_rows([w / wsum * ROUTED_SCALE for w in wsel], F32)


def _route(x, g, mods, layer, router_w, router_bias):
    tm = ROUTE_TM
    return pl.pallas_call(
        _route_kernel,
        grid=(T // tm,),
        in_specs=[
            pl.BlockSpec((tm, D), lambda i: (i, 0)),
            pl.BlockSpec((1, D), lambda i: (0, 0)),
            _mod_spec(layer, 3, tm, 1, 0),
            _mod_spec(layer, 4, tm, 1, 0),
            pl.BlockSpec((N_EXPERTS, D), lambda i: (0, 0)),
            pl.BlockSpec((N_EXPERTS, 1), lambda i: (0, 0)),
        ],
        out_specs=[
            pl.BlockSpec((tm, D), lambda i: (i, 0)),
            pl.BlockSpec((8, tm), lambda i: (0, i)),
            pl.BlockSpec((8, tm), lambda i: (0, i)),
            pl.BlockSpec((8, tm), lambda i: (0, i)),
            pl.BlockSpec((N_EXPERTS, 1), lambda i: (0, 0)),
        ],
        out_shape=[
            jax.ShapeDtypeStruct((T, D), F32),
            jax.ShapeDtypeStruct((8, T), jnp.int32),
            jax.ShapeDtypeStruct((8, T), jnp.int32),
            jax.ShapeDtypeStruct((8, T), F32),
            jax.ShapeDtypeStruct((N_EXPERTS, 1), F32),
        ],
        scratch_shapes=[pltpu.VMEM((N_EXPERTS, 1), F32)],
        compiler_params=_cparams(1),
        name="route",
    )(x, g.reshape(1, D), mods, mods, router_w.T, router_bias.reshape(N_EXPERTS, 1))


def _row_copy(src, src_row, dst, dst_row, sem):
    return pltpu.make_async_copy(src.at[pl.ds(src_row, 1), :], dst.at[pl.ds(dst_row, 1), :], sem)


def _tile_positions(pos, tm):
    return pos.reshape(TOP_K, T // tm, tm).transpose(1, 0, 2).reshape(T // tm, 1, TOP_K * tm)


def _dispatch_kernel(pos_ref, h_ref, xs_ref, sem):
    tm = ROUTE_TM

    def issue(j, carry):
        for k in range(TOP_K):
            _row_copy(h_ref, j, xs_ref, pos_ref[0, 0, k * tm + j], sem).start(priority=k % 2)
        return carry

    def drain(j, carry):
        for _ in range(TOP_K):
            _row_copy(h_ref, 0, xs_ref, 0, sem).wait()
        return carry

    lax.fori_loop(0, tm, issue, 0, unroll=2)
    lax.fori_loop(0, tm, drain, 0, unroll=2)


def _dispatch(pos, h):
    tm = ROUTE_TM
    return pl.pallas_call(
        _dispatch_kernel,
        grid=(T // tm,),
        in_specs=[
            pl.BlockSpec((1, 1, TOP_K * tm), lambda i: (i, 0, 0), memory_space=pltpu.SMEM),
            pl.BlockSpec((tm, D), lambda i: (i, 0)),
        ],
        out_specs=pl.BlockSpec(memory_space=pl.ANY),
        out_shape=jax.ShapeDtypeStruct((N_PAIRS, D), F32),
        scratch_shapes=[pltpu.SemaphoreType.DMA(())],
        compiler_params=_cparams(1),
        name="dispatch",
    )(_tile_positions(pos, tm), h)


def _combine_kernel(pos_ref, pos_next_ref, x_ref, sh_ref, w_ref, gate_ref, fg_ref, ys_ref, *rest, final):
    out_refs, (buf_ref, sem) = rest[:-2], rest[-2:]
    tm = COMBINE_TM
    i = pl.program_id(0)
    n = pl.num_programs(0)
    slot = i % 2

    def start_gathers(p_ref, dst_slot):
        def issue(j, carry):
            for k in range(TOP_K):
                _row_copy(ys_ref, p_ref[0, 0, k * tm + j], buf_ref.at[dst_slot, k], j,
                          sem.at[dst_slot]).start(priority=k % 2)
            return carry
        lax.fori_loop(0, tm, issue, 0, unroll=2)

    @pl.when(i == 0)
    def _():
        start_gathers(pos_ref, 0)

    @pl.when(i + 1 < n)
    def _():
        start_gathers(pos_next_ref, 1 - slot)

    def drain(j, carry):
        for k in range(TOP_K):
            _row_copy(ys_ref, 0, buf_ref.at[slot, k], 0, sem.at[slot]).wait()
        return carry

    lax.fori_loop(0, tm, drain, 0, unroll=2)
    w = w_ref[...]
    acc = sh_ref[...]
    for k in range(TOP_K):
        acc = acc + w[:, k:k + 1] * buf_ref[slot, k]
    xn = x_ref[...] + gate_ref[...] * acc
    if not final:
        out_refs[0][...] = xn
    else:
        xn = _rms(xn, fg_ref[...])

        @pl.when(i < T_PROMPT // tm)
        def _():
            out_refs[0][...] = xn

        @pl.when(i >= T_PROMPT // tm)
        def _():
            out_refs[1][...] = xn


def _combine(pos, x, shared, wts_t, mods, layer, final_g, ys, final):
    tm = COMBINE_TM
    n_p = T_PROMPT // tm
    if final:
        out_specs = [pl.BlockSpec((tm, D), lambda i: (jnp.minimum(i, n_p - 1), 0)),
                     pl.BlockSpec((tm, D), lambda i: (jnp.maximum(i - n_p, 0), 0))]
        out_shape = [jax.ShapeDtypeStruct((T_PROMPT, D), F32), jax.ShapeDtypeStruct((T_LAT, D), F32)]
    else:
        out_specs = [pl.BlockSpec((tm, D), lambda i: (i, 0))]
        out_shape = [jax.ShapeDtypeStruct((T, D), F32)]
    tiles = _tile_positions(pos, tm)
    return pl.pallas_call(
        functools.partial(_combine_kernel, final=final),
        grid=(T // tm,),
        in_specs=[
            pl.BlockSpec((1, 1, TOP_K * tm), lambda i: (i, 0, 0), memory_space=pltpu.SMEM),
            pl.BlockSpec((1, 1, TOP_K * tm), lambda i: (jnp.minimum(i + 1, T // tm - 1), 0, 0),
                         memory_space=pltpu.SMEM),
            pl.BlockSpec((tm, D), lambda i: (i, 0)),
            pl.BlockSpec((tm, D), lambda i: (i, 0)),
            pl.BlockSpec((tm, 8), lambda i: (i, 0)),
            pl.BlockSpec((None, None, 1, D), lambda i: (layer, _group_of_tile(i, tm), 0, 5)),
            pl.BlockSpec((1, D), lambda i: (0, 0)),
            pl.BlockSpec(memory_space=pl.ANY),
        ],
        out_specs=out_specs,
        out_shape=out_shape,
        scratch_shapes=[pltpu.VMEM((2, TOP_K, tm, D), F32), pltpu.SemaphoreType.DMA((2,))],
        compiler_params=_cparams(1),
        name="combine",
    )(tiles, tiles, x, shared, wts_t, mods, final_g.reshape(1, D), ys)


def _swiglu(x, wgu_scr, wd_scr):
    gu = jnp.dot(x.astype(BF16), wgu_scr[...], preferred_element_type=F32)
    gate, up = gu[:, :D_EXPERT], gu[:, D_EXPERT:]
    act = (gate * _sigmoid(gate) * up).astype(BF16)
    return jnp.dot(act, wd_scr[...], preferred_element_type=F32)


def _load_expert_weights(wg_ref, wu_ref, wd_ref, wgu_scr, wd_scr):
    wgu_scr[:, :D_EXPERT] = wg_ref[...].astype(BF16)
    wgu_scr[:, D_EXPERT:] = wu_ref[...].astype(BF16)
    wd_scr[...] = wd_ref[...].astype(BF16)


def _shared_kernel(h_ref, wg_ref, wu_ref, wd_ref, o_ref, wgu_scr, wd_scr):
    @pl.when(pl.program_id(0) == 0)
    def _():
        _load_expert_weights(wg_ref, wu_ref, wd_ref, wgu_scr, wd_scr)

    o_ref[...] = _swiglu(h_ref[...], wgu_scr, wd_scr)


def _shared_expert(h, layer, w_gate, w_up, w_down):
    tm = 512
    return pl.pallas_call(
        _shared_kernel,
        grid=(T // tm,),
        in_specs=[
            pl.BlockSpec((tm, D), lambda i: (i, 0)),
            pl.BlockSpec((None, D, D_EXPERT), lambda i: (layer, 0, 0)),
            pl.BlockSpec((None, D, D_EXPERT), lambda i: (layer, 0, 0)),
            pl.BlockSpec((None, D_EXPERT, D), lambda i: (layer, 0, 0)),
        ],
        out_specs=pl.BlockSpec((tm, D), lambda i: (i, 0)),
        out_shape=jax.ShapeDtypeStruct((T, D), F32),
        scratch_shapes=[pltpu.VMEM((D, 2 * D_EXPERT), BF16), pltpu.VMEM((D_EXPERT, D), BF16)],
        compiler_params=_cparams(1),
        name="shared_expert",
    )(h, w_gate, w_up, w_down)


def _gmm_kernel(e_ref, t_ref, lo_ref, hi_ref, xs_ref, wg_ref, wu_ref, wd_ref, ys_ref, wgu_scr, wd_scr):
    v = pl.program_id(0)
    prev = jnp.maximum(v - 1, 0)
    new_expert = (v == 0) | (e_ref[v] != e_ref[prev])
    new_tile = (v == 0) | (t_ref[v] != t_ref[prev])
    lo, hi = lo_ref[v], hi_ref[v]

    @pl.when(new_expert)
    def _():
        _load_expert_weights(wg_ref, wu_ref, wd_ref, wgu_scr, wd_scr)

    @pl.when(hi > lo)
    def _():
        y = _swiglu(xs_ref[...], wgu_scr, wd_scr)
        row = lax.broadcasted_iota(jnp.int32, (GMM_TM, 1), 0)
        mine = (row >= lo) & (row < hi)

        @pl.when(new_tile)
        def _():
            ys_ref[...] = jnp.where(mine, y, 0.0)

        @pl.when(jnp.logical_not(new_tile))
        def _():
            ys_ref[...] = jnp.where(mine, y, ys_ref[...])


def _gmm(sched, xs, layer, w_gate, w_up, w_down):
    return pl.pallas_call(
        _gmm_kernel,
        grid_spec=pltpu.PrefetchScalarGridSpec(
            num_scalar_prefetch=4,
            grid=(N_VISITS,),
            in_specs=[
                pl.BlockSpec((GMM_TM, D), lambda v, e, t, lo, hi: (t[v], 0)),
                pl.BlockSpec((None, None, D, D_EXPERT), lambda v, e, t, lo, hi: (layer, e[v], 0, 0)),
                pl.BlockSpec((None, None, D, D_EXPERT), lambda v, e, t, lo, hi: (layer, e[v], 0, 0)),
                pl.BlockSpec((None, None, D_EXPERT, D), lambda v, e, t, lo, hi: (layer, e[v], 0, 0)),
            ],
            out_specs=pl.BlockSpec((GMM_TM, D), lambda v, e, t, lo, hi: (t[v], 0)),
            scratch_shapes=[pltpu.VMEM((D, 2 * D_EXPERT), BF16), pltpu.VMEM((D_EXPERT, D), BF16)],
        ),
        out_shape=jax.ShapeDtypeStruct((N_PAIRS, D), F32),
        compiler_params=_cparams(1),
        name="experts",
    )(*sched, xs, w_gate, w_up, w_down)


def _visit_schedule(counts):
    tm = GMM_TM
    ends = jnp.cumsum(counts)
    starts = ends - counts
    first_tile = starts // tm
    last_tile = jnp.maximum(ends - 1, 0) // tm
    n_vis = jnp.where(counts > 0, last_tile - first_tile + 1, 0)
    vis_end = jnp.cumsum(n_vis)
    vis_start = vis_end - n_vis
    total = vis_end[-1]
    v = jnp.minimum(jnp.arange(N_VISITS, dtype=jnp.int32), total - 1)
    e = jnp.sum((vis_end[None, :] <= v[:, None]).astype(jnp.int32), axis=1)
    tile = first_tile[e] + (v - vis_start[e])
    lo = jnp.clip(starts[e] - tile * tm, 0, tm)
    hi = jnp.clip(ends[e] - tile * tm, 0, tm)
    hi = jnp.where(jnp.arange(N_VISITS) < total, hi, lo)
    return starts.astype(jnp.int32), (e, tile.astype(jnp.int32), lo.astype(jnp.int32), hi.astype(jnp.int32))


def _moe(x, mods, layer, norm_g, router_w, router_bias, w_gate, w_up, w_down,
         ws_gate, ws_up, ws_down, final_g, final):
    h, eidx, rank, wts, counts = _route(x, norm_g, mods, layer, router_w, router_bias)
    starts, sched = _visit_schedule(counts.reshape(N_EXPERTS).astype(jnp.int32))
    eidx, rank = eidx[:TOP_K], rank[:TOP_K]
    expert_ids = jnp.arange(N_EXPERTS, dtype=jnp.int32)[:, None, None]
    pos = rank + jnp.sum(jnp.where(eidx[None] == expert_ids, starts[:, None, None], 0), axis=0)
    xs = _dispatch(pos, h)
    ys = _gmm(sched, xs, layer, w_gate, w_up, w_down)
    shared = _shared_expert(h, layer, ws_gate, ws_up, ws_down)
    return _combine(pos, x, shared, wts.T, mods, layer, final_g, ys, final)


def _pool_diff_layer(x, mods, layer, j, norm_g, cache_k, cache_v, cos, sin,
                     diff_w_in, pool_w, pool_scale, lq1, lk1, lq2, lk2, subln_g, w_out):
    lam_init = 0.8 - 0.6 * math.exp(-0.3 * layer)
    h = _norm_mod(x, norm_g, mods, layer, 0, 1)
    proj = _matmul([h], diff_w_in[j], [0], POOL_W + 3 * DIFF_W, F32, tn=1024, name="diff_in_proj")
    y_pool = _pool(proj, pool_w[j], pool_scale[j])
    lam_params = jnp.stack([lq1[j], lk1[j], lq2[j], lk2[j]])
    ck = cache_k[:, j].reshape(N_LAT_SEQ * PAST_LEN, DIFF_W)
    cv = cache_v[:, j].reshape(N_LAT_SEQ * PAST_LEN, DIFF_W)
    o = _diff_attn(proj, ck, cv, lam_params, subln_g[j], cos, sin, lam_init)
    x = _matmul([y_pool, o], w_out[j], [0, 1], D, F32, tn=1024, epilogue="residual",
                residual=x, mods=mods, layer=layer, gate_chunk=2, name="diff_out_proj")
    k_new = proj[:T_PROMPT, POOL_W + DIFF_W:POOL_W + 2 * DIFF_W]
    v_new = proj[:T_PROMPT, POOL_W + 2 * DIFF_W:]
    shape = (N_PROMPT_SEQ, PROMPT_LEN, DIFF_HEADS, 2 * DIFF_DH)
    return x, k_new.reshape(shape), v_new.reshape(shape)


def _mla_layer(x, mods, layer, j, norm_g, cache_ckv, cache_kpe, cos, sin,
               w_dq, q_norm_g, w_uq, w_dkv, kv_norm_g, w_ukv, w_o):
    h = _norm_mod(x, norm_g, mods, layer, 0, 1)
    cq = _matmul([h], w_dq[j], [0], Q_LORA, BF16, epilogue="rms", gain=q_norm_g[j], name="mla_dq")
    ckv = _matmul([h], w_dkv[j], [0], KV_LORA, F32, epilogue="rms", gain=kv_norm_g[j], name="mla_dkv")
    w_kpe = w_dkv[j][:, KV_LORA:]
    kp_dup = _matmul([h], jnp.concatenate([w_kpe, w_kpe], axis=1), [0], LANES, F32, name="mla_kpe")
    w_uq3 = w_uq[j].reshape(Q_LORA, MLA_HEADS, QK_NOPE + QK_ROPE)
    w_uq_nope = w_uq3[:, :, :QK_NOPE].reshape(Q_LORA, MLA_HEADS * QK_NOPE)
    w_uq_pe = w_uq3[:, :, QK_NOPE:].reshape(Q_LORA, MLA_HEADS * QK_ROPE)
    q_nope = _matmul([cq], w_uq_nope, [0], MLA_HEADS * QK_NOPE, BF16, tn=1024, name="mla_uq_nope")
    q_pe = _matmul([cq], w_uq_pe, [0], MLA_HEADS * QK_ROPE, F32, tn=1024, name="mla_uq_pe")
    n_kv = MLA_HEADS * (QK_NOPE + V_DIM)
    kv = _matmul([ckv], w_ukv[j], [0], n_kv, BF16, tn=1024, name="mla_ukv")
    ckv_ctx = cache_ckv[:, j].reshape(N_LAT_SEQ * PAST_LEN, KV_LORA)
    kv_ctx = _matmul([ckv_ctx], w_ukv[j], [0], n_kv, BF16, tn=1024, name="mla_ukv_ctx")
    kpe_ctx = cache_kpe[:, j].reshape(N_LAT_SEQ * PAST_LEN, QK_ROPE)
    kp_ctx_dup = jnp.concatenate([kpe_ctx, kpe_ctx], axis=1)
    o = _mla_attn(q_nope, q_pe, kv, kp_dup, kv_ctx, kp_ctx_dup, cos, sin)
    x = _matmul([o], w_o[j], [0], D, F32, tn=1024, epilogue="residual",
                residual=x, mods=mods, layer=layer, gate_chunk=2, name="mla_out_proj")
    new_ckv = ckv[:T_PROMPT].reshape(N_PROMPT_SEQ, PROMPT_LEN, KV_LORA)
    new_kpe = kp_dup[:T_PROMPT, :QK_ROPE].reshape(N_PROMPT_SEQ, PROMPT_LEN, QK_ROPE)
    return x, new_ckv, new_kpe


def kernel(x_prompt, x_sample, cache_diff_k, cache_diff_v, cache_mla_ckv, cache_mla_kpe, c, c_ctx,
           ada_w, ada_b, norm_mix_g, norm_ffn_g, final_norm_g,
           diff_w_in, pool_w, pool_scale, diff_lambda_q1, diff_lambda_k1, diff_lambda_q2, diff_lambda_k2,
           diff_subln_g, even_w_out,
           mla_w_dq, mla_q_norm_g, mla_w_uq, mla_w_dkv, mla_kv_norm_g, mla_w_ukv, mla_w_o,
           router_w, router_bias, expert_w_gate, expert_w_up, expert_w_down,
           shared_w_gate, shared_w_up, shared_w_down):
    depth = ada_w.shape[0]
    x = jnp.concatenate([x_prompt.reshape(T_PROMPT, D), x_sample.reshape(T_LAT, D)], axis=0)
    cond = jnp.concatenate(
        [c_ctx[None, :], c, jnp.zeros((N_GROUPS_PAD - 1 - N_LAT_SEQ, D), F32)], axis=0)
    mods = _ada_params(cond, ada_w, ada_b)
    cos, sin = _rope_tables()

    new_dk, new_dv, new_ckv, new_kpe = [], [], [], []
    for i in range(depth):
        j = i // 2
        if i % 2 == 0:
            x, k_new, v_new = _pool_diff_layer(
                x, mods, i, j, norm_mix_g[i], cache_diff_k, cache_diff_v, cos, sin,
                diff_w_in, pool_w, pool_scale, diff_lambda_q1, diff_lambda_k1, diff_lambda_q2,
                diff_lambda_k2, diff_subln_g, even_w_out)
            new_dk.append(k_new)
            new_dv.append(v_new)
        else:
            x, ckv, kpe = _mla_layer(
                x, mods, i, j, norm_mix_g[i], cache_mla_ckv, cache_mla_kpe, cos, sin,
                mla_w_dq, mla_q_norm_g, mla_w_uq, mla_w_dkv, mla_kv_norm_g, mla_w_ukv, mla_w_o)
            new_ckv.append(ckv)
            new_kpe.append(kpe)
        outs = _moe(x, mods, i, norm_ffn_g[i], router_w[i], router_bias[i],
                    expert_w_gate, expert_w_up, expert_w_down,
                    shared_w_gate, shared_w_up, shared_w_down,
                    final_norm_g, final=(i == depth - 1))
        x = outs[0]
    y_prompt = outs[0].reshape(N_PROMPT_SEQ, PROMPT_LEN, D)
    y_sample = outs[1].reshape(N_LAT_SEQ, LAT_LEN, D)
    return (y_prompt, y_sample, jnp.stack(new_dk, axis=1), jnp.stack(new_dv, axis=1),
            jnp.stack(new_ckv, axis=1), jnp.stack(new_kpe, axis=1))
```

```python
import functools
import math

import jax
import jax.numpy as jnp
from jax import lax
from jax.experimental import pallas as pl
from jax.experimental.pallas import tpu as pltpu

F32 = jnp.float32
BF16 = jnp.bfloat16

D = 2048
N_PROMPT_SEQ = 16
PROMPT_LEN = 256
N_LAT_SEQ = 4
LAT_LEN = 1024
PAST_LEN = 256
T_PROMPT = N_PROMPT_SEQ * PROMPT_LEN
T_LAT = N_LAT_SEQ * LAT_LEN
T = T_PROMPT + T_LAT
UNIT = 1024
N_UNITS = T // UNIT
N_PROMPT_UNITS = T_PROMPT // UNIT
SEQ_PER_PROMPT_UNIT = UNIT // PROMPT_LEN
N_GROUPS_PAD = 8

GRID_W = 64
ROPE_BASE = 10000.0
EPS = 1e-6
POOL_W = 1024
POOL_WINDOWS = (2, 4, 8, 16)
POOL_G = 256
POOL_HALO = 8
DIFF_W = 1024
DIFF_DH = 64
DIFF_HEADS = 8
DIFF_SCALE = DIFF_DH ** -0.5
MLA_HEADS = 16
Q_LORA = 512
KV_LORA = 512
QK_NOPE = 128
QK_ROPE = 64
V_DIM = 128
MLA_SCALE = (QK_NOPE + QK_ROPE) ** -0.5
N_EXPERTS = 64
TOP_K = 6
N_EGROUPS = 8
EGROUP = N_EXPERTS // N_EGROUPS
TOPK_GROUPS = 4
D_EXPERT = 512
ROUTED_SCALE = 2.5

LANES = 128
ATT_QB = 256
ROUTE_TM = 256
COMBINE_TM = 128
GMM_TM = 256
N_PAIRS = T * TOP_K
N_ROW_TILES = N_PAIRS // GMM_TM
N_VISITS = N_ROW_TILES + N_EXPERTS
VMEM_LIMIT = 56 * 1024 * 1024

_NT = (((1,), (1,)), ((), ()))
_NN = (((1,), (0,)), ((), ()))


def _cparams(n_axes, vmem=VMEM_LIMIT):
    return pltpu.CompilerParams(dimension_semantics=("arbitrary",) * n_axes, vmem_limit_bytes=vmem)


def _group_of_tile(i, tm):
    n_p = T_PROMPT // tm
    per_seq = LAT_LEN // tm
    return jnp.where(i < n_p, 0, 1 + (i - n_p) // per_seq)


def _sigmoid(x):
    return 1.0 / (1.0 + jnp.exp(-x))


def _rms(x, g):
    return x * lax.rsqrt(jnp.mean(x * x, axis=-1, keepdims=True) + EPS) * g


def _dot_3pass(a, b, dims):
    a_hi = a.astype(BF16)
    a_lo = (a - a_hi.astype(F32)).astype(BF16)
    b_hi = b.astype(BF16)
    b_lo = (b - b_hi.astype(F32)).astype(BF16)

    def dot(p, q):
        return lax.dot_general(p, q, dims, preferred_element_type=F32)

    return dot(a_hi, b_hi) + (dot(a_hi, b_lo) + dot(a_lo, b_hi))


def _ada_kernel(c_ref, w_ref, b_ref, o_ref):
    c = c_ref[...]
    s = c * _sigmoid(c)
    o_ref[...] = _dot_3pass(s, w_ref[...], _NN) + b_ref[...]


def _ada_params(cond, ada_w, ada_b):
    depth, _, n = ada_w.shape
    tn = 1024
    out = pl.pallas_call(
        _ada_kernel,
        grid=(depth, n // tn),
        in_specs=[
            pl.BlockSpec((N_GROUPS_PAD, D), lambda l, j: (0, 0)),
            pl.BlockSpec((None, D, tn), lambda l, j: (l, 0, j)),
            pl.BlockSpec((None, 1, tn), lambda l, j: (l, 0, j)),
        ],
        out_specs=pl.BlockSpec((None, N_GROUPS_PAD, tn), lambda l, j: (l, 0, j)),
        out_shape=jax.ShapeDtypeStruct((depth, N_GROUPS_PAD, n), F32),
        compiler_params=_cparams(2),
        name="ada_params",
    )(cond, ada_w, ada_b.reshape(depth, 1, n))
    return out.reshape(depth, N_GROUPS_PAD, 1, n)


def _mod_spec(layer, chunk, tm, grid_rank, row_axis):
    def index_map(*ids):
        return (layer, _group_of_tile(ids[row_axis], tm), 0, chunk)
    del grid_rank
    return pl.BlockSpec((None, None, 1, D), index_map)


def _row_source(x):
    return list(x) if isinstance(x, (tuple, list)) else [x]


def _row_source_specs(parts, tm, width, col_of, row_axis):
    if len(parts) == 1:
        return [pl.BlockSpec((tm, width), lambda *ids: (ids[row_axis], col_of(*ids)))]
    n_p = T_PROMPT // tm
    return [pl.BlockSpec((tm, width), lambda *ids: (jnp.minimum(ids[row_axis], n_p - 1), col_of(*ids))),
            pl.BlockSpec((tm, width), lambda *ids: (jnp.maximum(ids[row_axis] - n_p, 0), col_of(*ids)))]


def _read_rows(refs, row_tile, tm):
    if len(refs) == 1:
        return refs[0][...]
    return jnp.where(row_tile < T_PROMPT // tm, refs[0][...], refs[1][...])


def _norm_mod_kernel(*refs, n_x, tm):
    x_refs = refs[:n_x]
    g_ref, sh_ref, sc_ref, o_ref = refs[n_x:]
    y = _rms(_read_rows(x_refs, pl.program_id(0), tm), g_ref[...])
    o_ref[...] = (y * (1.0 + sc_ref[...]) + sh_ref[...]).astype(o_ref.dtype)


def _norm_mod(x, g, mods, layer, shift_chunk, scale_chunk):
    tm = 512
    parts = _row_source(x)
    return pl.pallas_call(
        functools.partial(_norm_mod_kernel, n_x=len(parts), tm=tm),
        grid=(T // tm,),
        in_specs=_row_source_specs(parts, tm, D, lambda i: 0, 0) + [
            pl.BlockSpec((1, D), lambda i: (0, 0)),
            _mod_spec(layer, shift_chunk, tm, 1, 0),
            _mod_spec(layer, scale_chunk, tm, 1, 0),
        ],
        out_specs=pl.BlockSpec((tm, D), lambda i: (i, 0)),
        out_shape=jax.ShapeDtypeStruct((T, D), BF16),
        compiler_params=_cparams(1),
        name="norm_mod",
    )(*parts, g.reshape(1, D), mods, mods)


def _mm_kernel(*refs, n_a, epilogue, n_res, tm):
    a_refs = refs[:n_a]
    w_refs = refs[n_a:2 * n_a]
    pos = 2 * n_a
    if epilogue == "rms":
        g_ref = refs[pos]
        pos += 1
    elif epilogue == "residual":
        res_refs, gate_ref = refs[pos:pos + n_res], refs[pos + n_res]
        pos += n_res + 1
    o_ref = refs[pos]
    w_scr = refs[pos + 1:]

    @pl.when(pl.program_id(1) == 0)
    def _():
        for w_ref, scr in zip(w_refs, w_scr):
            scr[...] = w_ref[...].astype(BF16)

    acc = None
    for a_ref, scr in zip(a_refs, w_scr):
        part = jnp.dot(a_ref[...].astype(BF16), scr[...], preferred_element_type=F32)
        acc = part if acc is None else acc + part
    if epilogue == "rms":
        acc = _rms(acc, g_ref[...])
    elif epilogue == "residual":
        acc = _read_rows(res_refs, pl.program_id(1), tm) + gate_ref[...] * acc
    o_ref[...] = acc.astype(o_ref.dtype)


def _matmul(a_list, w, w_row_blocks, n_out, out_dtype, *, tm=512, tn=None, epilogue=None,
            gain=None, residual=None, mods=None, layer=None, gate_chunk=None, name="matmul"):
    m = a_list[0].shape[0]
    tn = n_out if tn is None else tn
    n_a = len(a_list)
    in_specs, args = [], []
    for a in a_list:
        k = a.shape[1]
        in_specs.append(pl.BlockSpec((tm, k), lambda j, i: (i, 0)))
        args.append(a)
    scratch = []
    for a, rb in zip(a_list, w_row_blocks):
        k = a.shape[1]
        in_specs.append(pl.BlockSpec((k, tn), lambda j, i, rb=rb: (rb, j)))
        args.append(w)
        scratch.append(pltpu.VMEM((k, tn), BF16))
    if epilogue == "rms":
        assert tn == n_out
        in_specs.append(pl.BlockSpec((1, tn), lambda j, i: (0, 0)))
        args.append(gain.reshape(1, n_out))
    elif epilogue == "residual":
        res_parts = _row_source(residual)
        in_specs += _row_source_specs(res_parts, tm, tn, lambda j, i: j, 1)
        args += res_parts
        in_specs.append(pl.BlockSpec(
            (None, None, 1, tn),
            lambda j, i: (layer, _group_of_tile(i, tm), 0, gate_chunk * (D // tn) + j)))
        args.append(mods)
    return pl.pallas_call(
        functools.partial(_mm_kernel, n_a=n_a, epilogue=epilogue, tm=tm,
                          n_res=len(_row_source(residual)) if epilogue == "residual" else 0),
        grid=(n_out // tn, m // tm),
        in_specs=in_specs,
        out_specs=pl.BlockSpec((tm, tn), lambda j, i: (i, j)),
        out_shape=jax.ShapeDtypeStruct((m, n_out), out_dtype),
        scratch_shapes=scratch,
        compiler_params=_cparams(2),
        name=name,
    )(*args)


def _pool_seq(u_ref, row0, seq_len, pw_ref, scale_ref, o_ref, pad_ref):
    t = lax.broadcasted_iota(jnp.int32, (seq_len, 1), 0)
    zeros = jnp.zeros((POOL_HALO, POOL_G), F32)
    for g, w in enumerate(POOL_WINDOWS):
        cols = slice(g * POOL_G, (g + 1) * POOL_G)
        ug = u_ref[row0:row0 + seq_len, cols]
        pad_ref[0:POOL_HALO, :] = zeros
        pad_ref[POOL_HALO:POOL_HALO + seq_len, :] = ug
        pad_ref[POOL_HALO + seq_len:2 * POOL_HALO + seq_len, :] = zeros
        total = None
        for off in range(-(w // 2), w // 2):
            part = pad_ref[POOL_HALO + off:POOL_HALO + off + seq_len, :]
            total = part if total is None else total + part
        cnt = (jnp.minimum(t + w // 2, seq_len) - jnp.maximum(t - w // 2, 0)).astype(F32)
        d = (total / cnt - ug).astype(BF16)
        y = jnp.dot(d, pw_ref[g].astype(BF16), preferred_element_type=F32)
        o_ref[row0:row0 + seq_len, cols] = (y * scale_ref[:, cols]).astype(o_ref.dtype)


def _pool_kernel(u_ref, pw_ref, scale_ref, o_ref, pad_ref):
    unit = pl.program_id(0)

    @pl.when(unit < N_PROMPT_UNITS)
    def _():
        for s in range(SEQ_PER_PROMPT_UNIT):
            _pool_seq(u_ref, s * PROMPT_LEN, PROMPT_LEN, pw_ref, scale_ref, o_ref, pad_ref)

    @pl.when(unit >= N_PROMPT_UNITS)
    def _():
        _pool_seq(u_ref, 0, LAT_LEN, pw_ref, scale_ref, o_ref, pad_ref)


def _pool(proj, pool_w, pool_scale):
    return pl.pallas_call(
        _pool_kernel,
        grid=(N_UNITS,),
        in_specs=[
            pl.BlockSpec((UNIT, POOL_W), lambda u: (u, 0)),
            pl.BlockSpec((len(POOL_WINDOWS), POOL_G, POOL_G), lambda u: (0, 0, 0)),
            pl.BlockSpec((1, POOL_W), lambda u: (0, 0)),
        ],
        out_specs=pl.BlockSpec((UNIT, POOL_W), lambda u: (u, 0)),
        out_shape=jax.ShapeDtypeStruct((T, POOL_W), BF16),
        scratch_shapes=[pltpu.VMEM((LAT_LEN + 2 * POOL_HALO, POOL_G), F32)],
        compiler_params=_cparams(1),
        name="pool",
    )(proj, pool_w, pool_scale.reshape(1, POOL_W))


def _rope_tables():
    pos = jnp.arange(LAT_LEN)
    row = (pos // GRID_W).astype(F32)
    col = (pos % GRID_W).astype(F32)
    quarter = QK_ROPE // 4
    inv = ROPE_BASE ** (-jnp.arange(quarter, dtype=F32) / quarter)
    lane = jnp.arange(LANES)
    axis = (lane % QK_ROPE) // (QK_ROPE // 2)
    freq = inv[lane % quarter]
    p = jnp.where(axis[None, :] == 0, row[:, None], col[:, None])
    ang = p * freq[None, :]
    sign = jnp.where((lane % (QK_ROPE // 2)) < quarter, -1.0, 1.0).astype(F32)
    return jnp.cos(ang), jnp.sin(ang) * sign[None, :]


def _rope(x, cos, sin_signed):
    quarter = QK_ROPE // 4
    lane = lax.broadcasted_iota(jnp.int32, x.shape, 1)
    first_half = (lane % (QK_ROPE // 2)) < quarter
    partner = jnp.where(first_half, pltpu.roll(x, LANES - quarter, 1), pltpu.roll(x, quarter, 1))
    return x * cos + partner * sin_signed


def _diff_core(q, k, v, lam, g, lam_init):
    lane = lax.broadcasted_iota(jnp.int32, q.shape, 1)
    low = lane < DIFF_DH
    q1 = jnp.where(low, q, 0.0).astype(BF16)
    q2 = jnp.where(low, 0.0, q).astype(BF16)

    def probs(qm):
        s = lax.dot_general(qm, k, _NT, preferred_element_type=F32)
        e = jnp.exp(s - jnp.max(s, axis=-1, keepdims=True))
        return e * (1.0 / jnp.sum(e, axis=-1, keepdims=True))

    w = (probs(q1) - lam * probs(q2)).astype(BF16)
    o = jnp.dot(w, v, preferred_element_type=F32)
    return _rms(o, g) * (1.0 - lam_init)


def _diff_attn_kernel(lp_ref, q_ref, k_ref, v_ref, ck_ref, cv_ref, cos_ref, sin_ref, g_ref, o_ref,
                      *, lam_init):
    unit = pl.program_id(0)
    lp = lp_ref[...]
    lam = (jnp.exp(jnp.sum(lp[0:1] * lp[1:2], axis=-1, keepdims=True))
           - jnp.exp(jnp.sum(lp[2:3] * lp[3:4], axis=-1, keepdims=True)) + lam_init)
    g = g_ref[...]

    @pl.when(unit < N_PROMPT_UNITS)
    def _():
        for s in range(SEQ_PER_PROMPT_UNIT):
            rows = slice(s * PROMPT_LEN, (s + 1) * PROMPT_LEN)
            q = q_ref[rows, :] * DIFF_SCALE
            k = k_ref[rows, :].astype(BF16)
            v = v_ref[rows, :].astype(BF16)
            o_ref[rows, :] = _diff_core(q, k, v, lam, g, lam_init).astype(o_ref.dtype)

    @pl.when(unit >= N_PROMPT_UNITS)
    def _():
        cos, sin = cos_ref[...], sin_ref[...]
        k = jnp.concatenate([ck_ref[...].astype(BF16),
                             _rope(k_ref[...], cos, sin).astype(BF16)], axis=0)
        v = jnp.concatenate([cv_ref[...].astype(BF16), v_ref[...].astype(BF16)], axis=0)
        for b in range(LAT_LEN // ATT_QB):
            rows = slice(b * ATT_QB, (b + 1) * ATT_QB)
            q = _rope(q_ref[rows, :], cos[rows], sin[rows]) * DIFF_SCALE
            o_ref[rows, :] = _diff_core(q, k, v, lam, g, lam_init).astype(o_ref.dtype)


def _diff_attn(proj, cache_k, cache_v, lam_params, subln_g, cos, sin, lam_init):
    hd = 2 * DIFF_DH
    q0, k0, v0 = POOL_W // hd, (POOL_W + DIFF_W) // hd, (POOL_W + 2 * DIFF_W) // hd

    def ctx_map(u, h):
        return (jnp.maximum(u - N_PROMPT_UNITS, 0), h)

    return pl.pallas_call(
        functools.partial(_diff_attn_kernel, lam_init=lam_init),
        grid=(N_UNITS, DIFF_HEADS),
        in_specs=[
            pl.BlockSpec((4, DIFF_DH), lambda u, h: (0, 0)),
            pl.BlockSpec((UNIT, hd), lambda u, h: (u, q0 + h)),
            pl.BlockSpec((UNIT, hd), lambda u, h: (u, k0 + h)),
            pl.BlockSpec((UNIT, hd), lambda u, h: (u, v0 + h)),
            pl.BlockSpec((PAST_LEN, hd), ctx_map),
            pl.BlockSpec((PAST_LEN, hd), ctx_map),
            pl.BlockSpec((LAT_LEN, LANES), lambda u, h: (0, 0)),
            pl.BlockSpec((LAT_LEN, LANES), lambda u, h: (0, 0)),
            pl.BlockSpec((1, hd), lambda u, h: (0, 0)),
        ],
        out_specs=pl.BlockSpec((UNIT, hd), lambda u, h: (u, h)),
        out_shape=jax.ShapeDtypeStruct((T, DIFF_W), BF16),
        compiler_params=_cparams(2),
        name="diff_attn",
    )(lam_params, proj, proj, proj, cache_k, cache_v, cos, sin, subln_g.reshape(1, hd))


def _mla_core(qc, kc, v):
    s = lax.dot_general(qc, kc, _NT, preferred_element_type=F32) * MLA_SCALE
    e = jnp.exp(s - jnp.max(s, axis=-1, keepdims=True))
    p = (e * (1.0 / jnp.sum(e, axis=-1, keepdims=True))).astype(BF16)
    return jnp.dot(p, v, preferred_element_type=F32)


def _mla_attn_kernel(qn_ref, qp_ref, kn_ref, v_ref, kp_ref, ckn_ref, cv_ref, ckp_ref, cos_ref, sin_ref,
                     o_ref):
    unit = pl.program_id(0)
    head = pl.program_id(1)
    lane = lax.broadcasted_iota(jnp.int32, (ATT_QB, LANES), 1)
    mine = (lane // QK_ROPE) == (head % 2)

    @pl.when(unit < N_PROMPT_UNITS)
    def _():
        for s in range(SEQ_PER_PROMPT_UNIT):
            rows = slice(s * PROMPT_LEN, (s + 1) * PROMPT_LEN)
            qp = jnp.where(mine, qp_ref[rows, :], 0.0).astype(BF16)
            qc = jnp.concatenate([qn_ref[rows, :], qp], axis=1)
            kc = jnp.concatenate([kn_ref[rows, :], kp_ref[rows, :].astype(BF16)], axis=1)
            o_ref[rows, :] = _mla_core(qc, kc, v_ref[rows, :]).astype(o_ref.dtype)

    @pl.when(unit >= N_PROMPT_UNITS)
    def _():
        cos, sin = cos_ref[...], sin_ref[...]
        k_ctx = jnp.concatenate([ckn_ref[...], ckp_ref[...].astype(BF16)], axis=1)
        k_new = jnp.concatenate([kn_ref[...], _rope(kp_ref[...], cos, sin).astype(BF16)], axis=1)
        kc = jnp.concatenate([k_ctx, k_new], axis=0)
        v = jnp.concatenate([cv_ref[...], v_ref[...]], axis=0)
        for b in range(LAT_LEN // ATT_QB):
            rows = slice(b * ATT_QB, (b + 1) * ATT_QB)
            qp = jnp.where(mine, _rope(qp_ref[rows, :], cos[rows], sin[rows]), 0.0).astype(BF16)
            qc = jnp.concatenate([qn_ref[rows, :], qp], axis=1)
            o_ref[rows, :] = _mla_core(qc, kc, v).astype(o_ref.dtype)


def _mla_attn(q_nope, q_pe, kv, kp_dup, kv_ctx, kp_ctx_dup, cos, sin):
    def ctx_row(u):
        return jnp.maximum(u - N_PROMPT_UNITS, 0)

    return pl.pallas_call(
        _mla_attn_kernel,
        grid=(N_UNITS, MLA_HEADS),
        in_specs=[
            pl.BlockSpec((UNIT, QK_NOPE), lambda u, h: (u, h)),
            pl.BlockSpec((UNIT, LANES), lambda u, h: (u, h // 2)),
            pl.BlockSpec((UNIT, QK_NOPE), lambda u, h: (u, 2 * h)),
            pl.BlockSpec((UNIT, V_DIM), lambda u, h: (u, 2 * h + 1)),
            pl.BlockSpec((UNIT, LANES), lambda u, h: (u, 0)),
            pl.BlockSpec((PAST_LEN, QK_NOPE), lambda u, h: (ctx_row(u), 2 * h)),
            pl.BlockSpec((PAST_LEN, V_DIM), lambda u, h: (ctx_row(u), 2 * h + 1)),
            pl.BlockSpec((PAST_LEN, LANES), lambda u, h: (ctx_row(u), 0)),
            pl.BlockSpec((LAT_LEN, LANES), lambda u, h: (0, 0)),
            pl.BlockSpec((LAT_LEN, LANES), lambda u, h: (0, 0)),
        ],
        out_specs=pl.BlockSpec((UNIT, V_DIM), lambda u, h: (u, h)),
        out_shape=jax.ShapeDtypeStruct((T, MLA_HEADS * V_DIM), BF16),
        compiler_params=_cparams(2),
        name="mla_attn",
    )(q_nope, q_pe, kv, kv, kp_dup, kv_ctx, kv_ctx, kp_ctx_dup, cos, sin)


def _route_kernel(x_ref, g_ref, sh_ref, sc_ref, rw_ref, rb_ref,
                  h_ref, eidx_ref, rank_ref, wts_ref, cnt_ref, carry_ref):
    tm = ROUTE_TM

    @pl.when(pl.program_id(0) == 0)
    def _():
        carry_ref[...] = jnp.zeros_like(carry_ref)

    h = _rms(x_ref[...], g_ref[...]) * (1.0 + sc_ref[...]) + sh_ref[...]
    h_ref[...] = h
    logits = _dot_3pass(rw_ref[...], h, _NT)
    scores = _sigmoid(logits)
    choice = scores + rb_ref[...]
    neg = -jnp.inf

    def take_max(vals, iota, n):
        m = jnp.max(vals, axis=0, keepdims=True)
        idx = jnp.min(jnp.where(vals == m, iota, n), axis=0, keepdims=True)
        return iota == idx, m, idx

    iota8 = lax.broadcasted_iota(jnp.int32, (8, tm), 0)

    def stack_rows(rows, dtype):
        out = jnp.zeros((8, tm), dtype)
        for k, r in enumerate(rows):
            out = jnp.where(iota8 == k, r.astype(dtype), out)
        return out

    gscore = []
    for gi in range(N_EGROUPS):
        grp = choice[gi * EGROUP:(gi + 1) * EGROUP]
        oh, m1, _ = take_max(grp, iota8, EGROUP)
        m2 = jnp.max(jnp.where(oh, neg, grp), axis=0, keepdims=True)
        gscore.append(m1 + m2)
    cur = stack_rows(gscore, F32)
    gsel = jnp.zeros((N_EGROUPS, tm), F32)
    for _ in range(TOPK_GROUPS):
        oh, _, _ = take_max(cur, iota8, N_EGROUPS)
        gsel = jnp.where(oh, 1.0, gsel)
        cur = jnp.where(oh, neg, cur)
    emask = jnp.concatenate(
        [jnp.broadcast_to(gsel[gi:gi + 1], (EGROUP, tm)) for gi in range(N_EGROUPS)], axis=0)
    masked = jnp.where(emask > 0.0, choice, neg)

    iota_e = lax.broadcasted_iota(jnp.int32, (N_EXPERTS, tm), 0)
    onehots, idxs, wsel = [], [], []
    sel = jnp.zeros((N_EXPERTS, tm), F32)
    for _ in range(TOP_K):
        oh, _, idx = take_max(masked, iota_e, N_EXPERTS)
        onehots.append(oh)
        idxs.append(idx)
        wsel.append(jnp.sum(jnp.where(oh, scores, 0.0), axis=0, keepdims=True))
        masked = jnp.where(oh, neg, masked)
        sel = jnp.where(oh, 1.0, sel)
    wsum = wsel[0]
    for w in wsel[1:]:
        wsum = wsum + w

    before = jnp.where(lax.broadcasted_iota(jnp.int32, (tm, tm), 0)
                       < lax.broadcasted_iota(jnp.int32, (tm, tm), 1), 1.0, 0.0).astype(BF16)
    rank_all = carry_ref[...] + jnp.dot(sel.astype(BF16), before, preferred_element_type=F32)
    carry_ref[...] = carry_ref[...] + jnp.sum(sel, axis=1, keepdims=True)
    cnt_ref[...] = carry_ref[...]

    ranks = [jnp.sum(jnp.where(oh, rank_all, 0.0), axis=0, keepdims=True) for oh in onehots]
    eidx_ref[...] = stack_rows(idxs, jnp.int32)
    rank_ref[...] = stack_rows(ranks, jnp.int32)
    wts_ref[...] = stack_rows([w / wsum * ROUTED_SCALE for w in wsel], F32)


def _route(x, g, mods, layer, router_w, router_bias):
    tm = ROUTE_TM
    return pl.pallas_call(
        _route_kernel,
        grid=(T // tm,),
        in_specs=[
            pl.BlockSpec((tm, D), lambda i: (i, 0)),
            pl.BlockSpec((1, D), lambda i: (0, 0)),
            _mod_spec(layer, 3, tm, 1, 0),
            _mod_spec(layer, 4, tm, 1, 0),
            pl.BlockSpec((N_EXPERTS, D), lambda i: (0, 0)),
            pl.BlockSpec((N_EXPERTS, 1), lambda i: (0, 0)),
        ],
        out_specs=[
            pl.BlockSpec((tm, D), lambda i: (i, 0)),
            pl.BlockSpec((8, tm), lambda i: (0, i)),
            pl.BlockSpec((8, tm), lambda i: (0, i)),
            pl.BlockSpec((8, tm), lambda i: (0, i)),
            pl.BlockSpec((N_EXPERTS, 1), lambda i: (0, 0)),
        ],
        out_shape=[
            jax.ShapeDtypeStruct((T, D), F32),
            jax.ShapeDtypeStruct((8, T), jnp.int32),
            jax.ShapeDtypeStruct((8, T), jnp.int32),
            jax.ShapeDtypeStruct((8, T), F32),
            jax.ShapeDtypeStruct((N_EXPERTS, 1), F32),
        ],
        scratch_shapes=[pltpu.VMEM((N_EXPERTS, 1), F32)],
        compiler_params=_cparams(1),
        name="route",
    )(x, g.reshape(1, D), mods, mods, router_w.T, router_bias.reshape(N_EXPERTS, 1))


def _row_copy(src, src_row, dst, dst_row, sem):
    return pltpu.make_async_copy(src.at[pl.ds(src_row, 1), :], dst.at[pl.ds(dst_row, 1), :], sem)


def _tile_positions(pos, tm):
    return pos.reshape(TOP_K, T // tm, tm).transpose(1, 0, 2).reshape(T // tm, 1, TOP_K * tm)


def _dispatch_kernel(pos_ref, h_ref, xs_ref, sem):
    tm = ROUTE_TM

    def issue(j, carry):
        for k in range(TOP_K):
            _row_copy(h_ref, j, xs_ref, pos_ref[0, 0, k * tm + j], sem).start(priority=k % 2)
        return carry

    def drain(j, carry):
        for _ in range(TOP_K):
            _row_copy(h_ref, 0, xs_ref, 0, sem).wait()
        return carry

    lax.fori_loop(0, tm, issue, 0, unroll=2)
    lax.fori_loop(0, tm, drain, 0, unroll=2)


def _dispatch(pos, h):
    tm = ROUTE_TM
    return pl.pallas_call(
        _dispatch_kernel,
        grid=(T // tm,),
        in_specs=[
            pl.BlockSpec((1, 1, TOP_K * tm), lambda i: (i, 0, 0), memory_space=pltpu.SMEM),
            pl.BlockSpec((tm, D), lambda i: (i, 0)),
        ],
        out_specs=pl.BlockSpec(memory_space=pl.ANY),
        out_shape=jax.ShapeDtypeStruct((N_PAIRS, D), F32),
        scratch_shapes=[pltpu.SemaphoreType.DMA(())],
        compiler_params=_cparams(1),
        name="dispatch",
    )(_tile_positions(pos, tm), h)


def _combine_kernel(pos_ref, pos_next_ref, x_ref, sh_ref, w_ref, gate_ref, fg_ref, ys_ref, *rest, final):
    out_refs, (buf_ref, sem) = rest[:-2], rest[-2:]
    tm = COMBINE_TM
    i = pl.program_id(0)
    n = pl.num_programs(0)
    slot = i % 2

    def start_gathers(p_ref, dst_slot):
        def issue(j, carry):
            for k in range(TOP_K):
                _row_copy(ys_ref, p_ref[0, 0, k * tm + j], buf_ref.at[dst_slot, k], j,
                          sem.at[dst_slot]).start(priority=k % 2)
            return carry
        lax.fori_loop(0, tm, issue, 0, unroll=2)

    @pl.when(i == 0)
    def _():
        start_gathers(pos_ref, 0)

    @pl.when(i + 1 < n)
    def _():
        start_gathers(pos_next_ref, 1 - slot)

    def drain(j, carry):
        for k in range(TOP_K):
            _row_copy(ys_ref, 0, buf_ref.at[slot, k], 0, sem.at[slot]).wait()
        return carry

    lax.fori_loop(0, tm, drain, 0, unroll=2)
    w = w_ref[...]
    acc = sh_ref[...]
    for k in range(TOP_K):
        acc = acc + w[:, k:k + 1] * buf_ref[slot, k]
    xn = x_ref[...] + gate_ref[...] * acc
    if not final:
        out_refs[0][...] = xn
    else:
        xn = _rms(xn, fg_ref[...])

        @pl.when(i < T_PROMPT // tm)
        def _():
            out_refs[0][...] = xn

        @pl.when(i >= T_PROMPT // tm)
        def _():
            out_refs[1][...] = xn


def _combine(pos, x, shared, wts_t, mods, layer, final_g, ys, final):
    tm = COMBINE_TM
    n_p = T_PROMPT // tm
    if final:
        out_specs = [pl.BlockSpec((tm, D), lambda i: (jnp.minimum(i, n_p - 1), 0)),
                     pl.BlockSpec((tm, D), lambda i: (jnp.maximum(i - n_p, 0), 0))]
        out_shape = [jax.ShapeDtypeStruct((T_PROMPT, D), F32), jax.ShapeDtypeStruct((T_LAT, D), F32)]
    else:
        out_specs = [pl.BlockSpec((tm, D), lambda i: (i, 0))]
        out_shape = [jax.ShapeDtypeStruct((T, D), F32)]
    tiles = _tile_positions(pos, tm)
    return pl.pallas_call(
        functools.partial(_combine_kernel, final=final),
        grid=(T // tm,),
        in_specs=[
            pl.BlockSpec((1, 1, TOP_K * tm), lambda i: (i, 0, 0), memory_space=pltpu.SMEM),
            pl.BlockSpec((1, 1, TOP_K * tm), lambda i: (jnp.minimum(i + 1, T // tm - 1), 0, 0),
                         memory_space=pltpu.SMEM),
            pl.BlockSpec((tm, D), lambda i: (i, 0)),
            pl.BlockSpec((tm, D), lambda i: (i, 0)),
            pl.BlockSpec((tm, 8), lambda i: (i, 0)),
            pl.BlockSpec((None, None, 1, D), lambda i: (layer, _group_of_tile(i, tm), 0, 5)),
            pl.BlockSpec((1, D), lambda i: (0, 0)),
            pl.BlockSpec(memory_space=pl.ANY),
        ],
        out_specs=out_specs,
        out_shape=out_shape,
        scratch_shapes=[pltpu.VMEM((2, TOP_K, tm, D), F32), pltpu.SemaphoreType.DMA((2,))],
        compiler_params=_cparams(1),
        name="combine",
    )(tiles, tiles, x, shared, wts_t, mods, final_g.reshape(1, D), ys)


def _swiglu(x, wgu_scr, wd_scr):
    gu = jnp.dot(x.astype(BF16), wgu_scr[...], preferred_element_type=F32)
    gate, up = gu[:, :D_EXPERT], gu[:, D_EXPERT:]
    act = (gate * _sigmoid(gate) * up).astype(BF16)
    return jnp.dot(act, wd_scr[...], preferred_element_type=F32)


def _load_expert_weights(wg_ref, wu_ref, wd_ref, wgu_scr, wd_scr):
    wgu_scr[:, :D_EXPERT] = wg_ref[...].astype(BF16)
    wgu_scr[:, D_EXPERT:] = wu_ref[...].astype(BF16)
    wd_scr[...] = wd_ref[...].astype(BF16)


def _shared_kernel(h_ref, wg_ref, wu_ref, wd_ref, o_ref, wgu_scr, wd_scr):
    @pl.when(pl.program_id(0) == 0)
    def _():
        _load_expert_weights(wg_ref, wu_ref, wd_ref, wgu_scr, wd_scr)

    o_ref[...] = _swiglu(h_ref[...], wgu_scr, wd_scr)


def _shared_expert(h, layer, w_gate, w_up, w_down):
    tm = 512
    return pl.pallas_call(
        _shared_kernel,
        grid=(T // tm,),
        in_specs=[
            pl.BlockSpec((tm, D), lambda i: (i, 0)),
            pl.BlockSpec((None, D, D_EXPERT), lambda i: (layer, 0, 0)),
            pl.BlockSpec((None, D, D_EXPERT), lambda i: (layer, 0, 0)),
            pl.BlockSpec((None, D_EXPERT, D), lambda i: (layer, 0, 0)),
        ],
        out_specs=pl.BlockSpec((tm, D), lambda i: (i, 0)),
        out_shape=jax.ShapeDtypeStruct((T, D), F32),
        scratch_shapes=[pltpu.VMEM((D, 2 * D_EXPERT), BF16), pltpu.VMEM((D_EXPERT, D), BF16)],
        compiler_params=_cparams(1),
        name="shared_expert",
    )(h, w_gate, w_up, w_down)


def _gmm_kernel(e_ref, t_ref, lo_ref, hi_ref, run_ref, next_e_ref, xs_ref, wg_hbm, wu_hbm, wd_hbm, ys_ref,
                wg_buf, wu_buf, wd_buf, wsem, wgu_scr, wd_scr, *, layer):
    v = pl.program_id(0)
    prev = jnp.maximum(v - 1, 0)
    new_expert = (v == 0) | (e_ref[v] != e_ref[prev])
    new_tile = (v == 0) | (t_ref[v] != t_ref[prev])
    lo, hi = lo_ref[v], hi_ref[v]
    slot = run_ref[v] % 2

    def weight_copies(expert, dst_slot):
        return [pltpu.make_async_copy(src.at[layer, expert], dst.at[dst_slot], wsem.at[dst_slot])
                for src, dst in ((wg_hbm, wg_buf), (wu_hbm, wu_buf), (wd_hbm, wd_buf))]

    @pl.when(v == 0)
    def _():
        for c in weight_copies(e_ref[0], 0):
            c.start()

    @pl.when(new_expert)
    def _():
        for c in weight_copies(e_ref[v], slot):
            c.wait()

        @pl.when(next_e_ref[v] >= 0)
        def _():
            for c in weight_copies(next_e_ref[v], 1 - slot):
                c.start()

        _load_expert_weights(wg_buf.at[slot], wu_buf.at[slot], wd_buf.at[slot], wgu_scr, wd_scr)

    @pl.when(hi > lo)
    def _():
        y = _swiglu(xs_ref[...], wgu_scr, wd_scr)
        row = lax.broadcasted_iota(jnp.int32, (GMM_TM, 1), 0)
        mine = (row >= lo) & (row < hi)

        @pl.when(new_tile)
        def _():
            ys_ref[...] = jnp.where(mine, y, 0.0)

        @pl.when(jnp.logical_not(new_tile))
        def _():
            ys_ref[...] = jnp.where(mine, y, ys_ref[...])


def _gmm(sched, xs, layer, w_gate, w_up, w_down):
    return pl.pallas_call(
        functools.partial(_gmm_kernel, layer=layer),
        grid_spec=pltpu.PrefetchScalarGridSpec(
            num_scalar_prefetch=6,
            grid=(N_VISITS,),
            in_specs=[
                pl.BlockSpec((GMM_TM, D), lambda v, e, t, *_: (t[v], 0)),
                pl.BlockSpec(memory_space=pl.ANY),
                pl.BlockSpec(memory_space=pl.ANY),
                pl.BlockSpec(memory_space=pl.ANY),
            ],
            out_specs=pl.BlockSpec((GMM_TM, D), lambda v, e, t, *_: (t[v], 0)),
            scratch_shapes=[
                pltpu.VMEM((2, D, D_EXPERT), F32),
                pltpu.VMEM((2, D, D_EXPERT), F32),
                pltpu.VMEM((2, D_EXPERT, D), F32),
                pltpu.SemaphoreType.DMA((2,)),
                pltpu.VMEM((D, 2 * D_EXPERT), BF16),
                pltpu.VMEM((D_EXPERT, D), BF16),
            ],
        ),
        out_shape=jax.ShapeDtypeStruct((N_PAIRS, D), F32),
        compiler_params=_cparams(1),
        name="experts",
    )(*sched, xs, w_gate, w_up, w_down)


def _visit_schedule(counts):
    tm = GMM_TM
    ends = jnp.cumsum(counts)
    starts = ends - counts
    first_tile = starts // tm
    last_tile = jnp.maximum(ends - 1, 0) // tm
    n_vis = jnp.where(counts > 0, last_tile - first_tile + 1, 0)
    vis_end = jnp.cumsum(n_vis)
    vis_start = vis_end - n_vis
    total = vis_end[-1]
    v = jnp.minimum(jnp.arange(N_VISITS, dtype=jnp.int32), total - 1)
    e = jnp.sum((vis_end[None, :] <= v[:, None]).astype(jnp.int32), axis=1)
    tile = first_tile[e] + (v - vis_start[e])
    lo = jnp.clip(starts[e] - tile * tm, 0, tm)
    hi = jnp.clip(ends[e] - tile * tm, 0, tm)
    hi = jnp.where(jnp.arange(N_VISITS) < total, hi, lo)
    run = jnp.cumsum((e != jnp.concatenate([e[:1] - 1, e[:-1]])).astype(jnp.int32)) - 1
    later = jnp.where(e[None, :] > e[:, None], e[None, :], N_EXPERTS)
    next_e = jnp.min(later, axis=1)
    next_e = jnp.where(next_e < N_EXPERTS, next_e, -1)
    sched = (e, tile, lo, hi, run, next_e)
    return starts.astype(jnp.int32), tuple(a.astype(jnp.int32) for a in sched)


def _moe(x, mods, layer, norm_g, router_w, router_bias, w_gate, w_up, w_down,
         ws_gate, ws_up, ws_down, final_g, final):
    h, eidx, rank, wts, counts = _route(x, norm_g, mods, layer, router_w, router_bias)
    starts, sched = _visit_schedule(counts.reshape(N_EXPERTS).astype(jnp.int32))
    eidx, rank = eidx[:TOP_K], rank[:TOP_K]
    expert_ids = jnp.arange(N_EXPERTS, dtype=jnp.int32)[:, None, None]
    pos = rank + jnp.sum(jnp.where(eidx[None] == expert_ids, starts[:, None, None], 0), axis=0)
    xs = _dispatch(pos, h)
    ys = _gmm(sched, xs, layer, w_gate, w_up, w_down)
    shared = _shared_expert(h, layer, ws_gate, ws_up, ws_down)
    return _combine(pos, x, shared, wts.T, mods, layer, final_g, ys, final)


def _pool_diff_layer(x, mods, layer, j, norm_g, cache_k, cache_v, cos, sin,
                     diff_w_in, pool_w, pool_scale, lq1, lk1, lq2, lk2, subln_g, w_out):
    lam_init = 0.8 - 0.6 * math.exp(-0.3 * layer)
    h = _norm_mod(x, norm_g, mods, layer, 0, 1)
    proj = _matmul([h], diff_w_in[j], [0], POOL_W + 3 * DIFF_W, F32, tn=1024, name="diff_in_proj")
    y_pool = _pool(proj, pool_w[j], pool_scale[j])
    lam_params = jnp.stack([lq1[j], lk1[j], lq2[j], lk2[j]])
    ck = cache_k[:, j].reshape(N_LAT_SEQ * PAST_LEN, DIFF_W)
    cv = cache_v[:, j].reshape(N_LAT_SEQ * PAST_LEN, DIFF_W)
    o = _diff_attn(proj, ck, cv, lam_params, subln_g[j], cos, sin, lam_init)
    x = _matmul([y_pool, o], w_out[j], [0, 1], D, F32, tn=1024, epilogue="residual",
                residual=x, mods=mods, layer=layer, gate_chunk=2, name="diff_out_proj")
    k_new = proj[:T_PROMPT, POOL_W + DIFF_W:POOL_W + 2 * DIFF_W]
    v_new = proj[:T_PROMPT, POOL_W + 2 * DIFF_W:]
    shape = (N_PROMPT_SEQ, PROMPT_LEN, DIFF_HEADS, 2 * DIFF_DH)
    return x, k_new.reshape(shape), v_new.reshape(shape)


def _mla_layer(x, mods, layer, j, norm_g, cache_ckv, cache_kpe, cos, sin,
               w_dq, q_norm_g, w_uq, w_dkv, kv_norm_g, w_ukv, w_o):
    h = _norm_mod(x, norm_g, mods, layer, 0, 1)
    cq = _matmul([h], w_dq[j], [0], Q_LORA, BF16, epilogue="rms", gain=q_norm_g[j], name="mla_dq")
    ckv = _matmul([h], w_dkv[j], [0], KV_LORA, F32, epilogue="rms", gain=kv_norm_g[j], name="mla_dkv")
    w_kpe = w_dkv[j][:, KV_LORA:]
    kp_dup = _matmul([h], jnp.concatenate([w_kpe, w_kpe], axis=1), [0], LANES, F32, name="mla_kpe")
    w_uq3 = w_uq[j].reshape(Q_LORA, MLA_HEADS, QK_NOPE + QK_ROPE)
    w_uq_nope = w_uq3[:, :, :QK_NOPE].reshape(Q_LORA, MLA_HEADS * QK_NOPE)
    w_uq_pe = w_uq3[:, :, QK_NOPE:].reshape(Q_LORA, MLA_HEADS * QK_ROPE)
    q_nope = _matmul([cq], w_uq_nope, [0], MLA_HEADS * QK_NOPE, BF16, tn=1024, name="mla_uq_nope")
    q_pe = _matmul([cq], w_uq_pe, [0], MLA_HEADS * QK_ROPE, F32, tn=1024, name="mla_uq_pe")
    n_kv = MLA_HEADS * (QK_NOPE + V_DIM)
    kv = _matmul([ckv], w_ukv[j], [0], n_kv, BF16, tn=1024, name="mla_ukv")
    ckv_ctx = cache_ckv[:, j].reshape(N_LAT_SEQ * PAST_LEN, KV_LORA)
    kv_ctx = _matmul([ckv_ctx], w_ukv[j], [0], n_kv, BF16, tn=1024, name="mla_ukv_ctx")
    kpe_ctx = cache_kpe[:, j].reshape(N_LAT_SEQ * PAST_LEN, QK_ROPE)
    kp_ctx_dup = jnp.concatenate([kpe_ctx, kpe_ctx], axis=1)
    o = _mla_attn(q_nope, q_pe, kv, kp_dup, kv_ctx, kp_ctx_dup, cos, sin)
    x = _matmul([o], w_o[j], [0], D, F32, tn=1024, epilogue="residual",
                residual=x, mods=mods, layer=layer, gate_chunk=2, name="mla_out_proj")
    new_ckv = ckv[:T_PROMPT].reshape(N_PROMPT_SEQ, PROMPT_LEN, KV_LORA)
    new_kpe = kp_dup[:T_PROMPT, :QK_ROPE].reshape(N_PROMPT_SEQ, PROMPT_LEN, QK_ROPE)
    return x, new_ckv, new_kpe


def kernel(x_prompt, x_sample, cache_diff_k, cache_diff_v, cache_mla_ckv, cache_mla_kpe, c, c_ctx,
           ada_w, ada_b, norm_mix_g, norm_ffn_g, final_norm_g,
           diff_w_in, pool_w, pool_scale, diff_lambda_q1, diff_lambda_k1, diff_lambda_q2, diff_lambda_k2,
           diff_subln_g, even_w_out,
           mla_w_dq, mla_q_norm_g, mla_w_uq, mla_w_dkv, mla_kv_norm_g, mla_w_ukv, mla_w_o,
           router_w, router_bias, expert_w_gate, expert_w_up, expert_w_down,
           shared_w_gate, shared_w_up, shared_w_down):
    depth = ada_w.shape[0]
    x = (x_prompt.reshape(T_PROMPT, D), x_sample.reshape(T_LAT, D))
    cond = jnp.concatenate(
        [c_ctx[None, :], c, jnp.zeros((N_GROUPS_PAD - 1 - N_LAT_SEQ, D), F32)], axis=0)
    mods = _ada_params(cond, ada_w, ada_b)
    cos, sin = _rope_tables()

    new_dk, new_dv, new_ckv, new_kpe = [], [], [], []
    for i in range(depth):
        j = i // 2
        if i % 2 == 0:
            x, k_new, v_new = _pool_diff_layer(
                x, mods, i, j, norm_mix_g[i], cache_diff_k, cache_diff_v, cos, sin,
                diff_w_in, pool_w, pool_scale, diff_lambda_q1, diff_lambda_k1, diff_lambda_q2,
                diff_lambda_k2, diff_subln_g, even_w_out)
            new_dk.append(k_new)
            new_dv.append(v_new)
        else:
            x, ckv, kpe = _mla_layer(
                x, mods, i, j, norm_mix_g[i], cache_mla_ckv, cache_mla_kpe, cos, sin,
                mla_w_dq, mla_q_norm_g, mla_w_uq, mla_w_dkv, mla_kv_norm_g, mla_w_ukv, mla_w_o)
            new_ckv.append(ckv)
            new_kpe.append(kpe)
        outs = _moe(x, mods, i, norm_ffn_g[i], router_w[i], router_bias[i],
                    expert_w_gate, expert_w_up, expert_w_down,
                    shared_w_gate, shared_w_up, shared_w_down,
                    final_norm_g, final=(i == depth - 1))
        x = outs[0]
    y_prompt = outs[0].reshape(N_PROMPT_SEQ, PROMPT_LEN, D)
    y_sample = outs[1].reshape(N_LAT_SEQ, LAT_LEN, D)
    return (y_prompt, y_sample, jnp.stack(new_dk, axis=1), jnp.stack(new_dv, axis=1),
            jnp.stack(new_ckv, axis=1), jnp.stack(new_kpe, axis=1))
```

```python
import functools
import math

import jax
import jax.numpy as jnp
from jax import lax
from jax.experimental import pallas as pl
from jax.experimental.pallas import tpu as pltpu

F32 = jnp.float32
BF16 = jnp.bfloat16

D = 2048
N_PROMPT_SEQ = 16
PROMPT_LEN = 256
N_LAT_SEQ = 4
LAT_LEN = 1024
PAST_LEN = 256
T_PROMPT = N_PROMPT_SEQ * PROMPT_LEN
T_LAT = N_LAT_SEQ * LAT_LEN
T = T_PROMPT + T_LAT
UNIT = 1024
N_UNITS = T // UNIT
N_PROMPT_UNITS = T_PROMPT // UNIT
SEQ_PER_PROMPT_UNIT = UNIT // PROMPT_LEN
N_GROUPS_PAD = 8

GRID_W = 64
ROPE_BASE = 10000.0
EPS = 1e-6
POOL_W = 1024
POOL_WINDOWS = (2, 4, 8, 16)
POOL_G = 256
POOL_HALO = 8
DIFF_W = 1024
DIFF_DH = 64
DIFF_HEADS = 8
DIFF_SCALE = DIFF_DH ** -0.5
MLA_HEADS = 16
Q_LORA = 512
KV_LORA = 512
QK_NOPE = 128
QK_ROPE = 64
V_DIM = 128
MLA_SCALE = (QK_NOPE + QK_ROPE) ** -0.5
N_EXPERTS = 64
TOP_K = 6
N_EGROUPS = 8
EGROUP = N_EXPERTS // N_EGROUPS
TOPK_GROUPS = 4
D_EXPERT = 512
ROUTED_SCALE = 2.5

LANES = 128
ATT_QB = 256
ROUTE_TM = 256
COMBINE_TM = 128
GMM_TM = 256
ROW_UNROLL = 4
N_PAIRS = T * TOP_K
N_ROW_TILES = N_PAIRS // GMM_TM
N_VISITS = N_ROW_TILES + N_EXPERTS
VMEM_LIMIT = 56 * 1024 * 1024

_NT = (((1,), (1,)), ((), ()))
_NN = (((1,), (0,)), ((), ()))


def _cparams(n_axes, vmem=VMEM_LIMIT):
    return pltpu.CompilerParams(dimension_semantics=("arbitrary",) * n_axes, vmem_limit_bytes=vmem)


def _group_of_tile(i, tm):
    n_p = T_PROMPT // tm
    per_seq = LAT_LEN // tm
    return jnp.where(i < n_p, 0, 1 + (i - n_p) // per_seq)


def _sigmoid(x):
    return 1.0 / (1.0 + jnp.exp(-x))


def _rms(x, g):
    return x * lax.rsqrt(jnp.mean(x * x, axis=-1, keepdims=True) + EPS) * g


def _dot_3pass(a, b, dims):
    a_hi = a.astype(BF16)
    a_lo = (a - a_hi.astype(F32)).astype(BF16)
    b_hi = b.astype(BF16)
    b_lo = (b - b_hi.astype(F32)).astype(BF16)

    def dot(p, q):
        return lax.dot_general(p, q, dims, preferred_element_type=F32)

    return dot(a_hi, b_hi) + (dot(a_hi, b_lo) + dot(a_lo, b_hi))


def _ada_kernel(c_ref, w_ref, b_ref, o_ref):
    c = c_ref[...]
    s = c * _sigmoid(c)
    o_ref[...] = _dot_3pass(s, w_ref[...], _NN) + b_ref[...]


def _ada_params(cond, ada_w, ada_b):
    depth, _, n = ada_w.shape
    tn = 1024
    out = pl.pallas_call(
        _ada_kernel,
        grid=(depth, n // tn),
        in_specs=[
            pl.BlockSpec((N_GROUPS_PAD, D), lambda l, j: (0, 0)),
            pl.BlockSpec((None, D, tn), lambda l, j: (l, 0, j)),
            pl.BlockSpec((None, 1, tn), lambda l, j: (l, 0, j)),
        ],
        out_specs=pl.BlockSpec((None, N_GROUPS_PAD, tn), lambda l, j: (l, 0, j)),
        out_shape=jax.ShapeDtypeStruct((depth, N_GROUPS_PAD, n), F32),
        compiler_params=_cparams(2),
        name="ada_params",
    )(cond, ada_w, ada_b.reshape(depth, 1, n))
    return out.reshape(depth, N_GROUPS_PAD, 1, n)


def _mod_spec(layer, chunk, tm, grid_rank, row_axis):
    def index_map(*ids):
        return (layer, _group_of_tile(ids[row_axis], tm), 0, chunk)
    del grid_rank
    return pl.BlockSpec((None, None, 1, D), index_map)


def _row_source(x):
    return list(x) if isinstance(x, (tuple, list)) else [x]


def _row_source_specs(parts, tm, width, col_of, row_axis):
    if len(parts) == 1:
        return [pl.BlockSpec((tm, width), lambda *ids: (ids[row_axis], col_of(*ids)))]
    n_p = T_PROMPT // tm
    return [pl.BlockSpec((tm, width), lambda *ids: (jnp.minimum(ids[row_axis], n_p - 1), col_of(*ids))),
            pl.BlockSpec((tm, width), lambda *ids: (jnp.maximum(ids[row_axis] - n_p, 0), col_of(*ids)))]


def _read_rows(refs, row_tile, tm):
    if len(refs) == 1:
        return refs[0][...]
    return jnp.where(row_tile < T_PROMPT // tm, refs[0][...], refs[1][...])


def _norm_mod_kernel(*refs, n_x, tm):
    x_refs = refs[:n_x]
    g_ref, sh_ref, sc_ref, o_ref = refs[n_x:]
    y = _rms(_read_rows(x_refs, pl.program_id(0), tm), g_ref[...])
    o_ref[...] = (y * (1.0 + sc_ref[...]) + sh_ref[...]).astype(o_ref.dtype)


def _norm_mod(x, g, mods, layer, shift_chunk, scale_chunk):
    tm = 512
    parts = _row_source(x)
    return pl.pallas_call(
        functools.partial(_norm_mod_kernel, n_x=len(parts), tm=tm),
        grid=(T // tm,),
        in_specs=_row_source_specs(parts, tm, D, lambda i: 0, 0) + [
            pl.BlockSpec((1, D), lambda i: (0, 0)),
            _mod_spec(layer, shift_chunk, tm, 1, 0),
            _mod_spec(layer, scale_chunk, tm, 1, 0),
        ],
        out_specs=pl.BlockSpec((tm, D), lambda i: (i, 0)),
        out_shape=jax.ShapeDtypeStruct((T, D), BF16),
        compiler_params=_cparams(1),
        name="norm_mod",
    )(*parts, g.reshape(1, D), mods, mods)


def _mm_kernel(*refs, n_a, epilogue, n_res, tm):
    a_refs = refs[:n_a]
    w_refs = refs[n_a:2 * n_a]
    pos = 2 * n_a
    if epilogue == "rms":
        g_ref = refs[pos]
        pos += 1
    elif epilogue == "residual":
        res_refs, gate_ref = refs[pos:pos + n_res], refs[pos + n_res]
        pos += n_res + 1
    o_ref = refs[pos]
    w_scr = refs[pos + 1:]

    @pl.when(pl.program_id(1) == 0)
    def _():
        for w_ref, scr in zip(w_refs, w_scr):
            scr[...] = w_ref[...].astype(BF16)

    acc = None
    for a_ref, scr in zip(a_refs, w_scr):
        part = jnp.dot(a_ref[...].astype(BF16), scr[...], preferred_element_type=F32)
        acc = part if acc is None else acc + part
    if epilogue == "rms":
        acc = _rms(acc, g_ref[...])
    elif epilogue == "residual":
        acc = _read_rows(res_refs, pl.program_id(1), tm) + gate_ref[...] * acc
    o_ref[...] = acc.astype(o_ref.dtype)


def _matmul(a_list, w, w_row_blocks, n_out, out_dtype, *, tm=512, tn=None, epilogue=None,
            gain=None, residual=None, mods=None, layer=None, gate_chunk=None, name="matmul"):
    m = a_list[0].shape[0]
    tn = n_out if tn is None else tn
    n_a = len(a_list)
    in_specs, args = [], []
    for a in a_list:
        k = a.shape[1]
        in_specs.append(pl.BlockSpec((tm, k), lambda j, i: (i, 0)))
        args.append(a)
    scratch = []
    for a, rb in zip(a_list, w_row_blocks):
        k = a.shape[1]
        in_specs.append(pl.BlockSpec((k, tn), lambda j, i, rb=rb: (rb, j)))
        args.append(w)
        scratch.append(pltpu.VMEM((k, tn), BF16))
    if epilogue == "rms":
        assert tn == n_out
        in_specs.append(pl.BlockSpec((1, tn), lambda j, i: (0, 0)))
        args.append(gain.reshape(1, n_out))
    elif epilogue == "residual":
        res_parts = _row_source(residual)
        in_specs += _row_source_specs(res_parts, tm, tn, lambda j, i: j, 1)
        args += res_parts
        in_specs.append(pl.BlockSpec(
            (None, None, 1, tn),
            lambda j, i: (layer, _group_of_tile(i, tm), 0, gate_chunk * (D // tn) + j)))
        args.append(mods)
    return pl.pallas_call(
        functools.partial(_mm_kernel, n_a=n_a, epilogue=epilogue, tm=tm,
                          n_res=len(_row_source(residual)) if epilogue == "residual" else 0),
        grid=(n_out // tn, m // tm),
        in_specs=in_specs,
        out_specs=pl.BlockSpec((tm, tn), lambda j, i: (i, j)),
        out_shape=jax.ShapeDtypeStruct((m, n_out), out_dtype),
        scratch_shapes=scratch,
        compiler_params=_cparams(2),
        name=name,
    )(*args)


def _pool_seq(u_ref, row0, seq_len, pw_ref, scale_ref, o_ref, pad_ref):
    t = lax.broadcasted_iota(jnp.int32, (seq_len, 1), 0)
    zeros = jnp.zeros((POOL_HALO, POOL_G), F32)
    for g, w in enumerate(POOL_WINDOWS):
        cols = slice(g * POOL_G, (g + 1) * POOL_G)
        ug = u_ref[row0:row0 + seq_len, cols]
        pad_ref[0:POOL_HALO, :] = zeros
        pad_ref[POOL_HALO:POOL_HALO + seq_len, :] = ug
        pad_ref[POOL_HALO + seq_len:2 * POOL_HALO + seq_len, :] = zeros
        total = None
        for off in range(-(w // 2), w // 2):
            part = pad_ref[POOL_HALO + off:POOL_HALO + off + seq_len, :]
            total = part if total is None else total + part
        cnt = (jnp.minimum(t + w // 2, seq_len) - jnp.maximum(t - w // 2, 0)).astype(F32)
        d = (total / cnt - ug).astype(BF16)
        y = jnp.dot(d, pw_ref[g].astype(BF16), preferred_element_type=F32)
        o_ref[row0:row0 + seq_len, cols] = (y * scale_ref[:, cols]).astype(o_ref.dtype)


def _pool_kernel(u_ref, pw_ref, scale_ref, o_ref, pad_ref):
    unit = pl.program_id(0)

    @pl.when(unit < N_PROMPT_UNITS)
    def _():
        for s in range(SEQ_PER_PROMPT_UNIT):
            _pool_seq(u_ref, s * PROMPT_LEN, PROMPT_LEN, pw_ref, scale_ref, o_ref, pad_ref)

    @pl.when(unit >= N_PROMPT_UNITS)
    def _():
        _pool_seq(u_ref, 0, LAT_LEN, pw_ref, scale_ref, o_ref, pad_ref)


def _pool(proj, pool_w, pool_scale):
    return pl.pallas_call(
        _pool_kernel,
        grid=(N_UNITS,),
        in_specs=[
            pl.BlockSpec((UNIT, POOL_W), lambda u: (u, 0)),
            pl.BlockSpec((len(POOL_WINDOWS), POOL_G, POOL_G), lambda u: (0, 0, 0)),
            pl.BlockSpec((1, POOL_W), lambda u: (0, 0)),
        ],
        out_specs=pl.BlockSpec((UNIT, POOL_W), lambda u: (u, 0)),
        out_shape=jax.ShapeDtypeStruct((T, POOL_W), BF16),
        scratch_shapes=[pltpu.VMEM((LAT_LEN + 2 * POOL_HALO, POOL_G), F32)],
        compiler_params=_cparams(1),
        name="pool",
    )(proj, pool_w, pool_scale.reshape(1, POOL_W))


def _rope_tables():
    pos = jnp.arange(LAT_LEN)
    row = (pos // GRID_W).astype(F32)
    col = (pos % GRID_W).astype(F32)
    quarter = QK_ROPE // 4
    inv = ROPE_BASE ** (-jnp.arange(quarter, dtype=F32) / quarter)
    lane = jnp.arange(LANES)
    axis = (lane % QK_ROPE) // (QK_ROPE // 2)
    freq = inv[lane % quarter]
    p = jnp.where(axis[None, :] == 0, row[:, None], col[:, None])
    ang = p * freq[None, :]
    sign = jnp.where((lane % (QK_ROPE // 2)) < quarter, -1.0, 1.0).astype(F32)
    return jnp.cos(ang), jnp.sin(ang) * sign[None, :]


def _rope(x, cos, sin_signed):
    quarter = QK_ROPE // 4
    lane = lax.broadcasted_iota(jnp.int32, x.shape, 1)
    first_half = (lane % (QK_ROPE // 2)) < quarter
    partner = jnp.where(first_half, pltpu.roll(x, LANES - quarter, 1), pltpu.roll(x, quarter, 1))
    return x * cos + partner * sin_signed


def _diff_core(q, k, v, lam, g, lam_init):
    lane = lax.broadcasted_iota(jnp.int32, q.shape, 1)
    low = lane < DIFF_DH
    q1 = jnp.where(low, q, 0.0).astype(BF16)
    q2 = jnp.where(low, 0.0, q).astype(BF16)

    def probs(qm):
        s = lax.dot_general(qm, k, _NT, preferred_element_type=F32)
        e = jnp.exp(s - jnp.max(s, axis=-1, keepdims=True))
        return e * (1.0 / jnp.sum(e, axis=-1, keepdims=True))

    w = (probs(q1) - lam * probs(q2)).astype(BF16)
    o = jnp.dot(w, v, preferred_element_type=F32)
    return _rms(o, g) * (1.0 - lam_init)


def _diff_attn_kernel(lp_ref, q_ref, k_ref, v_ref, ck_ref, cv_ref, cos_ref, sin_ref, g_ref, o_ref,
                      *, lam_init):
    unit = pl.program_id(0)
    lp = lp_ref[...]
    lam = (jnp.exp(jnp.sum(lp[0:1] * lp[1:2], axis=-1, keepdims=True))
           - jnp.exp(jnp.sum(lp[2:3] * lp[3:4], axis=-1, keepdims=True)) + lam_init)
    g = g_ref[...]

    @pl.when(unit < N_PROMPT_UNITS)
    def _():
        for s in range(SEQ_PER_PROMPT_UNIT):
            rows = slice(s * PROMPT_LEN, (s + 1) * PROMPT_LEN)
            q = q_ref[rows, :] * DIFF_SCALE
            k = k_ref[rows, :].astype(BF16)
            v = v_ref[rows, :].astype(BF16)
            o_ref[rows, :] = _diff_core(q, k, v, lam, g, lam_init).astype(o_ref.dtype)

    @pl.when(unit >= N_PROMPT_UNITS)
    def _():
        cos, sin = cos_ref[...], sin_ref[...]
        k = jnp.concatenate([ck_ref[...].astype(BF16),
                             _rope(k_ref[...], cos, sin).astype(BF16)], axis=0)
        v = jnp.concatenate([cv_ref[...].astype(BF16), v_ref[...].astype(BF16)], axis=0)
        for b in range(LAT_LEN // ATT_QB):
            rows = slice(b * ATT_QB, (b + 1) * ATT_QB)
            q = _rope(q_ref[rows, :], cos[rows], sin[rows]) * DIFF_SCALE
            o_ref[rows, :] = _diff_core(q, k, v, lam, g, lam_init).astype(o_ref.dtype)


def _diff_attn(proj, cache_k, cache_v, lam_params, subln_g, cos, sin, lam_init):
    hd = 2 * DIFF_DH
    q0, k0, v0 = POOL_W // hd, (POOL_W + DIFF_W) // hd, (POOL_W + 2 * DIFF_W) // hd

    def ctx_map(u, h):
        return (jnp.maximum(u - N_PROMPT_UNITS, 0), h)

    return pl.pallas_call(
        functools.partial(_diff_attn_kernel, lam_init=lam_init),
        grid=(N_UNITS, DIFF_HEADS),
        in_specs=[
            pl.BlockSpec((4, DIFF_DH), lambda u, h: (0, 0)),
            pl.BlockSpec((UNIT, hd), lambda u, h: (u, q0 + h)),
            pl.BlockSpec((UNIT, hd), lambda u, h: (u, k0 + h)),
            pl.BlockSpec((UNIT, hd), lambda u, h: (u, v0 + h)),
            pl.BlockSpec((PAST_LEN, hd), ctx_map),
            pl.BlockSpec((PAST_LEN, hd), ctx_map),
            pl.BlockSpec((LAT_LEN, LANES), lambda u, h: (0, 0)),
            pl.BlockSpec((LAT_LEN, LANES), lambda u, h: (0, 0)),
            pl.BlockSpec((1, hd), lambda u, h: (0, 0)),
        ],
        out_specs=pl.BlockSpec((UNIT, hd), lambda u, h: (u, h)),
        out_shape=jax.ShapeDtypeStruct((T, DIFF_W), BF16),
        compiler_params=_cparams(2),
        name="diff_attn",
    )(lam_params, proj, proj, proj, cache_k, cache_v, cos, sin, subln_g.reshape(1, hd))


def _mla_core(qc, kc, v):
    s = lax.dot_general(qc, kc, _NT, preferred_element_type=F32) * MLA_SCALE
    e = jnp.exp(s - jnp.max(s, axis=-1, keepdims=True))
    p = (e * (1.0 / jnp.sum(e, axis=-1, keepdims=True))).astype(BF16)
    return jnp.dot(p, v, preferred_element_type=F32)


def _mla_attn_kernel(qn_ref, qp_ref, kn_ref, v_ref, kp_ref, ckn_ref, cv_ref, ckp_ref, cos_ref, sin_ref,
                     o_ref):
    unit = pl.program_id(0)
    head = pl.program_id(1)
    lane = lax.broadcasted_iota(jnp.int32, (ATT_QB, LANES), 1)
    mine = (lane // QK_ROPE) == (head % 2)

    @pl.when(unit < N_PROMPT_UNITS)
    def _():
        for s in range(SEQ_PER_PROMPT_UNIT):
            rows = slice(s * PROMPT_LEN, (s + 1) * PROMPT_LEN)
            qp = jnp.where(mine, qp_ref[rows, :], 0.0).astype(BF16)
            qc = jnp.concatenate([qn_ref[rows, :], qp], axis=1)
            kc = jnp.concatenate([kn_ref[rows, :], kp_ref[rows, :].astype(BF16)], axis=1)
            o_ref[rows, :] = _mla_core(qc, kc, v_ref[rows, :]).astype(o_ref.dtype)

    @pl.when(unit >= N_PROMPT_UNITS)
    def _():
        cos, sin = cos_ref[...], sin_ref[...]
        k_ctx = jnp.concatenate([ckn_ref[...], ckp_ref[...].astype(BF16)], axis=1)
        k_new = jnp.concatenate([kn_ref[...], _rope(kp_ref[...], cos, sin).astype(BF16)], axis=1)
        kc = jnp.concatenate([k_ctx, k_new], axis=0)
        v = jnp.concatenate([cv_ref[...], v_ref[...]], axis=0)
        for b in range(LAT_LEN // ATT_QB):
            rows = slice(b * ATT_QB, (b + 1) * ATT_QB)
            qp = jnp.where(mine, _rope(qp_ref[rows, :], cos[rows], sin[rows]), 0.0).astype(BF16)
            qc = jnp.concatenate([qn_ref[rows, :], qp], axis=1)
            o_ref[rows, :] = _mla_core(qc, kc, v).astype(o_ref.dtype)


def _mla_attn(q_nope, q_pe, kv, kp_dup, kv_ctx, kp_ctx_dup, cos, sin):
    def ctx_row(u):
        return jnp.maximum(u - N_PROMPT_UNITS, 0)

    return pl.pallas_call(
        _mla_attn_kernel,
        grid=(N_UNITS, MLA_HEADS),
        in_specs=[
            pl.BlockSpec((UNIT, QK_NOPE), lambda u, h: (u, h)),
            pl.BlockSpec((UNIT, LANES), lambda u, h: (u, h // 2)),
            pl.BlockSpec((UNIT, QK_NOPE), lambda u, h: (u, 2 * h)),
            pl.BlockSpec((UNIT, V_DIM), lambda u, h: (u, 2 * h + 1)),
            pl.BlockSpec((UNIT, LANES), lambda u, h: (u, 0)),
            pl.BlockSpec((PAST_LEN, QK_NOPE), lambda u, h: (ctx_row(u), 2 * h)),
            pl.BlockSpec((PAST_LEN, V_DIM), lambda u, h: (ctx_row(u), 2 * h + 1)),
            pl.BlockSpec((PAST_LEN, LANES), lambda u, h: (ctx_row(u), 0)),
            pl.BlockSpec((LAT_LEN, LANES), lambda u, h: (0, 0)),
            pl.BlockSpec((LAT_LEN, LANES), lambda u, h: (0, 0)),
        ],
        out_specs=pl.BlockSpec((UNIT, V_DIM), lambda u, h: (u, h)),
        out_shape=jax.ShapeDtypeStruct((T, MLA_HEADS * V_DIM), BF16),
        compiler_params=_cparams(2),
        name="mla_attn",
    )(q_nope, q_pe, kv, kv, kp_dup, kv_ctx, kv_ctx, kp_ctx_dup, cos, sin)


def _route_kernel(x_ref, g_ref, sh_ref, sc_ref, rw_ref, rb_ref,
                  h_ref, eidx_ref, rank_ref, wts_ref, cnt_ref, carry_ref):
    tm = ROUTE_TM

    @pl.when(pl.program_id(0) == 0)
    def _():
        carry_ref[...] = jnp.zeros_like(carry_ref)

    h = _rms(x_ref[...], g_ref[...]) * (1.0 + sc_ref[...]) + sh_ref[...]
    h_ref[...] = h
    logits = _dot_3pass(rw_ref[...], h, _NT)
    scores = _sigmoid(logits)
    choice = scores + rb_ref[...]
    neg = -jnp.inf

    def take_max(vals, iota, n):
        m = jnp.max(vals, axis=0, keepdims=True)
        idx = jnp.min(jnp.where(vals == m, iota, n), axis=0, keepdims=True)
        return iota == idx, m, idx

    iota8 = lax.broadcasted_iota(jnp.int32, (8, tm), 0)

    def stack_rows(rows, dtype):
        out = jnp.zeros((8, tm), dtype)
        for k, r in enumerate(rows):
            out = jnp.where(iota8 == k, r.astype(dtype), out)
        return out

    gscore = []
    for gi in range(N_EGROUPS):
        grp = choice[gi * EGROUP:(gi + 1) * EGROUP]
        oh, m1, _ = take_max(grp, iota8, EGROUP)
        m2 = jnp.max(jnp.where(oh, neg, grp), axis=0, keepdims=True)
        gscore.append(m1 + m2)
    cur = stack_rows(gscore, F32)
    gsel = jnp.zeros((N_EGROUPS, tm), F32)
    for _ in range(TOPK_GROUPS):
        oh, _, _ = take_max(cur, iota8, N_EGROUPS)
        gsel = jnp.where(oh, 1.0, gsel)
        cur = jnp.where(oh, neg, cur)
    emask = jnp.concatenate(
        [jnp.broadcast_to(gsel[gi:gi + 1], (EGROUP, tm)) for gi in range(N_EGROUPS)], axis=0)
    masked = jnp.where(emask > 0.0, choice, neg)

    iota_e = lax.broadcasted_iota(jnp.int32, (N_EXPERTS, tm), 0)
    onehots, idxs, wsel = [], [], []
    sel = jnp.zeros((N_EXPERTS, tm), F32)
    for _ in range(TOP_K):
        oh, _, idx = take_max(masked, iota_e, N_EXPERTS)
        onehots.append(oh)
        idxs.append(idx)
        wsel.append(jnp.sum(jnp.where(oh, scores, 0.0), axis=0, keepdims=True))
        masked = jnp.where(oh, neg, masked)
        sel = jnp.where(oh, 1.0, sel)
    wsum = wsel[0]
    for w in wsel[1:]:
        wsum = wsum + w

    before = jnp.where(lax.broadcasted_iota(jnp.int32, (tm, tm), 0)
                       < lax.broadcasted_iota(jnp.int32, (tm, tm), 1), 1.0, 0.0).astype(BF16)
    rank_all = carry_ref[...] + jnp.dot(sel.astype(BF16), before, preferred_element_type=F32)
    carry_ref[...] = carry_ref[...] + jnp.sum(sel, axis=1, keepdims=True)
    cnt_ref[...] = carry_ref[...]

    ranks = [jnp.sum(jnp.where(oh, rank_all, 0.0), axis=0, keepdims=True) for oh in onehots]
    eidx_ref[...] = stack_rows(idxs, jnp.int32)
    rank_ref[...] = stack_rows(ranks, jnp.int32)
    wts_ref[...] = stack_rows([w / wsum * ROUTED_SCALE for w in wsel], F32)


def _route(x, g, mods, layer, router_w, router_bias):
    tm = ROUTE_TM
    return pl.pallas_call(
        _route_kernel,
        grid=(T // tm,),
        in_specs=[
            pl.BlockSpec((tm, D), lambda i: (i, 0)),
            pl.BlockSpec((1, D), lambda i: (0, 0)),
            _mod_spec(layer, 3, tm, 1, 0),
            _mod_spec(layer, 4, tm, 1, 0),
            pl.BlockSpec((N_EXPERTS, D), lambda i: (0, 0)),
            pl.BlockSpec((N_EXPERTS, 1), lambda i: (0, 0)),
        ],
        out_specs=[
            pl.BlockSpec((tm, D), lambda i: (i, 0)),
            pl.BlockSpec((8, tm), lambda i: (0, i)),
            pl.BlockSpec((8, tm), lambda i: (0, i)),
            pl.BlockSpec((8, tm), lambda i: (0, i)),
            pl.BlockSpec((N_EXPERTS, 1), lambda i: (0, 0)),
        ],
        out_shape=[
            jax.ShapeDtypeStruct((T, D), F32),
            jax.ShapeDtypeStruct((8, T), jnp.int32),
            jax.ShapeDtypeStruct((8, T), jnp.int32),
            jax.ShapeDtypeStruct((8, T), F32),
            jax.ShapeDtypeStruct((N_EXPERTS, 1), F32),
        ],
        scratch_shapes=[pltpu.VMEM((N_EXPERTS, 1), F32)],
        compiler_params=_cparams(1),
        name="route",
    )(x, g.reshape(1, D), mods, mods, router_w.T, router_bias.reshape(N_EXPERTS, 1))


def _row_copy(src, src_row, dst, dst_row, sem):
    return pltpu.make_async_copy(src.at[pl.ds(src_row, 1), :], dst.at[pl.ds(dst_row, 1), :], sem)


def _tile_positions(pos, tm):
    return pos.reshape(TOP_K, T // tm, tm).transpose(1, 0, 2).reshape(T // tm, 1, TOP_K * tm)


def _dispatch_kernel(pos_ref, h_ref, xs_ref, sem):
    tm = ROUTE_TM

    def issue(j, carry):
        for k in range(TOP_K):
            _row_copy(h_ref, j, xs_ref, pos_ref[0, 0, k * tm + j], sem).start(priority=k % 2)
        return carry

    def drain(j, carry):
        for _ in range(TOP_K):
            _row_copy(h_ref, 0, xs_ref, 0, sem).wait()
        return carry

    lax.fori_loop(0, tm, issue, 0, unroll=ROW_UNROLL)
    lax.fori_loop(0, tm, drain, 0, unroll=ROW_UNROLL)


def _dispatch(pos, h):
    tm = ROUTE_TM
    return pl.pallas_call(
        _dispatch_kernel,
        grid=(T // tm,),
        in_specs=[
            pl.BlockSpec((1, 1, TOP_K * tm), lambda i: (i, 0, 0), memory_space=pltpu.SMEM),
            pl.BlockSpec((tm, D), lambda i: (i, 0)),
        ],
        out_specs=pl.BlockSpec(memory_space=pl.ANY),
        out_shape=jax.ShapeDtypeStruct((N_PAIRS, D), F32),
        scratch_shapes=[pltpu.SemaphoreType.DMA(())],
        compiler_params=_cparams(1),
        name="dispatch",
    )(_tile_positions(pos, tm), h)


def _combine_kernel(pos_ref, pos_next_ref, x_ref, sh_ref, w_ref, gate_ref, fg_ref, ys_ref, *rest, final):
    out_refs, (buf_ref, sem) = rest[:-2], rest[-2:]
    tm = COMBINE_TM
    i = pl.program_id(0)
    n = pl.num_programs(0)
    slot = i % 2

    def start_gathers(p_ref, dst_slot):
        def issue(j, carry):
            for k in range(TOP_K):
                _row_copy(ys_ref, p_ref[0, 0, k * tm + j], buf_ref.at[dst_slot, k], j,
                          sem.at[dst_slot]).start(priority=k % 2)
            return carry
        lax.fori_loop(0, tm, issue, 0, unroll=ROW_UNROLL)

    @pl.when(i == 0)
    def _():
        start_gathers(pos_ref, 0)

    @pl.when(i + 1 < n)
    def _():
        start_gathers(pos_next_ref, 1 - slot)

    def drain(j, carry):
        for k in range(TOP_K):
            _row_copy(ys_ref, 0, buf_ref.at[slot, k], 0, sem.at[slot]).wait()
        return carry

    lax.fori_loop(0, tm, drain, 0, unroll=ROW_UNROLL)
    w = w_ref[...]
    acc = sh_ref[...]
    for k in range(TOP_K):
        acc = acc + w[:, k:k + 1] * buf_ref[slot, k]
    xn = x_ref[...] + gate_ref[...] * acc
    if not final:
        out_refs[0][...] = xn
    else:
        xn = _rms(xn, fg_ref[...])

        @pl.when(i < T_PROMPT // tm)
        def _():
            out_refs[0][...] = xn

        @pl.when(i >= T_PROMPT // tm)
        def _():
            out_refs[1][...] = xn


def _combine(pos, x, shared, wts_t, mods, layer, final_g, ys, final):
    tm = COMBINE_TM
    n_p = T_PROMPT // tm
    if final:
        out_specs = [pl.BlockSpec((tm, D), lambda i: (jnp.minimum(i, n_p - 1), 0)),
                     pl.BlockSpec((tm, D), lambda i: (jnp.maximum(i - n_p, 0), 0))]
        out_shape = [jax.ShapeDtypeStruct((T_PROMPT, D), F32), jax.ShapeDtypeStruct((T_LAT, D), F32)]
    else:
        out_specs = [pl.BlockSpec((tm, D), lambda i: (i, 0))]
        out_shape = [jax.ShapeDtypeStruct((T, D), F32)]
    tiles = _tile_positions(pos, tm)
    return pl.pallas_call(
        functools.partial(_combine_kernel, final=final),
        grid=(T // tm,),
        in_specs=[
            pl.BlockSpec((1, 1, TOP_K * tm), lambda i: (i, 0, 0), memory_space=pltpu.SMEM),
            pl.BlockSpec((1, 1, TOP_K * tm), lambda i: (jnp.minimum(i + 1, T // tm - 1), 0, 0),
                         memory_space=pltpu.SMEM),
            pl.BlockSpec((tm, D), lambda i: (i, 0)),
            pl.BlockSpec((tm, D), lambda i: (i, 0)),
            pl.BlockSpec((tm, 8), lambda i: (i, 0)),
            pl.BlockSpec((None, None, 1, D), lambda i: (layer, _group_of_tile(i, tm), 0, 5)),
            pl.BlockSpec((1, D), lambda i: (0, 0)),
            pl.BlockSpec(memory_space=pl.ANY),
        ],
        out_specs=out_specs,
        out_shape=out_shape,
        scratch_shapes=[pltpu.VMEM((2, TOP_K, tm, D), F32), pltpu.SemaphoreType.DMA((2,))],
        compiler_params=_cparams(1),
        name="combine",
    )(tiles, tiles, x, shared, wts_t, mods, final_g.reshape(1, D), ys)


def _swiglu(x, wgu_scr, wd_scr):
    gu = jnp.dot(x.astype(BF16), wgu_scr[...], preferred_element_type=F32)
    gate, up = gu[:, :D_EXPERT], gu[:, D_EXPERT:]
    act = (gate * _sigmoid(gate) * up).astype(BF16)
    return jnp.dot(act, wd_scr[...], preferred_element_type=F32)


def _load_expert_weights(wg_ref, wu_ref, wd_ref, wgu_scr, wd_scr):
    wgu_scr[:, :D_EXPERT] = wg_ref[...].astype(BF16)
    wgu_scr[:, D_EXPERT:] = wu_ref[...].astype(BF16)
    wd_scr[...] = wd_ref[...].astype(BF16)


def _shared_kernel(h_ref, wg_ref, wu_ref, wd_ref, o_ref, wgu_scr, wd_scr):
    @pl.when(pl.program_id(0) == 0)
    def _():
        _load_expert_weights(wg_ref, wu_ref, wd_ref, wgu_scr, wd_scr)

    o_ref[...] = _swiglu(h_ref[...], wgu_scr, wd_scr)


def _shared_expert(h, layer, w_gate, w_up, w_down):
    tm = 512
    return pl.pallas_call(
        _shared_kernel,
        grid=(T // tm,),
        in_specs=[
            pl.BlockSpec((tm, D), lambda i: (i, 0)),
            pl.BlockSpec((None, D, D_EXPERT), lambda i: (layer, 0, 0)),
            pl.BlockSpec((None, D, D_EXPERT), lambda i: (layer, 0, 0)),
            pl.BlockSpec((None, D_EXPERT, D), lambda i: (layer, 0, 0)),
        ],
        out_specs=pl.BlockSpec((tm, D), lambda i: (i, 0)),
        out_shape=jax.ShapeDtypeStruct((T, D), F32),
        scratch_shapes=[pltpu.VMEM((D, 2 * D_EXPERT), BF16), pltpu.VMEM((D_EXPERT, D), BF16)],
        compiler_params=_cparams(1),
        name="shared_expert",
    )(h, w_gate, w_up, w_down)


def _gmm_kernel(e_ref, t_ref, lo_ref, hi_ref, run_ref, next_e_ref, xs_ref, wg_hbm, wu_hbm, wd_hbm, ys_ref,
                wg_buf, wu_buf, wd_buf, wsem, wgu_scr, wd_scr, *, layer):
    v = pl.program_id(0)
    prev = jnp.maximum(v - 1, 0)
    new_expert = (v == 0) | (e_ref[v] != e_ref[prev])
    new_tile = (v == 0) | (t_ref[v] != t_ref[prev])
    lo, hi = lo_ref[v], hi_ref[v]
    slot = run_ref[v] % 2

    def weight_copies(expert, dst_slot):
        return [pltpu.make_async_copy(src.at[layer, expert], dst.at[dst_slot], wsem.at[dst_slot])
                for src, dst in ((wg_hbm, wg_buf), (wu_hbm, wu_buf), (wd_hbm, wd_buf))]

    @pl.when(v == 0)
    def _():
        for c in weight_copies(e_ref[0], 0):
            c.start(priority=1)

    @pl.when(new_expert)
    def _():
        for c in weight_copies(e_ref[v], slot):
            c.wait()

        @pl.when(next_e_ref[v] >= 0)
        def _():
            for c in weight_copies(next_e_ref[v], 1 - slot):
                c.start(priority=1)

        _load_expert_weights(wg_buf.at[slot], wu_buf.at[slot], wd_buf.at[slot], wgu_scr, wd_scr)

    @pl.when(hi > lo)
    def _():
        y = _swiglu(xs_ref[...], wgu_scr, wd_scr)
        row = lax.broadcasted_iota(jnp.int32, (GMM_TM, 1), 0)
        mine = (row >= lo) & (row < hi)

        @pl.when(new_tile)
        def _():
            ys_ref[...] = jnp.where(mine, y, 0.0)

        @pl.when(jnp.logical_not(new_tile))
        def _():
            ys_ref[...] = jnp.where(mine, y, ys_ref[...])


def _gmm(sched, xs, layer, w_gate, w_up, w_down):
    return pl.pallas_call(
        functools.partial(_gmm_kernel, layer=layer),
        grid_spec=pltpu.PrefetchScalarGridSpec(
            num_scalar_prefetch=6,
            grid=(N_VISITS,),
            in_specs=[
                pl.BlockSpec((GMM_TM, D), lambda v, e, t, *_: (t[v], 0)),
                pl.BlockSpec(memory_space=pl.ANY),
                pl.BlockSpec(memory_space=pl.ANY),
                pl.BlockSpec(memory_space=pl.ANY),
            ],
            out_specs=pl.BlockSpec((GMM_TM, D), lambda v, e, t, *_: (t[v], 0)),
            scratch_shapes=[
                pltpu.VMEM((2, D, D_EXPERT), F32),
                pltpu.VMEM((2, D, D_EXPERT), F32),
                pltpu.VMEM((2, D_EXPERT, D), F32),
                pltpu.SemaphoreType.DMA((2,)),
                pltpu.VMEM((D, 2 * D_EXPERT), BF16),
                pltpu.VMEM((D_EXPERT, D), BF16),
            ],
        ),
        out_shape=jax.ShapeDtypeStruct((N_PAIRS, D), F32),
        compiler_params=_cparams(1),
        name="experts",
    )(*sched, xs, w_gate, w_up, w_down)


def _visit_schedule(counts):
    tm = GMM_TM
    ends = jnp.cumsum(counts)
    starts = ends - counts
    first_tile = starts // tm
    last_tile = jnp.maximum(ends - 1, 0) // tm
    n_vis = jnp.where(counts > 0, last_tile - first_tile + 1, 0)
    vis_end = jnp.cumsum(n_vis)
    vis_start = vis_end - n_vis
    total = vis_end[-1]
    v = jnp.minimum(jnp.arange(N_VISITS, dtype=jnp.int32), total - 1)
    e = jnp.sum((vis_end[None, :] <= v[:, None]).astype(jnp.int32), axis=1)
    tile = first_tile[e] + (v - vis_start[e])
    lo = jnp.clip(starts[e] - tile * tm, 0, tm)
    hi = jnp.clip(ends[e] - tile * tm, 0, tm)
    hi = jnp.where(jnp.arange(N_VISITS) < total, hi, lo)
    run = jnp.cumsum((e != jnp.concatenate([e[:1] - 1, e[:-1]])).astype(jnp.int32)) - 1
    later = jnp.where(e[None, :] > e[:, None], e[None, :], N_EXPERTS)
    next_e = jnp.min(later, axis=1)
    next_e = jnp.where(next_e < N_EXPERTS, next_e, -1)
    sched = (e, tile, lo, hi, run, next_e)
    return starts.astype(jnp.int32), tuple(a.astype(jnp.int32) for a in sched)


def _moe(x, mods, layer, norm_g, router_w, router_bias, w_gate, w_up, w_down,
         ws_gate, ws_up, ws_down, final_g, final):
    h, eidx, rank, wts, counts = _route(x, norm_g, mods, layer, router_w, router_bias)
    starts, sched = _visit_schedule(counts.reshape(N_EXPERTS).astype(jnp.int32))
    eidx, rank = eidx[:TOP_K], rank[:TOP_K]
    expert_ids = jnp.arange(N_EXPERTS, dtype=jnp.int32)[:, None, None]
    pos = rank + jnp.sum(jnp.where(eidx[None] == expert_ids, starts[:, None, None], 0), axis=0)
    xs = _dispatch(pos, h)
    ys = _gmm(sched, xs, layer, w_gate, w_up, w_down)
    shared = _shared_expert(h, layer, ws_gate, ws_up, ws_down)
    return _combine(pos, x, shared, wts.T, mods, layer, final_g, ys, final)


def _pool_diff_layer(x, mods, layer, j, norm_g, cache_k, cache_v, cos, sin,
                     diff_w_in, pool_w, pool_scale, lq1, lk1, lq2, lk2, subln_g, w_out):
    lam_init = 0.8 - 0.6 * math.exp(-0.3 * layer)
    h = _norm_mod(x, norm_g, mods, layer, 0, 1)
    proj = _matmul([h], diff_w_in[j], [0], POOL_W + 3 * DIFF_W, F32, tn=1024, name="diff_in_proj")
    y_pool = _pool(proj, pool_w[j], pool_scale[j])
    lam_params = jnp.stack([lq1[j], lk1[j], lq2[j], lk2[j]])
    ck = cache_k[:, j].reshape(N_LAT_SEQ * PAST_LEN, DIFF_W)
    cv = cache_v[:, j].reshape(N_LAT_SEQ * PAST_LEN, DIFF_W)
    o = _diff_attn(proj, ck, cv, lam_params, subln_g[j], cos, sin, lam_init)
    x = _matmul([y_pool, o], w_out[j], [0, 1], D, F32, tn=1024, epilogue="residual",
                residual=x, mods=mods, layer=layer, gate_chunk=2, name="diff_out_proj")
    k_new = proj[:T_PROMPT, POOL_W + DIFF_W:POOL_W + 2 * DIFF_W]
    v_new = proj[:T_PROMPT, POOL_W + 2 * DIFF_W:]
    shape = (N_PROMPT_SEQ, PROMPT_LEN, DIFF_HEADS, 2 * DIFF_DH)
    return x, k_new.reshape(shape), v_new.reshape(shape)


def _mla_layer(x, mods, layer, j, norm_g, cache_ckv, cache_kpe, cos, sin,
               w_dq, q_norm_g, w_uq, w_dkv, kv_norm_g, w_ukv, w_o):
    h = _norm_mod(x, norm_g, mods, layer, 0, 1)
    cq = _matmul([h], w_dq[j], [0], Q_LORA, BF16, epilogue="rms", gain=q_norm_g[j], name="mla_dq")
    ckv = _matmul([h], w_dkv[j], [0], KV_LORA, F32, epilogue="rms", gain=kv_norm_g[j], name="mla_dkv")
    w_kpe = w_dkv[j][:, KV_LORA:]
    kp_dup = _matmul([h], jnp.concatenate([w_kpe, w_kpe], axis=1), [0], LANES, F32, name="mla_kpe")
    w_uq3 = w_uq[j].reshape(Q_LORA, MLA_HEADS, QK_NOPE + QK_ROPE)
    w_uq_nope = w_uq3[:, :, :QK_NOPE].reshape(Q_LORA, MLA_HEADS * QK_NOPE)
    w_uq_pe = w_uq3[:, :, QK_NOPE:].reshape(Q_LORA, MLA_HEADS * QK_ROPE)
    q_nope = _matmul([cq], w_uq_nope, [0], MLA_HEADS * QK_NOPE, BF16, tn=1024, name="mla_uq_nope")
    q_pe = _matmul([cq], w_uq_pe, [0], MLA_HEADS * QK_ROPE, F32, tn=1024, name="mla_uq_pe")
    n_kv = MLA_HEADS * (QK_NOPE + V_DIM)
    kv = _matmul([ckv], w_ukv[j], [0], n_kv, BF16, tn=1024, name="mla_ukv")
    ckv_ctx = cache_ckv[:, j].reshape(N_LAT_SEQ * PAST_LEN, KV_LORA)
    kv_ctx = _matmul([ckv_ctx], w_ukv[j], [0], n_kv, BF16, tn=1024, name="mla_ukv_ctx")
    kpe_ctx = cache_kpe[:, j].reshape(N_LAT_SEQ * PAST_LEN, QK_ROPE)
    kp_ctx_dup = jnp.concatenate([kpe_ctx, kpe_ctx], axis=1)
    o = _mla_attn(q_nope, q_pe, kv, kp_dup, kv_ctx, kp_ctx_dup, cos, sin)
    x = _matmul([o], w_o[j], [0], D, F32, tn=1024, epilogue="residual",
                residual=x, mods=mods, layer=layer, gate_chunk=2, name="mla_out_proj")
    new_ckv = ckv[:T_PROMPT].reshape(N_PROMPT_SEQ, PROMPT_LEN, KV_LORA)
    new_kpe = kp_dup[:T_PROMPT, :QK_ROPE].reshape(N_PROMPT_SEQ, PROMPT_LEN, QK_ROPE)
    return x, new_ckv, new_kpe


def kernel(x_prompt, x_sample, cache_diff_k, cache_diff_v, cache_mla_ckv, cache_mla_kpe, c, c_ctx,
           ada_w, ada_b, norm_mix_g, norm_ffn_g, final_norm_g,
           diff_w_in, pool_w, pool_scale, diff_lambda_q1, diff_lambda_k1, diff_lambda_q2, diff_lambda_k2,
           diff_subln_g, even_w_out,
           mla_w_dq, mla_q_norm_g, mla_w_uq, mla_w_dkv, mla_kv_norm_g, mla_w_ukv, mla_w_o,
           router_w, router_bias, expert_w_gate, expert_w_up, expert_w_down,
           shared_w_gate, shared_w_up, shared_w_down):
    depth = ada_w.shape[0]
    x = (x_prompt.reshape(T_PROMPT, D), x_sample.reshape(T_LAT, D))
    cond = jnp.concatenate(
        [c_ctx[None, :], c, jnp.zeros((N_GROUPS_PAD - 1 - N_LAT_SEQ, D), F32)], axis=0)
    mods = _ada_params(cond, ada_w, ada_b)
    cos, sin = _rope_tables()

    new_dk, new_dv, new_ckv, new_kpe = [], [], [], []
    for i in range(depth):
        j = i // 2
        if i % 2 == 0:
            x, k_new, v_new = _pool_diff_layer(
                x, mods, i, j, norm_mix_g[i], cache_diff_k, cache_diff_v, cos, sin,
                diff_w_in, pool_w, pool_scale, diff_lambda_q1, diff_lambda_k1, diff_lambda_q2,
                diff_lambda_k2, diff_subln_g, even_w_out)
            new_dk.append(k_new)
            new_dv.append(v_new)
        else:
            x, ckv, kpe = _mla_layer(
                x, mods, i, j, norm_mix_g[i], cache_mla_ckv, cache_mla_kpe, cos, sin,
                mla_w_dq, mla_q_norm_g, mla_w_uq, mla_w_dkv, mla_kv_norm_g, mla_w_ukv, mla_w_o)
            new_ckv.append(ckv)
            new_kpe.append(kpe)
        outs = _moe(x, mods, i, norm_ffn_g[i], router_w[i], router_bias[i],
                    expert_w_gate, expert_w_up, expert_w_down,
                    shared_w_gate, shared_w_up, shared_w_down,
                    final_norm_g, final=(i == depth - 1))
        x = outs[0]
    y_prompt = outs[0].reshape(N_PROMPT_SEQ, PROMPT_LEN, D)
    y_sample = outs[1].reshape(N_LAT_SEQ, LAT_LEN, D)
    return (y_prompt, y_sample, jnp.stack(new_dk, axis=1), jnp.stack(new_dv, axis=1),
            jnp.stack(new_ckv, axis=1), jnp.stack(new_kpe, axis=1))
```

```python
import functools
import math

import jax
import jax.numpy as jnp
from jax import lax
from jax.experimental import pallas as pl
from jax.experimental.pallas import tpu as pltpu

F32 = jnp.float32
BF16 = jnp.bfloat16

D = 2048
N_PROMPT_SEQ = 16
PROMPT_LEN = 256
N_LAT_SEQ = 4
LAT_LEN = 1024
PAST_LEN = 256
T_PROMPT = N_PROMPT_SEQ * PROMPT_LEN
T_LAT = N_LAT_SEQ * LAT_LEN
T = T_PROMPT + T_LAT
UNIT = 1024
N_UNITS = T // UNIT
N_PROMPT_UNITS = T_PROMPT // UNIT
SEQ_PER_PROMPT_UNIT = UNIT // PROMPT_LEN
N_GROUPS_PAD = 8

GRID_W = 64
ROPE_BASE = 10000.0
EPS = 1e-6
POOL_W = 1024
POOL_WINDOWS = (2, 4, 8, 16)
POOL_G = 256
POOL_HALO = 8
DIFF_W = 1024
DIFF_DH = 64
DIFF_HEADS = 8
DIFF_SCALE = DIFF_DH ** -0.5
MLA_HEADS = 16
Q_LORA = 512
KV_LORA = 512
QK_NOPE = 128
QK_ROPE = 64
V_DIM = 128
MLA_SCALE = (QK_NOPE + QK_ROPE) ** -0.5
N_EXPERTS = 64
TOP_K = 6
N_EGROUPS = 8
EGROUP = N_EXPERTS // N_EGROUPS
TOPK_GROUPS = 4
D_EXPERT = 512
ROUTED_SCALE = 2.5

LANES = 128
ATT_QB = 256
ROUTE_TM = 256
COMBINE_TM = 128
GMM_TM = 256
ROW_UNROLL = 4
N_PAIRS = T * TOP_K
N_ROW_TILES = N_PAIRS // GMM_TM
N_VISITS = N_ROW_TILES + N_EXPERTS
VMEM_LIMIT = 56 * 1024 * 1024

_NT = (((1,), (1,)), ((), ()))
_NN = (((1,), (0,)), ((), ()))


def _cparams(n_axes, vmem=VMEM_LIMIT):
    return pltpu.CompilerParams(dimension_semantics=("arbitrary",) * n_axes, vmem_limit_bytes=vmem)


def _group_of_tile(i, tm):
    n_p = T_PROMPT // tm
    per_seq = LAT_LEN // tm
    return jnp.where(i < n_p, 0, 1 + (i - n_p) // per_seq)


def _sigmoid(x):
    return 1.0 / (1.0 + jnp.exp(-x))


def _rms(x, g):
    return x * lax.rsqrt(jnp.mean(x * x, axis=-1, keepdims=True) + EPS) * g


def _dot_3pass(a, b, dims):
    a_hi = a.astype(BF16)
    a_lo = (a - a_hi.astype(F32)).astype(BF16)
    b_hi = b.astype(BF16)
    b_lo = (b - b_hi.astype(F32)).astype(BF16)

    def dot(p, q):
        return lax.dot_general(p, q, dims, preferred_element_type=F32)

    return dot(a_hi, b_hi) + (dot(a_hi, b_lo) + dot(a_lo, b_hi))


def _ada_kernel(c_ref, w_ref, b_ref, o_ref):
    c = c_ref[...]
    s = c * _sigmoid(c)
    o_ref[...] = _dot_3pass(s, w_ref[...], _NN) + b_ref[...]


def _ada_params(cond, ada_w, ada_b):
    depth, _, n = ada_w.shape
    tn = 1024
    out = pl.pallas_call(
        _ada_kernel,
        grid=(depth, n // tn),
        in_specs=[
            pl.BlockSpec((N_GROUPS_PAD, D), lambda l, j: (0, 0)),
            pl.BlockSpec((None, D, tn), lambda l, j: (l, 0, j)),
            pl.BlockSpec((None, 1, tn), lambda l, j: (l, 0, j)),
        ],
        out_specs=pl.BlockSpec((None, N_GROUPS_PAD, tn), lambda l, j: (l, 0, j)),
        out_shape=jax.ShapeDtypeStruct((depth, N_GROUPS_PAD, n), F32),
        compiler_params=_cparams(2),
        name="ada_params",
    )(cond, ada_w, ada_b.reshape(depth, 1, n))
    return out.reshape(depth, N_GROUPS_PAD, 1, n)


def _mod_spec(layer, chunk, tm, grid_rank, row_axis):
    def index_map(*ids):
        return (layer, _group_of_tile(ids[row_axis], tm), 0, chunk)
    del grid_rank
    return pl.BlockSpec((None, None, 1, D), index_map)


def _row_source(x):
    return list(x) if isinstance(x, (tuple, list)) else [x]


def _row_source_specs(parts, tm, width, col_of, row_axis):
    if len(parts) == 1:
        return [pl.BlockSpec((tm, width), lambda *ids: (ids[row_axis], col_of(*ids)))]
    n_p = T_PROMPT // tm
    return [pl.BlockSpec((tm, width), lambda *ids: (jnp.minimum(ids[row_axis], n_p - 1), col_of(*ids))),
            pl.BlockSpec((tm, width), lambda *ids: (jnp.maximum(ids[row_axis] - n_p, 0), col_of(*ids)))]


def _read_rows(refs, row_tile, tm):
    if len(refs) == 1:
        return refs[0][...]
    return jnp.where(row_tile < T_PROMPT // tm, refs[0][...], refs[1][...])


def _norm_mod_kernel(*refs, n_x, tm):
    x_refs = refs[:n_x]
    g_ref, sh_ref, sc_ref, o_ref = refs[n_x:]
    y = _rms(_read_rows(x_refs, pl.program_id(0), tm), g_ref[...])
    o_ref[...] = (y * (1.0 + sc_ref[...]) + sh_ref[...]).astype(o_ref.dtype)


def _norm_mod(x, g, mods, layer, shift_chunk, scale_chunk):
    tm = 512
    parts = _row_source(x)
    return pl.pallas_call(
        functools.partial(_norm_mod_kernel, n_x=len(parts), tm=tm),
        grid=(T // tm,),
        in_specs=_row_source_specs(parts, tm, D, lambda i: 0, 0) + [
            pl.BlockSpec((1, D), lambda i: (0, 0)),
            _mod_spec(layer, shift_chunk, tm, 1, 0),
            _mod_spec(layer, scale_chunk, tm, 1, 0),
        ],
        out_specs=pl.BlockSpec((tm, D), lambda i: (i, 0)),
        out_shape=jax.ShapeDtypeStruct((T, D), BF16),
        compiler_params=_cparams(1),
        name="norm_mod",
    )(*parts, g.reshape(1, D), mods, mods)


def _mm_kernel(*refs, n_a, epilogue, n_res, tm):
    a_refs = refs[:n_a]
    w_refs = refs[n_a:2 * n_a]
    pos = 2 * n_a
    if epilogue == "rms":
        g_ref = refs[pos]
        pos += 1
    elif epilogue == "residual":
        res_refs, gate_ref = refs[pos:pos + n_res], refs[pos + n_res]
        pos += n_res + 1
    o_ref = refs[pos]
    w_scr = refs[pos + 1:]

    @pl.when(pl.program_id(1) == 0)
    def _():
        for w_ref, scr in zip(w_refs, w_scr):
            scr[...] = w_ref[...].astype(BF16)

    acc = None
    for a_ref, scr in zip(a_refs, w_scr):
        part = jnp.dot(a_ref[...].astype(BF16), scr[...], preferred_element_type=F32)
        acc = part if acc is None else acc + part
    if epilogue == "rms":
        acc = _rms(acc, g_ref[...])
    elif epilogue == "residual":
        acc = _read_rows(res_refs, pl.program_id(1), tm) + gate_ref[...] * acc
    o_ref[...] = acc.astype(o_ref.dtype)


def _matmul(a_list, w, w_row_blocks, n_out, out_dtype, *, tm=512, tn=None, epilogue=None,
            gain=None, residual=None, mods=None, layer=None, gate_chunk=None, name="matmul"):
    m = a_list[0].shape[0]
    tn = n_out if tn is None else tn
    n_a = len(a_list)
    in_specs, args = [], []
    for a in a_list:
        k = a.shape[1]
        in_specs.append(pl.BlockSpec((tm, k), lambda j, i: (i, 0)))
        args.append(a)
    scratch = []
    for a, rb in zip(a_list, w_row_blocks):
        k = a.shape[1]
        in_specs.append(pl.BlockSpec((k, tn), lambda j, i, rb=rb: (rb, j)))
        args.append(w)
        scratch.append(pltpu.VMEM((k, tn), BF16))
    if epilogue == "rms":
        assert tn == n_out
        in_specs.append(pl.BlockSpec((1, tn), lambda j, i: (0, 0)))
        args.append(gain.reshape(1, n_out))
    elif epilogue == "residual":
        res_parts = _row_source(residual)
        in_specs += _row_source_specs(res_parts, tm, tn, lambda j, i: j, 1)
        args += res_parts
        in_specs.append(pl.BlockSpec(
            (None, None, 1, tn),
            lambda j, i: (layer, _group_of_tile(i, tm), 0, gate_chunk * (D // tn) + j)))
        args.append(mods)
    return pl.pallas_call(
        functools.partial(_mm_kernel, n_a=n_a, epilogue=epilogue, tm=tm,
                          n_res=len(_row_source(residual)) if epilogue == "residual" else 0),
        grid=(n_out // tn, m // tm),
        in_specs=in_specs,
        out_specs=pl.BlockSpec((tm, tn), lambda j, i: (i, j)),
        out_shape=jax.ShapeDtypeStruct((m, n_out), out_dtype),
        scratch_shapes=scratch,
        compiler_params=_cparams(2),
        name=name,
    )(*args)


def _pool_seq(u_ref, row0, seq_len, pw_ref, scale_ref, o_ref, pad_ref):
    t = lax.broadcasted_iota(jnp.int32, (seq_len, 1), 0)
    zeros = jnp.zeros((POOL_HALO, POOL_G), F32)
    for g, w in enumerate(POOL_WINDOWS):
        cols = slice(g * POOL_G, (g + 1) * POOL_G)
        ug = u_ref[row0:row0 + seq_len, cols]
        pad_ref[0:POOL_HALO, :] = zeros
        pad_ref[POOL_HALO:POOL_HALO + seq_len, :] = ug
        pad_ref[POOL_HALO + seq_len:2 * POOL_HALO + seq_len, :] = zeros
        total = None
        for off in range(-(w // 2), w // 2):
            part = pad_ref[POOL_HALO + off:POOL_HALO + off + seq_len, :]
            total = part if total is None else total + part
        cnt = (jnp.minimum(t + w // 2, seq_len) - jnp.maximum(t - w // 2, 0)).astype(F32)
        d = (total / cnt - ug).astype(BF16)
        y = jnp.dot(d, pw_ref[g].astype(BF16), preferred_element_type=F32)
        o_ref[row0:row0 + seq_len, cols] = (y * scale_ref[:, cols]).astype(o_ref.dtype)


def _pool_kernel(u_ref, pw_ref, scale_ref, o_ref, pad_ref):
    unit = pl.program_id(0)

    @pl.when(unit < N_PROMPT_UNITS)
    def _():
        for s in range(SEQ_PER_PROMPT_UNIT):
            _pool_seq(u_ref, s * PROMPT_LEN, PROMPT_LEN, pw_ref, scale_ref, o_ref, pad_ref)

    @pl.when(unit >= N_PROMPT_UNITS)
    def _():
        _pool_seq(u_ref, 0, LAT_LEN, pw_ref, scale_ref, o_ref, pad_ref)


def _pool(proj, pool_w, pool_scale):
    return pl.pallas_call(
        _pool_kernel,
        grid=(N_UNITS,),
        in_specs=[
            pl.BlockSpec((UNIT, POOL_W), lambda u: (u, 0)),
            pl.BlockSpec((len(POOL_WINDOWS), POOL_G, POOL_G), lambda u: (0, 0, 0)),
            pl.BlockSpec((1, POOL_W), lambda u: (0, 0)),
        ],
        out_specs=pl.BlockSpec((UNIT, POOL_W), lambda u: (u, 0)),
        out_shape=jax.ShapeDtypeStruct((T, POOL_W), BF16),
        scratch_shapes=[pltpu.VMEM((LAT_LEN + 2 * POOL_HALO, POOL_G), F32)],
        compiler_params=_cparams(1),
        name="pool",
    )(proj, pool_w, pool_scale.reshape(1, POOL_W))


def _rope_tables():
    pos = jnp.arange(LAT_LEN)
    row = (pos // GRID_W).astype(F32)
    col = (pos % GRID_W).astype(F32)
    quarter = QK_ROPE // 4
    inv = ROPE_BASE ** (-jnp.arange(quarter, dtype=F32) / quarter)
    lane = jnp.arange(LANES)
    axis = (lane % QK_ROPE) // (QK_ROPE // 2)
    freq = inv[lane % quarter]
    p = jnp.where(axis[None, :] == 0, row[:, None], col[:, None])
    ang = p * freq[None, :]
    sign = jnp.where((lane % (QK_ROPE // 2)) < quarter, -1.0, 1.0).astype(F32)
    return jnp.cos(ang), jnp.sin(ang) * sign[None, :]


def _rope(x, cos, sin_signed):
    quarter = QK_ROPE // 4
    lane = lax.broadcasted_iota(jnp.int32, x.shape, 1)
    first_half = (lane % (QK_ROPE // 2)) < quarter
    partner = jnp.where(first_half, pltpu.roll(x, LANES - quarter, 1), pltpu.roll(x, quarter, 1))
    return x * cos + partner * sin_signed


def _diff_core(q, k, v, lam, g, lam_init):
    lane = lax.broadcasted_iota(jnp.int32, q.shape, 1)
    low = lane < DIFF_DH
    q1 = jnp.where(low, q, 0.0).astype(BF16)
    q2 = jnp.where(low, 0.0, q).astype(BF16)

    def probs(qm):
        s = lax.dot_general(qm, k, _NT, preferred_element_type=F32)
        e = jnp.exp(s - jnp.max(s, axis=-1, keepdims=True))
        return e * (1.0 / jnp.sum(e, axis=-1, keepdims=True))

    w = (probs(q1) - lam * probs(q2)).astype(BF16)
    o = jnp.dot(w, v, preferred_element_type=F32)
    return _rms(o, g) * (1.0 - lam_init)


def _diff_attn_kernel(lp_ref, q_ref, k_ref, v_ref, ck_ref, cv_ref, cos_ref, sin_ref, g_ref, o_ref,
                      *, lam_init):
    unit = pl.program_id(0)
    lp = lp_ref[...]
    lam = (jnp.exp(jnp.sum(lp[0:1] * lp[1:2], axis=-1, keepdims=True))
           - jnp.exp(jnp.sum(lp[2:3] * lp[3:4], axis=-1, keepdims=True)) + lam_init)
    g = g_ref[...]

    @pl.when(unit < N_PROMPT_UNITS)
    def _():
        for s in range(SEQ_PER_PROMPT_UNIT):
            rows = slice(s * PROMPT_LEN, (s + 1) * PROMPT_LEN)
            q = q_ref[rows, :] * DIFF_SCALE
            k = k_ref[rows, :].astype(BF16)
            v = v_ref[rows, :].astype(BF16)
            o_ref[rows, :] = _diff_core(q, k, v, lam, g, lam_init).astype(o_ref.dtype)

    @pl.when(unit >= N_PROMPT_UNITS)
    def _():
        cos, sin = cos_ref[...], sin_ref[...]
        k = jnp.concatenate([ck_ref[...].astype(BF16),
                             _rope(k_ref[...], cos, sin).astype(BF16)], axis=0)
        v = jnp.concatenate([cv_ref[...].astype(BF16), v_ref[...].astype(BF16)], axis=0)
        for b in range(LAT_LEN // ATT_QB):
            rows = slice(b * ATT_QB, (b + 1) * ATT_QB)
            q = _rope(q_ref[rows, :], cos[rows], sin[rows]) * DIFF_SCALE
            o_ref[rows, :] = _diff_core(q, k, v, lam, g, lam_init).astype(o_ref.dtype)


def _diff_attn(proj, cache_k, cache_v, lam_params, subln_g, cos, sin, lam_init):
    hd = 2 * DIFF_DH
    q0, k0, v0 = POOL_W // hd, (POOL_W + DIFF_W) // hd, (POOL_W + 2 * DIFF_W) // hd

    def ctx_map(u, h):
        return (jnp.maximum(u - N_PROMPT_UNITS, 0), h)

    return pl.pallas_call(
        functools.partial(_diff_attn_kernel, lam_init=lam_init),
        grid=(N_UNITS, DIFF_HEADS),
        in_specs=[
            pl.BlockSpec((4, DIFF_DH), lambda u, h: (0, 0)),
            pl.BlockSpec((UNIT, hd), lambda u, h: (u, q0 + h)),
            pl.BlockSpec((UNIT, hd), lambda u, h: (u, k0 + h)),
            pl.BlockSpec((UNIT, hd), lambda u, h: (u, v0 + h)),
            pl.BlockSpec((PAST_LEN, hd), ctx_map),
            pl.BlockSpec((PAST_LEN, hd), ctx_map),
            pl.BlockSpec((LAT_LEN, LANES), lambda u, h: (0, 0)),
            pl.BlockSpec((LAT_LEN, LANES), lambda u, h: (0, 0)),
            pl.BlockSpec((1, hd), lambda u, h: (0, 0)),
        ],
        out_specs=pl.BlockSpec((UNIT, hd), lambda u, h: (u, h)),
        out_shape=jax.ShapeDtypeStruct((T, DIFF_W), BF16),
        compiler_params=_cparams(2),
        name="diff_attn",
    )(lam_params, proj, proj, proj, cache_k, cache_v, cos, sin, subln_g.reshape(1, hd))


def _mla_core(qc, kc, v):
    s = lax.dot_general(qc, kc, _NT, preferred_element_type=F32) * MLA_SCALE
    e = jnp.exp(s - jnp.max(s, axis=-1, keepdims=True))
    p = (e * (1.0 / jnp.sum(e, axis=-1, keepdims=True))).astype(BF16)
    return jnp.dot(p, v, preferred_element_type=F32)


def _mla_attn_kernel(qn_ref, qp_ref, kn_ref, v_ref, kp_ref, ckn_ref, cv_ref, ckp_ref, cos_ref, sin_ref,
                     o_ref):
    unit = pl.program_id(0)
    head = pl.program_id(1)
    lane = lax.broadcasted_iota(jnp.int32, (ATT_QB, LANES), 1)
    mine = (lane // QK_ROPE) == (head % 2)

    @pl.when(unit < N_PROMPT_UNITS)
    def _():
        for s in range(SEQ_PER_PROMPT_UNIT):
            rows = slice(s * PROMPT_LEN, (s + 1) * PROMPT_LEN)
            qp = jnp.where(mine, qp_ref[rows, :], 0.0).astype(BF16)
            qc = jnp.concatenate([qn_ref[rows, :], qp], axis=1)
            kc = jnp.concatenate([kn_ref[rows, :], kp_ref[rows, :].astype(BF16)], axis=1)
            o_ref[rows, :] = _mla_core(qc, kc, v_ref[rows, :]).astype(o_ref.dtype)

    @pl.when(unit >= N_PROMPT_UNITS)
    def _():
        cos, sin = cos_ref[...], sin_ref[...]
        k_ctx = jnp.concatenate([ckn_ref[...], ckp_ref[...].astype(BF16)], axis=1)
        k_new = jnp.concatenate([kn_ref[...], _rope(kp_ref[...], cos, sin).astype(BF16)], axis=1)
        kc = jnp.concatenate([k_ctx, k_new], axis=0)
        v = jnp.concatenate([cv_ref[...], v_ref[...]], axis=0)
        for b in range(LAT_LEN // ATT_QB):
            rows = slice(b * ATT_QB, (b + 1) * ATT_QB)
            qp = jnp.where(mine, _rope(qp_ref[rows, :], cos[rows], sin[rows]), 0.0).astype(BF16)
            qc = jnp.concatenate([qn_ref[rows, :], qp], axis=1)
            o_ref[rows, :] = _mla_core(qc, kc, v).astype(o_ref.dtype)


def _mla_attn(q_nope, q_pe, kv, kp_dup, kv_ctx, kp_ctx_dup, cos, sin):
    def ctx_row(u):
        return jnp.maximum(u - N_PROMPT_UNITS, 0)

    return pl.pallas_call(
        _mla_attn_kernel,
        grid=(N_UNITS, MLA_HEADS),
        in_specs=[
            pl.BlockSpec((UNIT, QK_NOPE), lambda u, h: (u, h)),
            pl.BlockSpec((UNIT, LANES), lambda u, h: (u, h // 2)),
            pl.BlockSpec((UNIT, QK_NOPE), lambda u, h: (u, 2 * h)),
            pl.BlockSpec((UNIT, V_DIM), lambda u, h: (u, 2 * h + 1)),
            pl.BlockSpec((UNIT, LANES), lambda u, h: (u, 0)),
            pl.BlockSpec((PAST_LEN, QK_NOPE), lambda u, h: (ctx_row(u), 2 * h)),
            pl.BlockSpec((PAST_LEN, V_DIM), lambda u, h: (ctx_row(u), 2 * h + 1)),
            pl.BlockSpec((PAST_LEN, LANES), lambda u, h: (ctx_row(u), 0)),
            pl.BlockSpec((LAT_LEN, LANES), lambda u, h: (0, 0)),
            pl.BlockSpec((LAT_LEN, LANES), lambda u, h: (0, 0)),
        ],
        out_specs=pl.BlockSpec((UNIT, V_DIM), lambda u, h: (u, h)),
        out_shape=jax.ShapeDtypeStruct((T, MLA_HEADS * V_DIM), BF16),
        compiler_params=_cparams(2),
        name="mla_attn",
    )(q_nope, q_pe, kv, kv, kp_dup, kv_ctx, kv_ctx, kp_ctx_dup, cos, sin)


def _route_kernel(x_ref, g_ref, sh_ref, sc_ref, rw_ref, rb_ref,
                  h_ref, eidx_ref, rank_ref, wts_ref, cnt_ref, carry_ref):
    tm = ROUTE_TM

    @pl.when(pl.program_id(0) == 0)
    def _():
        carry_ref[...] = jnp.zeros_like(carry_ref)

    h = _rms(x_ref[...], g_ref[...]) * (1.0 + sc_ref[...]) + sh_ref[...]
    h_ref[...] = h
    logits = _dot_3pass(rw_ref[...], h, _NT)
    scores = _sigmoid(logits)
    choice = scores + rb_ref[...]
    neg = -jnp.inf

    def take_max(vals, iota, n):
        m = jnp.max(vals, axis=0, keepdims=True)
        idx = jnp.min(jnp.where(vals == m, iota, n), axis=0, keepdims=True)
        return iota == idx, m, idx

    iota8 = lax.broadcasted_iota(jnp.int32, (8, tm), 0)

    def stack_rows(rows, dtype):
        out = jnp.zeros((8, tm), dtype)
        for k, r in enumerate(rows):
            out = jnp.where(iota8 == k, r.astype(dtype), out)
        return out

    gscore = []
    for gi in range(N_EGROUPS):
        grp = choice[gi * EGROUP:(gi + 1) * EGROUP]
        oh, m1, _ = take_max(grp, iota8, EGROUP)
        m2 = jnp.max(jnp.where(oh, neg, grp), axis=0, keepdims=True)
        gscore.append(m1 + m2)
    cur = stack_rows(gscore, F32)
    gsel = jnp.zeros((N_EGROUPS, tm), F32)
    for _ in range(TOPK_GROUPS):
        oh, _, _ = take_max(cur, iota8, N_EGROUPS)
        gsel = jnp.where(oh, 1.0, gsel)
        cur = jnp.where(oh, neg, cur)
    emask = jnp.concatenate(
        [jnp.broadcast_to(gsel[gi:gi + 1], (EGROUP, tm)) for gi in range(N_EGROUPS)], axis=0)
    masked = jnp.where(emask > 0.0, choice, neg)

    iota_e = lax.broadcasted_iota(jnp.int32, (N_EXPERTS, tm), 0)
    onehots, idxs, wsel = [], [], []
    sel = jnp.zeros((N_EXPERTS, tm), F32)
    for _ in range(TOP_K):
        oh, _, idx = take_max(masked, iota_e, N_EXPERTS)
        onehots.append(oh)
        idxs.append(idx)
        wsel.append(jnp.sum(jnp.where(oh, scores, 0.0), axis=0, keepdims=True))
        masked = jnp.where(oh, neg, masked)
        sel = jnp.where(oh, 1.0, sel)
    wsum = wsel[0]
    for w in wsel[1:]:
        wsum = wsum + w

    before = jnp.where(lax.broadcasted_iota(jnp.int32, (tm, tm), 0)
                       < lax.broadcasted_iota(jnp.int32, (tm, tm), 1), 1.0, 0.0).astype(BF16)
    rank_all = carry_ref[...] + jnp.dot(sel.astype(BF16), before, preferred_element_type=F32)
    carry_ref[...] = carry_ref[...] + jnp.sum(sel, axis=1, keepdims=True)
    cnt_ref[...] = carry_ref[...]

    ranks = [jnp.sum(jnp.where(oh, rank_all, 0.0), axis=0, keepdims=True) for oh in onehots]
    eidx_ref[...] = stack_rows(idxs, jnp.int32)
    rank_ref[...] = stack_rows(ranks, jnp.int32)
    wts_ref[...] = stack_rows([w / wsum * ROUTED_SCALE for w in wsel], F32)


def _route(x, g, mods, layer, router_w, router_bias):
    tm = ROUTE_TM
    return pl.pallas_call(
        _route_kernel,
        grid=(T // tm,),
        in_specs=[
            pl.BlockSpec((tm, D), lambda i: (i, 0)),
            pl.BlockSpec((1, D), lambda i: (0, 0)),
            _mod_spec(layer, 3, tm, 1, 0),
            _mod_spec(layer, 4, tm, 1, 0),
            pl.BlockSpec((N_EXPERTS, D), lambda i: (0, 0)),
            pl.BlockSpec((N_EXPERTS, 1), lambda i: (0, 0)),
        ],
        out_specs=[
            pl.BlockSpec((tm, D), lambda i: (i, 0)),
            pl.BlockSpec((8, tm), lambda i: (0, i)),
            pl.BlockSpec((8, tm), lambda i: (0, i)),
            pl.BlockSpec((8, tm), lambda i: (0, i)),
            pl.BlockSpec((N_EXPERTS, 1), lambda i: (0, 0)),
        ],
        out_shape=[
            jax.ShapeDtypeStruct((T, D), F32),
            jax.ShapeDtypeStruct((8, T), jnp.int32),
            jax.ShapeDtypeStruct((8, T), jnp.int32),
            jax.ShapeDtypeStruct((8, T), F32),
            jax.ShapeDtypeStruct((N_EXPERTS, 1), F32),
        ],
        scratch_shapes=[pltpu.VMEM((N_EXPERTS, 1), F32)],
        compiler_params=_cparams(1),
        name="route",
    )(x, g.reshape(1, D), mods, mods, router_w.T, router_bias.reshape(N_EXPERTS, 1))


def _row_copy(src, src_row, dst, dst_row, sem):
    return pltpu.make_async_copy(src.at[pl.ds(src_row, 1), :], dst.at[pl.ds(dst_row, 1), :], sem)


def _tile_positions(pos, tm):
    return pos.reshape(TOP_K, T // tm, tm).transpose(1, 0, 2).reshape(T // tm, 1, TOP_K * tm)


def _dispatch_kernel(pos_ref, h_ref, xs_ref, sem):
    tm = ROUTE_TM

    def issue(j, carry):
        for k in range(TOP_K):
            _row_copy(h_ref, j, xs_ref, pos_ref[0, 0, k * tm + j], sem).start(priority=k % 2)
        return carry

    def drain(j, carry):
        for _ in range(TOP_K):
            _row_copy(h_ref, 0, xs_ref, 0, sem).wait()
        return carry

    lax.fori_loop(0, tm, issue, 0, unroll=ROW_UNROLL)
    lax.fori_loop(0, tm, drain, 0, unroll=ROW_UNROLL)


def _dispatch(pos, h):
    tm = ROUTE_TM
    return pl.pallas_call(
        _dispatch_kernel,
        grid=(T // tm,),
        in_specs=[
            pl.BlockSpec((1, 1, TOP_K * tm), lambda i: (i, 0, 0), memory_space=pltpu.SMEM),
            pl.BlockSpec((tm, D), lambda i: (i, 0)),
        ],
        out_specs=pl.BlockSpec(memory_space=pl.ANY),
        out_shape=jax.ShapeDtypeStruct((N_PAIRS, D), F32),
        scratch_shapes=[pltpu.SemaphoreType.DMA(())],
        compiler_params=_cparams(1),
        name="dispatch",
    )(_tile_positions(pos, tm), h)


def _combine_kernel(pos_ref, pos_next_ref, x_ref, sh_ref, w_ref, gate_ref, ng_ref, nsh_ref, nsc_ref,
                    ys_ref, *rest, final):
    out_refs, (buf_ref, sem) = rest[:-2], rest[-2:]
    tm = COMBINE_TM
    i = pl.program_id(0)
    n = pl.num_programs(0)
    slot = i % 2

    def start_gathers(p_ref, dst_slot):
        def issue(j, carry):
            for k in range(TOP_K):
                _row_copy(ys_ref, p_ref[0, 0, k * tm + j], buf_ref.at[dst_slot, k], j,
                          sem.at[dst_slot]).start(priority=k % 2)
            return carry
        lax.fori_loop(0, tm, issue, 0, unroll=ROW_UNROLL)

    @pl.when(i == 0)
    def _():
        start_gathers(pos_ref, 0)

    @pl.when(i + 1 < n)
    def _():
        start_gathers(pos_next_ref, 1 - slot)

    def drain(j, carry):
        for k in range(TOP_K):
            _row_copy(ys_ref, 0, buf_ref.at[slot, k], 0, sem.at[slot]).wait()
        return carry

    lax.fori_loop(0, tm, drain, 0, unroll=ROW_UNROLL)
    w = w_ref[...]
    acc = sh_ref[...]
    for k in range(TOP_K):
        acc = acc + w[:, k:k + 1] * buf_ref[slot, k]
    xn = x_ref[...] + gate_ref[...] * acc
    normed = _rms(xn, ng_ref[...])
    if not final:
        out_refs[0][...] = xn
        out_refs[1][...] = (normed * (1.0 + nsc_ref[...]) + nsh_ref[...]).astype(BF16)
    else:
        xn = normed

        @pl.when(i < T_PROMPT // tm)
        def _():
            out_refs[0][...] = xn

        @pl.when(i >= T_PROMPT // tm)
        def _():
            out_refs[1][...] = xn


def _combine(pos, x, shared, wts_t, mods, layer, next_norm_g, ys, final):
    tm = COMBINE_TM
    n_p = T_PROMPT // tm
    next_layer = layer if final else layer + 1
    if final:
        out_specs = [pl.BlockSpec((tm, D), lambda i: (jnp.minimum(i, n_p - 1), 0)),
                     pl.BlockSpec((tm, D), lambda i: (jnp.maximum(i - n_p, 0), 0))]
        out_shape = [jax.ShapeDtypeStruct((T_PROMPT, D), F32), jax.ShapeDtypeStruct((T_LAT, D), F32)]
    else:
        out_specs = [pl.BlockSpec((tm, D), lambda i: (i, 0)), pl.BlockSpec((tm, D), lambda i: (i, 0))]
        out_shape = [jax.ShapeDtypeStruct((T, D), F32), jax.ShapeDtypeStruct((T, D), BF16)]
    tiles = _tile_positions(pos, tm)
    return pl.pallas_call(
        functools.partial(_combine_kernel, final=final),
        grid=(T // tm,),
        in_specs=[
            pl.BlockSpec((1, 1, TOP_K * tm), lambda i: (i, 0, 0), memory_space=pltpu.SMEM),
            pl.BlockSpec((1, 1, TOP_K * tm), lambda i: (jnp.minimum(i + 1, T // tm - 1), 0, 0),
                         memory_space=pltpu.SMEM),
            pl.BlockSpec((tm, D), lambda i: (i, 0)),
            pl.BlockSpec((tm, D), lambda i: (i, 0)),
            pl.BlockSpec((tm, 8), lambda i: (i, 0)),
            pl.BlockSpec((None, None, 1, D), lambda i: (layer, _group_of_tile(i, tm), 0, 5)),
            pl.BlockSpec((1, D), lambda i: (0, 0)),
            _mod_spec(next_layer, 0, tm, 1, 0),
            _mod_spec(next_layer, 1, tm, 1, 0),
            pl.BlockSpec(memory_space=pl.ANY),
        ],
        out_specs=out_specs,
        out_shape=out_shape,
        scratch_shapes=[pltpu.VMEM((2, TOP_K, tm, D), F32), pltpu.SemaphoreType.DMA((2,))],
        compiler_params=_cparams(1),
        name="combine",
    )(tiles, tiles, x, shared, wts_t, mods, next_norm_g.reshape(1, D), mods, mods, ys)


def _swiglu(x, wgu_scr, wd_scr):
    gu = jnp.dot(x, wgu_scr[...], preferred_element_type=F32)
    gate, up = gu[:, :D_EXPERT], gu[:, D_EXPERT:]
    act = (gate * _sigmoid(gate) * up).astype(BF16)
    return jnp.dot(act, wd_scr[...], preferred_element_type=F32)


def _load_expert_weights(wg_ref, wu_ref, wd_ref, wgu_scr, wd_scr):
    wgu_scr[:, :D_EXPERT] = wg_ref[...].astype(BF16)
    wgu_scr[:, D_EXPERT:] = wu_ref[...].astype(BF16)
    wd_scr[...] = wd_ref[...].astype(BF16)


def _shared_kernel(h_ref, wg_ref, wu_ref, wd_ref, o_ref, wgu_scr, wd_scr):
    @pl.when(pl.program_id(0) == 0)
    def _():
        _load_expert_weights(wg_ref, wu_ref, wd_ref, wgu_scr, wd_scr)

    o_ref[...] = _swiglu(h_ref[...].astype(BF16), wgu_scr, wd_scr)


def _shared_expert(h, layer, w_gate, w_up, w_down):
    tm = 512
    return pl.pallas_call(
        _shared_kernel,
        grid=(T // tm,),
        in_specs=[
            pl.BlockSpec((tm, D), lambda i: (i, 0)),
            pl.BlockSpec((None, D, D_EXPERT), lambda i: (layer, 0, 0)),
            pl.BlockSpec((None, D, D_EXPERT), lambda i: (layer, 0, 0)),
            pl.BlockSpec((None, D_EXPERT, D), lambda i: (layer, 0, 0)),
        ],
        out_specs=pl.BlockSpec((tm, D), lambda i: (i, 0)),
        out_shape=jax.ShapeDtypeStruct((T, D), F32),
        scratch_shapes=[pltpu.VMEM((D, 2 * D_EXPERT), BF16), pltpu.VMEM((D_EXPERT, D), BF16)],
        compiler_params=_cparams(1),
        name="shared_expert",
    )(h, w_gate, w_up, w_down)


def _gmm_kernel(e_ref, t_ref, lo_ref, hi_ref, run_ref, next_e_ref, xs_ref, wg_hbm, wu_hbm, wd_hbm, ys_ref,
                wg_buf, wu_buf, wd_buf, wsem, wgu_scr, wd_scr, *, layer):
    v = pl.program_id(0)
    prev = jnp.maximum(v - 1, 0)
    new_expert = (v == 0) | (e_ref[v] != e_ref[prev])
    new_tile = (v == 0) | (t_ref[v] != t_ref[prev])
    lo, hi = lo_ref[v], hi_ref[v]
    slot = run_ref[v] % 2

    def weight_copies(expert, dst_slot):
        return [pltpu.make_async_copy(src.at[layer, expert], dst.at[dst_slot], wsem.at[dst_slot])
                for src, dst in ((wg_hbm, wg_buf), (wu_hbm, wu_buf), (wd_hbm, wd_buf))]

    @pl.when(v == 0)
    def _():
        for c in weight_copies(e_ref[0], 0):
            c.start(priority=1)

    @pl.when(new_expert)
    def _():
        for c in weight_copies(e_ref[v], slot):
            c.wait()

        @pl.when(next_e_ref[v] >= 0)
        def _():
            for c in weight_copies(next_e_ref[v], 1 - slot):
                c.start(priority=1)

        _load_expert_weights(wg_buf.at[slot], wu_buf.at[slot], wd_buf.at[slot], wgu_scr, wd_scr)

    @pl.when(hi > lo)
    def _():
        y = _swiglu(xs_ref[...].astype(BF16), wgu_scr, wd_scr)
        row = lax.broadcasted_iota(jnp.int32, (GMM_TM, 1), 0)
        mine = (row >= lo) & (row < hi)

        @pl.when(new_tile)
        def _():
            ys_ref[...] = jnp.where(mine, y, 0.0)

        @pl.when(jnp.logical_not(new_tile))
        def _():
            ys_ref[...] = jnp.where(mine, y, ys_ref[...])


def _gmm(sched, xs, layer, w_gate, w_up, w_down):
    return pl.pallas_call(
        functools.partial(_gmm_kernel, layer=layer),
        grid_spec=pltpu.PrefetchScalarGridSpec(
            num_scalar_prefetch=6,
            grid=(N_VISITS,),
            in_specs=[
                pl.BlockSpec((GMM_TM, D), lambda v, e, t, *_: (t[v], 0)),
                pl.BlockSpec(memory_space=pl.ANY),
                pl.BlockSpec(memory_space=pl.ANY),
                pl.BlockSpec(memory_space=pl.ANY),
            ],
            out_specs=pl.BlockSpec((GMM_TM, D), lambda v, e, t, *_: (t[v], 0)),
            scratch_shapes=[
                pltpu.VMEM((2, D, D_EXPERT), F32),
                pltpu.VMEM((2, D, D_EXPERT), F32),
                pltpu.VMEM((2, D_EXPERT, D), F32),
                pltpu.SemaphoreType.DMA((2,)),
                pltpu.VMEM((D, 2 * D_EXPERT), BF16),
                pltpu.VMEM((D_EXPERT, D), BF16),
            ],
        ),
        out_shape=jax.ShapeDtypeStruct((N_PAIRS, D), F32),
        compiler_params=_cparams(1),
        name="experts",
    )(*sched, xs, w_gate, w_up, w_down)


def _visit_schedule(counts):
    tm = GMM_TM
    ends = jnp.cumsum(counts)
    starts = ends - counts
    first_tile = starts // tm
    last_tile = jnp.maximum(ends - 1, 0) // tm
    n_vis = jnp.where(counts > 0, last_tile - first_tile + 1, 0)
    vis_end = jnp.cumsum(n_vis)
    vis_start = vis_end - n_vis
    total = vis_end[-1]
    v = jnp.minimum(jnp.arange(N_VISITS, dtype=jnp.int32), total - 1)
    e = jnp.sum((vis_end[None, :] <= v[:, None]).astype(jnp.int32), axis=1)
    tile = first_tile[e] + (v - vis_start[e])
    lo = jnp.clip(starts[e] - tile * tm, 0, tm)
    hi = jnp.clip(ends[e] - tile * tm, 0, tm)
    hi = jnp.where(jnp.arange(N_VISITS) < total, hi, lo)
    run = jnp.cumsum((e != jnp.concatenate([e[:1] - 1, e[:-1]])).astype(jnp.int32)) - 1
    later = jnp.where(e[None, :] > e[:, None], e[None, :], N_EXPERTS)
    next_e = jnp.min(later, axis=1)
    next_e = jnp.where(next_e < N_EXPERTS, next_e, -1)
    sched = (e, tile, lo, hi, run, next_e)
    return starts.astype(jnp.int32), tuple(a.astype(jnp.int32) for a in sched)


def _moe(x, mods, layer, norm_g, router_w, router_bias, w_gate, w_up, w_down,
         ws_gate, ws_up, ws_down, next_norm_g, final):
    h, eidx, rank, wts, counts = _route(x, norm_g, mods, layer, router_w, router_bias)
    starts, sched = _visit_schedule(counts.reshape(N_EXPERTS).astype(jnp.int32))
    eidx, rank = eidx[:TOP_K], rank[:TOP_K]
    expert_ids = jnp.arange(N_EXPERTS, dtype=jnp.int32)[:, None, None]
    pos = rank + jnp.sum(jnp.where(eidx[None] == expert_ids, starts[:, None, None], 0), axis=0)
    xs = _dispatch(pos, h)
    ys = _gmm(sched, xs, layer, w_gate, w_up, w_down)
    shared = _shared_expert(h, layer, ws_gate, ws_up, ws_down)
    return _combine(pos, x, shared, wts.T, mods, layer, next_norm_g, ys, final)


def _pool_diff_layer(x, h, mods, layer, j, cache_k, cache_v, cos, sin,
                     diff_w_in, pool_w, pool_scale, lq1, lk1, lq2, lk2, subln_g, w_out):
    lam_init = 0.8 - 0.6 * math.exp(-0.3 * layer)
    proj = _matmul([h], diff_w_in[j], [0], POOL_W + 3 * DIFF_W, F32, tn=1024, name="diff_in_proj")
    y_pool = _pool(proj, pool_w[j], pool_scale[j])
    lam_params = jnp.stack([lq1[j], lk1[j], lq2[j], lk2[j]])
    ck = cache_k[:, j].reshape(N_LAT_SEQ * PAST_LEN, DIFF_W)
    cv = cache_v[:, j].reshape(N_LAT_SEQ * PAST_LEN, DIFF_W)
    o = _diff_attn(proj, ck, cv, lam_params, subln_g[j], cos, sin, lam_init)
    x = _matmul([y_pool, o], w_out[j], [0, 1], D, F32, tn=1024, epilogue="residual",
                residual=x, mods=mods, layer=layer, gate_chunk=2, name="diff_out_proj")
    k_new = proj[:T_PROMPT, POOL_W + DIFF_W:POOL_W + 2 * DIFF_W]
    v_new = proj[:T_PROMPT, POOL_W + 2 * DIFF_W:]
    shape = (N_PROMPT_SEQ, PROMPT_LEN, DIFF_HEADS, 2 * DIFF_DH)
    return x, k_new.reshape(shape), v_new.reshape(shape)


def _mla_layer(x, h, mods, layer, j, cache_ckv, cache_kpe, cos, sin,
               w_dq, q_norm_g, w_uq, w_dkv, kv_norm_g, w_ukv, w_o):
    cq = _matmul([h], w_dq[j], [0], Q_LORA, BF16, epilogue="rms", gain=q_norm_g[j], name="mla_dq")
    ckv = _matmul([h], w_dkv[j], [0], KV_LORA, F32, epilogue="rms", gain=kv_norm_g[j], name="mla_dkv")
    w_kpe = w_dkv[j][:, KV_LORA:]
    kp_dup = _matmul([h], jnp.concatenate([w_kpe, w_kpe], axis=1), [0], LANES, F32, name="mla_kpe")
    w_uq3 = w_uq[j].reshape(Q_LORA, MLA_HEADS, QK_NOPE + QK_ROPE)
    w_uq_nope = w_uq3[:, :, :QK_NOPE].reshape(Q_LORA, MLA_HEADS * QK_NOPE)
    w_uq_pe = w_uq3[:, :, QK_NOPE:].reshape(Q_LORA, MLA_HEADS * QK_ROPE)
    q_nope = _matmul([cq], w_uq_nope, [0], MLA_HEADS * QK_NOPE, BF16, tn=1024, name="mla_uq_nope")
    q_pe = _matmul([cq], w_uq_pe, [0], MLA_HEADS * QK_ROPE, F32, tn=1024, name="mla_uq_pe")
    n_kv = MLA_HEADS * (QK_NOPE + V_DIM)
    kv = _matmul([ckv], w_ukv[j], [0], n_kv, BF16, tn=1024, name="mla_ukv")
    ckv_ctx = cache_ckv[:, j].reshape(N_LAT_SEQ * PAST_LEN, KV_LORA)
    kv_ctx = _matmul([ckv_ctx], w_ukv[j], [0], n_kv, BF16, tn=1024, name="mla_ukv_ctx")
    kpe_ctx = cache_kpe[:, j].reshape(N_LAT_SEQ * PAST_LEN, QK_ROPE)
    kp_ctx_dup = jnp.concatenate([kpe_ctx, kpe_ctx], axis=1)
    o = _mla_attn(q_nope, q_pe, kv, kp_dup, kv_ctx, kp_ctx_dup, cos, sin)
    x = _matmul([o], w_o[j], [0], D, F32, tn=1024, epilogue="residual",
                residual=x, mods=mods, layer=layer, gate_chunk=2, name="mla_out_proj")
    new_ckv = ckv[:T_PROMPT].reshape(N_PROMPT_SEQ, PROMPT_LEN, KV_LORA)
    new_kpe = kp_dup[:T_PROMPT, :QK_ROPE].reshape(N_PROMPT_SEQ, PROMPT_LEN, QK_ROPE)
    return x, new_ckv, new_kpe


def kernel(x_prompt, x_sample, cache_diff_k, cache_diff_v, cache_mla_ckv, cache_mla_kpe, c, c_ctx,
           ada_w, ada_b, norm_mix_g, norm_ffn_g, final_norm_g,
           diff_w_in, pool_w, pool_scale, diff_lambda_q1, diff_lambda_k1, diff_lambda_q2, diff_lambda_k2,
           diff_subln_g, even_w_out,
           mla_w_dq, mla_q_norm_g, mla_w_uq, mla_w_dkv, mla_kv_norm_g, mla_w_ukv, mla_w_o,
           router_w, router_bias, expert_w_gate, expert_w_up, expert_w_down,
           shared_w_gate, shared_w_up, shared_w_down):
    depth = ada_w.shape[0]
    x = (x_prompt.reshape(T_PROMPT, D), x_sample.reshape(T_LAT, D))
    cond = jnp.concatenate(
        [c_ctx[None, :], c, jnp.zeros((N_GROUPS_PAD - 1 - N_LAT_SEQ, D), F32)], axis=0)
    mods = _ada_params(cond, ada_w, ada_b)
    cos, sin = _rope_tables()

    new_dk, new_dv, new_ckv, new_kpe = [], [], [], []
    h = _norm_mod(x, norm_mix_g[0], mods, 0, 0, 1)
    for i in range(depth):
        j = i // 2
        last = i == depth - 1
        if i % 2 == 0:
            x, k_new, v_new = _pool_diff_layer(
                x, h, mods, i, j, cache_diff_k, cache_diff_v, cos, sin,
                diff_w_in, pool_w, pool_scale, diff_lambda_q1, diff_lambda_k1, diff_lambda_q2,
                diff_lambda_k2, diff_subln_g, even_w_out)
            new_dk.append(k_new)
            new_dv.append(v_new)
        else:
            x, ckv, kpe = _mla_layer(
                x, h, mods, i, j, cache_mla_ckv, cache_mla_kpe, cos, sin,
                mla_w_dq, mla_q_norm_g, mla_w_uq, mla_w_dkv, mla_kv_norm_g, mla_w_ukv, mla_w_o)
            new_ckv.append(ckv)
            new_kpe.append(kpe)
        outs = _moe(x, mods, i, norm_ffn_g[i], router_w[i], router_bias[i],
                    expert_w_gate, expert_w_up, expert_w_down,
                    shared_w_gate, shared_w_up, shared_w_down,
                    final_norm_g if last else norm_mix_g[i + 1], final=last)
        if not last:
            x, h = outs
    y_prompt = outs[0].reshape(N_PROMPT_SEQ, PROMPT_LEN, D)
    y_sample = outs[1].reshape(N_LAT_SEQ, LAT_LEN, D)
    return (y_prompt, y_sample, jnp.stack(new_dk, axis=1), jnp.stack(new_dv, axis=1),
            jnp.stack(new_ckv, axis=1), jnp.stack(new_kpe, axis=1))
```

```python
import functools
import math

import jax
import jax.numpy as jnp
from jax import lax
from jax.experimental import pallas as pl
from jax.experimental.pallas import tpu as pltpu

F32 = jnp.float32
BF16 = jnp.bfloat16

D = 2048
N_PROMPT_SEQ = 16
PROMPT_LEN = 256
N_LAT_SEQ = 4
LAT_LEN = 1024
PAST_LEN = 256
T_PROMPT = N_PROMPT_SEQ * PROMPT_LEN
T_LAT = N_LAT_SEQ * LAT_LEN
T = T_PROMPT + T_LAT
UNIT = 1024
N_UNITS = T // UNIT
N_PROMPT_UNITS = T_PROMPT // UNIT
SEQ_PER_PROMPT_UNIT = UNIT // PROMPT_LEN
N_GROUPS_PAD = 8

GRID_W = 64
ROPE_BASE = 10000.0
EPS = 1e-6
POOL_W = 1024
POOL_WINDOWS = (2, 4, 8, 16)
POOL_G = 256
POOL_HALO = 8
DIFF_W = 1024
DIFF_DH = 64
DIFF_HEADS = 8
DIFF_SCALE = DIFF_DH ** -0.5
MLA_HEADS = 16
Q_LORA = 512
KV_LORA = 512
QK_NOPE = 128
QK_ROPE = 64
V_DIM = 128
MLA_SCALE = (QK_NOPE + QK_ROPE) ** -0.5
N_EXPERTS = 64
TOP_K = 6
N_EGROUPS = 8
EGROUP = N_EXPERTS // N_EGROUPS
TOPK_GROUPS = 4
D_EXPERT = 512
ROUTED_SCALE = 2.5

LANES = 128
ATT_QB = 256
ROUTE_TM = 256
COMBINE_TM = 256
GMM_TM = 256
ROW_UNROLL = 4
N_PAIRS = T * TOP_K
N_ROW_TILES = N_PAIRS // GMM_TM
N_VISITS = N_ROW_TILES + N_EXPERTS
VMEM_LIMIT = 56 * 1024 * 1024

_NT = (((1,), (1,)), ((), ()))
_NN = (((1,), (0,)), ((), ()))


def _cparams(n_axes, vmem=VMEM_LIMIT):
    return pltpu.CompilerParams(dimension_semantics=("arbitrary",) * n_axes, vmem_limit_bytes=vmem)


def _group_of_tile(i, tm):
    n_p = T_PROMPT // tm
    per_seq = LAT_LEN // tm
    return jnp.where(i < n_p, 0, 1 + (i - n_p) // per_seq)


def _sigmoid(x):
    return 1.0 / (1.0 + jnp.exp(-x))


def _rms(x, g):
    return x * lax.rsqrt(jnp.mean(x * x, axis=-1, keepdims=True) + EPS) * g


def _dot_3pass(a, b, dims):
    a_hi = a.astype(BF16)
    a_lo = (a - a_hi.astype(F32)).astype(BF16)
    b_hi = b.astype(BF16)
    b_lo = (b - b_hi.astype(F32)).astype(BF16)

    def dot(p, q):
        return lax.dot_general(p, q, dims, preferred_element_type=F32)

    return dot(a_hi, b_hi) + (dot(a_hi, b_lo) + dot(a_lo, b_hi))


def _ada_kernel(c_ref, w_ref, b_ref, o_ref):
    c = c_ref[...]
    s = c * _sigmoid(c)
    o_ref[...] = _dot_3pass(s, w_ref[...], _NN) + b_ref[...]


def _ada_params(cond, ada_w, ada_b):
    depth, _, n = ada_w.shape
    tn = 1024
    out = pl.pallas_call(
        _ada_kernel,
        grid=(depth, n // tn),
        in_specs=[
            pl.BlockSpec((N_GROUPS_PAD, D), lambda l, j: (0, 0)),
            pl.BlockSpec((None, D, tn), lambda l, j: (l, 0, j)),
            pl.BlockSpec((None, 1, tn), lambda l, j: (l, 0, j)),
        ],
        out_specs=pl.BlockSpec((None, N_GROUPS_PAD, tn), lambda l, j: (l, 0, j)),
        out_shape=jax.ShapeDtypeStruct((depth, N_GROUPS_PAD, n), F32),
        compiler_params=_cparams(2),
        name="ada_params",
    )(cond, ada_w, ada_b.reshape(depth, 1, n))
    return out.reshape(depth, N_GROUPS_PAD, 1, n)


def _mod_spec(layer, chunk, tm):
    return pl.BlockSpec((None, None, 1, D), lambda i: (layer, _group_of_tile(i, tm), 0, chunk))


def _row_source(x):
    return list(x) if isinstance(x, (tuple, list)) else [x]


def _row_source_specs(parts, tm, width, col_of, row_axis):
    if len(parts) == 1:
        return [pl.BlockSpec((tm, width), lambda *ids: (ids[row_axis], col_of(*ids)))]
    n_p = T_PROMPT // tm
    return [pl.BlockSpec((tm, width), lambda *ids: (jnp.minimum(ids[row_axis], n_p - 1), col_of(*ids))),
            pl.BlockSpec((tm, width), lambda *ids: (jnp.maximum(ids[row_axis] - n_p, 0), col_of(*ids)))]


def _read_rows(refs, row_tile, tm):
    if len(refs) == 1:
        return refs[0][...]
    return jnp.where(row_tile < T_PROMPT // tm, refs[0][...], refs[1][...])


def _norm_mod_kernel(*refs, n_x, tm):
    x_refs = refs[:n_x]
    g_ref, sh_ref, sc_ref, o_ref = refs[n_x:]
    y = _rms(_read_rows(x_refs, pl.program_id(0), tm), g_ref[...])
    o_ref[...] = (y * (1.0 + sc_ref[...]) + sh_ref[...]).astype(o_ref.dtype)


def _norm_mod(x, g, mods, layer, shift_chunk, scale_chunk):
    tm = 512
    parts = _row_source(x)
    return pl.pallas_call(
        functools.partial(_norm_mod_kernel, n_x=len(parts), tm=tm),
        grid=(T // tm,),
        in_specs=_row_source_specs(parts, tm, D, lambda i: 0, 0) + [
            pl.BlockSpec((1, D), lambda i: (0, 0)),
            _mod_spec(layer, shift_chunk, tm),
            _mod_spec(layer, scale_chunk, tm),
        ],
        out_specs=pl.BlockSpec((tm, D), lambda i: (i, 0)),
        out_shape=jax.ShapeDtypeStruct((T, D), BF16),
        compiler_params=_cparams(1),
        name="norm_mod",
    )(*parts, g.reshape(1, D), mods, mods)


def _mm_kernel(*refs, n_a, epilogue, n_res, tm):
    a_refs = refs[:n_a]
    w_refs = refs[n_a:2 * n_a]
    pos = 2 * n_a
    if epilogue == "rms":
        g_ref = refs[pos]
        pos += 1
    elif epilogue == "residual":
        res_refs, gate_ref = refs[pos:pos + n_res], refs[pos + n_res]
        pos += n_res + 1
    o_ref = refs[pos]
    w_scr = refs[pos + 1:]

    @pl.when(pl.program_id(1) == 0)
    def _():
        for w_ref, scr in zip(w_refs, w_scr):
            scr[...] = w_ref[...].astype(BF16)

    acc = None
    for a_ref, scr in zip(a_refs, w_scr):
        part = jnp.dot(a_ref[...].astype(BF16), scr[...], preferred_element_type=F32)
        acc = part if acc is None else acc + part
    if epilogue == "rms":
        acc = _rms(acc, g_ref[...])
    elif epilogue == "residual":
        acc = _read_rows(res_refs, pl.program_id(1), tm) + gate_ref[...] * acc
    o_ref[...] = acc.astype(o_ref.dtype)


def _matmul(a_list, w, w_row_blocks, n_out, out_dtype, *, tm=512, tn=None, epilogue=None,
            gain=None, residual=None, mods=None, layer=None, gate_chunk=None, name="matmul"):
    m = a_list[0].shape[0]
    tn = n_out if tn is None else tn
    n_a = len(a_list)
    in_specs, args = [], []
    for a in a_list:
        k = a.shape[1]
        in_specs.append(pl.BlockSpec((tm, k), lambda j, i: (i, 0)))
        args.append(a)
    scratch = []
    for a, rb in zip(a_list, w_row_blocks):
        k = a.shape[1]
        in_specs.append(pl.BlockSpec((k, tn), lambda j, i, rb=rb: (rb, j)))
        args.append(w)
        scratch.append(pltpu.VMEM((k, tn), BF16))
    if epilogue == "rms":
        assert tn == n_out
        in_specs.append(pl.BlockSpec((1, tn), lambda j, i: (0, 0)))
        args.append(gain.reshape(1, n_out))
    elif epilogue == "residual":
        res_parts = _row_source(residual)
        in_specs += _row_source_specs(res_parts, tm, tn, lambda j, i: j, 1)
        args += res_parts
        in_specs.append(pl.BlockSpec(
            (None, None, 1, tn),
            lambda j, i: (layer, _group_of_tile(i, tm), 0, gate_chunk * (D // tn) + j)))
        args.append(mods)
    return pl.pallas_call(
        functools.partial(_mm_kernel, n_a=n_a, epilogue=epilogue, tm=tm,
                          n_res=len(_row_source(residual)) if epilogue == "residual" else 0),
        grid=(n_out // tn, m // tm),
        in_specs=in_specs,
        out_specs=pl.BlockSpec((tm, tn), lambda j, i: (i, j)),
        out_shape=jax.ShapeDtypeStruct((m, n_out), out_dtype),
        scratch_shapes=scratch,
        compiler_params=_cparams(2),
        name=name,
    )(*args)


def _pool_seq(u_ref, row0, seq_len, pw_ref, scale_ref, o_ref, pad_ref):
    t = lax.broadcasted_iota(jnp.int32, (seq_len, 1), 0)
    zeros = jnp.zeros((POOL_HALO, POOL_G), F32)
    for g, w in enumerate(POOL_WINDOWS):
        cols = slice(g * POOL_G, (g + 1) * POOL_G)
        ug = u_ref[row0:row0 + seq_len, cols]
        pad_ref[0:POOL_HALO, :] = zeros
        pad_ref[POOL_HALO:POOL_HALO + seq_len, :] = ug
        pad_ref[POOL_HALO + seq_len:2 * POOL_HALO + seq_len, :] = zeros
        total = None
        for off in range(-(w // 2), w // 2):
            part = pad_ref[POOL_HALO + off:POOL_HALO + off + seq_len, :]
            total = part if total is None else total + part
        cnt = (jnp.minimum(t + w // 2, seq_len) - jnp.maximum(t - w // 2, 0)).astype(F32)
        d = (total / cnt - ug).astype(BF16)
        y = jnp.dot(d, pw_ref[g].astype(BF16), preferred_element_type=F32)
        o_ref[row0:row0 + seq_len, cols] = (y * scale_ref[:, cols]).astype(o_ref.dtype)


def _pool_kernel(u_ref, pw_ref, scale_ref, o_ref, pad_ref):
    unit = pl.program_id(0)

    @pl.when(unit < N_PROMPT_UNITS)
    def _():
        for s in range(SEQ_PER_PROMPT_UNIT):
            _pool_seq(u_ref, s * PROMPT_LEN, PROMPT_LEN, pw_ref, scale_ref, o_ref, pad_ref)

    @pl.when(unit >= N_PROMPT_UNITS)
    def _():
        _pool_seq(u_ref, 0, LAT_LEN, pw_ref, scale_ref, o_ref, pad_ref)


def _pool(proj, pool_w, pool_scale):
    return pl.pallas_call(
        _pool_kernel,
        grid=(N_UNITS,),
        in_specs=[
            pl.BlockSpec((UNIT, POOL_W), lambda u: (u, 0)),
            pl.BlockSpec((len(POOL_WINDOWS), POOL_G, POOL_G), lambda u: (0, 0, 0)),
            pl.BlockSpec((1, POOL_W), lambda u: (0, 0)),
        ],
        out_specs=pl.BlockSpec((UNIT, POOL_W), lambda u: (u, 0)),
        out_shape=jax.ShapeDtypeStruct((T, POOL_W), BF16),
        scratch_shapes=[pltpu.VMEM((LAT_LEN + 2 * POOL_HALO, POOL_G), F32)],
        compiler_params=_cparams(1),
        name="pool",
    )(proj, pool_w, pool_scale.reshape(1, POOL_W))


def _rope_tables():
    pos = jnp.arange(LAT_LEN)
    row = (pos // GRID_W).astype(F32)
    col = (pos % GRID_W).astype(F32)
    quarter = QK_ROPE // 4
    inv = ROPE_BASE ** (-jnp.arange(quarter, dtype=F32) / quarter)
    lane = jnp.arange(LANES)
    axis = (lane % QK_ROPE) // (QK_ROPE // 2)
    freq = inv[lane % quarter]
    p = jnp.where(axis[None, :] == 0, row[:, None], col[:, None])
    ang = p * freq[None, :]
    sign = jnp.where((lane % (QK_ROPE // 2)) < quarter, -1.0, 1.0).astype(F32)
    return jnp.cos(ang), jnp.sin(ang) * sign[None, :]


def _rope(x, cos, sin_signed):
    quarter = QK_ROPE // 4
    lane = lax.broadcasted_iota(jnp.int32, x.shape, 1)
    first_half = (lane % (QK_ROPE // 2)) < quarter
    partner = jnp.where(first_half, pltpu.roll(x, LANES - quarter, 1), pltpu.roll(x, quarter, 1))
    return x * cos + partner * sin_signed


def _diff_core(q, k, v, lam, g, lam_init):
    lane = lax.broadcasted_iota(jnp.int32, q.shape, 1)
    low = lane < DIFF_DH
    q1 = jnp.where(low, q, 0.0).astype(BF16)
    q2 = jnp.where(low, 0.0, q).astype(BF16)

    def probs(qm):
        s = lax.dot_general(qm, k, _NT, preferred_element_type=F32)
        e = jnp.exp(s - jnp.max(s, axis=-1, keepdims=True))
        return e * (1.0 / jnp.sum(e, axis=-1, keepdims=True))

    w = (probs(q1) - lam * probs(q2)).astype(BF16)
    o = jnp.dot(w, v, preferred_element_type=F32)
    return _rms(o, g) * (1.0 - lam_init)


def _diff_attn_kernel(lp_ref, q_ref, k_ref, v_ref, ck_ref, cv_ref, cos_ref, sin_ref, g_ref, o_ref,
                      *, lam_init):
    unit = pl.program_id(0)
    lp = lp_ref[...]
    lam = (jnp.exp(jnp.sum(lp[0:1] * lp[1:2], axis=-1, keepdims=True))
           - jnp.exp(jnp.sum(lp[2:3] * lp[3:4], axis=-1, keepdims=True)) + lam_init)
    g = g_ref[...]

    @pl.when(unit < N_PROMPT_UNITS)
    def _():
        for s in range(SEQ_PER_PROMPT_UNIT):
            rows = slice(s * PROMPT_LEN, (s + 1) * PROMPT_LEN)
            q = q_ref[rows, :] * DIFF_SCALE
            k = k_ref[rows, :].astype(BF16)
            v = v_ref[rows, :].astype(BF16)
            o_ref[rows, :] = _diff_core(q, k, v, lam, g, lam_init).astype(o_ref.dtype)

    @pl.when(unit >= N_PROMPT_UNITS)
    def _():
        cos, sin = cos_ref[...], sin_ref[...]
        k = jnp.concatenate([ck_ref[...].astype(BF16),
                             _rope(k_ref[...], cos, sin).astype(BF16)], axis=0)
        v = jnp.concatenate([cv_ref[...].astype(BF16), v_ref[...].astype(BF16)], axis=0)
        for b in range(LAT_LEN // ATT_QB):
            rows = slice(b * ATT_QB, (b + 1) * ATT_QB)
            q = _rope(q_ref[rows, :], cos[rows], sin[rows]) * DIFF_SCALE
            o_ref[rows, :] = _diff_core(q, k, v, lam, g, lam_init).astype(o_ref.dtype)


def _diff_attn(proj, cache_k, cache_v, lam_params, subln_g, cos, sin, lam_init):
    hd = 2 * DIFF_DH
    q0, k0, v0 = POOL_W // hd, (POOL_W + DIFF_W) // hd, (POOL_W + 2 * DIFF_W) // hd

    def ctx_map(u, h):
        return (jnp.maximum(u - N_PROMPT_UNITS, 0), h)

    return pl.pallas_call(
        functools.partial(_diff_attn_kernel, lam_init=lam_init),
        grid=(N_UNITS, DIFF_HEADS),
        in_specs=[
            pl.BlockSpec((4, DIFF_DH), lambda u, h: (0, 0)),
            pl.BlockSpec((UNIT, hd), lambda u, h: (u, q0 + h)),
            pl.BlockSpec((UNIT, hd), lambda u, h: (u, k0 + h)),
            pl.BlockSpec((UNIT, hd), lambda u, h: (u, v0 + h)),
            pl.BlockSpec((PAST_LEN, hd), ctx_map),
            pl.BlockSpec((PAST_LEN, hd), ctx_map),
            pl.BlockSpec((LAT_LEN, LANES), lambda u, h: (0, 0)),
            pl.BlockSpec((LAT_LEN, LANES), lambda u, h: (0, 0)),
            pl.BlockSpec((1, hd), lambda u, h: (0, 0)),
        ],
        out_specs=pl.BlockSpec((UNIT, hd), lambda u, h: (u, h)),
        out_shape=jax.ShapeDtypeStruct((T, DIFF_W), BF16),
        compiler_params=_cparams(2),
        name="diff_attn",
    )(lam_params, proj, proj, proj, cache_k, cache_v, cos, sin, subln_g.reshape(1, hd))


def _mla_core(qc, kc, v):
    s = lax.dot_general(qc, kc, _NT, preferred_element_type=F32) * MLA_SCALE
    e = jnp.exp(s - jnp.max(s, axis=-1, keepdims=True))
    p = (e * (1.0 / jnp.sum(e, axis=-1, keepdims=True))).astype(BF16)
    return jnp.dot(p, v, preferred_element_type=F32)


def _mla_attn_kernel(qn_ref, qp_ref, kn_ref, v_ref, kp_ref, ckn_ref, cv_ref, ckp_ref, cos_ref, sin_ref,
                     o_ref):
    unit = pl.program_id(0)
    head = pl.program_id(1)
    lane = lax.broadcasted_iota(jnp.int32, (ATT_QB, LANES), 1)
    mine = (lane // QK_ROPE) == (head % 2)

    @pl.when(unit < N_PROMPT_UNITS)
    def _():
        for s in range(SEQ_PER_PROMPT_UNIT):
            rows = slice(s * PROMPT_LEN, (s + 1) * PROMPT_LEN)
            qp = jnp.where(mine, qp_ref[rows, :], 0.0).astype(BF16)
            qc = jnp.concatenate([qn_ref[rows, :], qp], axis=1)
            kc = jnp.concatenate([kn_ref[rows, :], kp_ref[rows, :].astype(BF16)], axis=1)
            o_ref[rows, :] = _mla_core(qc, kc, v_ref[rows, :]).astype(o_ref.dtype)

    @pl.when(unit >= N_PROMPT_UNITS)
    def _():
        cos, sin = cos_ref[...], sin_ref[...]
        k_ctx = jnp.concatenate([ckn_ref[...], ckp_ref[...].astype(BF16)], axis=1)
        k_new = jnp.concatenate([kn_ref[...], _rope(kp_ref[...], cos, sin).astype(BF16)], axis=1)
        kc = jnp.concatenate([k_ctx, k_new], axis=0)
        v = jnp.concatenate([cv_ref[...], v_ref[...]], axis=0)
        for b in range(LAT_LEN // ATT_QB):
            rows = slice(b * ATT_QB, (b + 1) * ATT_QB)
            qp = jnp.where(mine, _rope(qp_ref[rows, :], cos[rows], sin[rows]), 0.0).astype(BF16)
            qc = jnp.concatenate([qn_ref[rows, :], qp], axis=1)
            o_ref[rows, :] = _mla_core(qc, kc, v).astype(o_ref.dtype)


def _mla_attn(q_nope, q_pe, kv, kp_dup, kv_ctx, kp_ctx_dup, cos, sin):
    def ctx_row(u):
        return jnp.maximum(u - N_PROMPT_UNITS, 0)

    return pl.pallas_call(
        _mla_attn_kernel,
        grid=(N_UNITS, MLA_HEADS),
        in_specs=[
            pl.BlockSpec((UNIT, QK_NOPE), lambda u, h: (u, h)),
            pl.BlockSpec((UNIT, LANES), lambda u, h: (u, h // 2)),
            pl.BlockSpec((UNIT, QK_NOPE), lambda u, h: (u, 2 * h)),
            pl.BlockSpec((UNIT, V_DIM), lambda u, h: (u, 2 * h + 1)),
            pl.BlockSpec((UNIT, LANES), lambda u, h: (u, 0)),
            pl.BlockSpec((PAST_LEN, QK_NOPE), lambda u, h: (ctx_row(u), 2 * h)),
            pl.BlockSpec((PAST_LEN, V_DIM), lambda u, h: (ctx_row(u), 2 * h + 1)),
            pl.BlockSpec((PAST_LEN, LANES), lambda u, h: (ctx_row(u), 0)),
            pl.BlockSpec((LAT_LEN, LANES), lambda u, h: (0, 0)),
            pl.BlockSpec((LAT_LEN, LANES), lambda u, h: (0, 0)),
        ],
        out_specs=pl.BlockSpec((UNIT, V_DIM), lambda u, h: (u, h)),
        out_shape=jax.ShapeDtypeStruct((T, MLA_HEADS * V_DIM), BF16),
        compiler_params=_cparams(2),
        name="mla_attn",
    )(q_nope, q_pe, kv, kv, kp_dup, kv_ctx, kv_ctx, kp_ctx_dup, cos, sin)


def _route_kernel(x_ref, g_ref, sh_ref, sc_ref, rw_ref, rb_ref,
                  h_ref, eidx_ref, rank_ref, wts_ref, cnt_ref, carry_ref):
    tm = ROUTE_TM

    @pl.when(pl.program_id(0) == 0)
    def _():
        carry_ref[...] = jnp.zeros_like(carry_ref)

    h = _rms(x_ref[...], g_ref[...]) * (1.0 + sc_ref[...]) + sh_ref[...]
    h_ref[...] = h
    logits = _dot_3pass(rw_ref[...], h, _NT)
    scores = _sigmoid(logits)
    choice = scores + rb_ref[...]
    neg = -jnp.inf

    def take_max(vals, iota, n):
        m = jnp.max(vals, axis=0, keepdims=True)
        idx = jnp.min(jnp.where(vals == m, iota, n), axis=0, keepdims=True)
        return iota == idx, m, idx

    iota8 = lax.broadcasted_iota(jnp.int32, (8, tm), 0)

    def stack_rows(rows, dtype):
        out = jnp.zeros((8, tm), dtype)
        for k, r in enumerate(rows):
            out = jnp.where(iota8 == k, r.astype(dtype), out)
        return out

    gscore = []
    for gi in range(N_EGROUPS):
        grp = choice[gi * EGROUP:(gi + 1) * EGROUP]
        oh, m1, _ = take_max(grp, iota8, EGROUP)
        m2 = jnp.max(jnp.where(oh, neg, grp), axis=0, keepdims=True)
        gscore.append(m1 + m2)
    cur = stack_rows(gscore, F32)
    gsel = jnp.zeros((N_EGROUPS, tm), F32)
    for _ in range(TOPK_GROUPS):
        oh, _, _ = take_max(cur, iota8, N_EGROUPS)
        gsel = jnp.where(oh, 1.0, gsel)
        cur = jnp.where(oh, neg, cur)
    emask = jnp.concatenate(
        [jnp.broadcast_to(gsel[gi:gi + 1], (EGROUP, tm)) for gi in range(N_EGROUPS)], axis=0)
    masked = jnp.where(emask > 0.0, choice, neg)

    iota_e = lax.broadcasted_iota(jnp.int32, (N_EXPERTS, tm), 0)
    onehots, idxs, wsel = [], [], []
    sel = jnp.zeros((N_EXPERTS, tm), F32)
    for _ in range(TOP_K):
        oh, _, idx = take_max(masked, iota_e, N_EXPERTS)
        onehots.append(oh)
        idxs.append(idx)
        wsel.append(jnp.sum(jnp.where(oh, scores, 0.0), axis=0, keepdims=True))
        masked = jnp.where(oh, neg, masked)
        sel = jnp.where(oh, 1.0, sel)
    wsum = wsel[0]
    for w in wsel[1:]:
        wsum = wsum + w

    before = jnp.where(lax.broadcasted_iota(jnp.int32, (tm, tm), 0)
                       < lax.broadcasted_iota(jnp.int32, (tm, tm), 1), 1.0, 0.0).astype(BF16)
    rank_all = carry_ref[...] + jnp.dot(sel.astype(BF16), before, preferred_element_type=F32)
    carry_ref[...] = carry_ref[...] + jnp.sum(sel, axis=1, keepdims=True)
    cnt_ref[...] = carry_ref[...]

    ranks = [jnp.sum(jnp.where(oh, rank_all, 0.0), axis=0, keepdims=True) for oh in onehots]
    eidx_ref[...] = stack_rows(idxs, jnp.int32)
    rank_ref[...] = stack_rows(ranks, jnp.int32)
    wts_ref[...] = stack_rows([w / wsum * ROUTED_SCALE for w in wsel], F32)


def _route(x, g, mods, layer, router_w, router_bias):
    tm = ROUTE_TM
    return pl.pallas_call(
        _route_kernel,
        grid=(T // tm,),
        in_specs=[
            pl.BlockSpec((tm, D), lambda i: (i, 0)),
            pl.BlockSpec((1, D), lambda i: (0, 0)),
            _mod_spec(layer, 3, tm),
            _mod_spec(layer, 4, tm),
            pl.BlockSpec((N_EXPERTS, D), lambda i: (0, 0)),
            pl.BlockSpec((N_EXPERTS, 1), lambda i: (0, 0)),
        ],
        out_specs=[
            pl.BlockSpec((tm, D), lambda i: (i, 0)),
            pl.BlockSpec((8, tm), lambda i: (0, i)),
            pl.BlockSpec((8, tm), lambda i: (0, i)),
            pl.BlockSpec((8, tm), lambda i: (0, i)),
            pl.BlockSpec((N_EXPERTS, 1), lambda i: (0, 0)),
        ],
        out_shape=[
            jax.ShapeDtypeStruct((T, D), F32),
            jax.ShapeDtypeStruct((8, T), jnp.int32),
            jax.ShapeDtypeStruct((8, T), jnp.int32),
            jax.ShapeDtypeStruct((8, T), F32),
            jax.ShapeDtypeStruct((N_EXPERTS, 1), F32),
        ],
        scratch_shapes=[pltpu.VMEM((N_EXPERTS, 1), F32)],
        compiler_params=_cparams(1),
        name="route",
    )(x, g.reshape(1, D), mods, mods, router_w.T, router_bias.reshape(N_EXPERTS, 1))


def _row_copy(src, src_row, dst, dst_row, sem):
    return pltpu.make_async_copy(src.at[pl.ds(src_row, 1), :], dst.at[pl.ds(dst_row, 1), :], sem)


def _tile_positions(pos, tm):
    return pos.reshape(TOP_K, T // tm, tm).transpose(1, 0, 2).reshape(T // tm, 1, TOP_K * tm)


def _dispatch_kernel(pos_ref, h_ref, xs_ref, sem):
    tm = ROUTE_TM

    def issue(j, carry):
        for k in range(TOP_K):
            _row_copy(h_ref, j, xs_ref, pos_ref[0, 0, k * tm + j], sem).start(priority=k % 2)
        return carry

    def drain(j, carry):
        for _ in range(TOP_K):
            _row_copy(h_ref, 0, xs_ref, 0, sem).wait()
        return carry

    lax.fori_loop(0, tm, issue, 0, unroll=ROW_UNROLL)
    lax.fori_loop(0, tm, drain, 0, unroll=ROW_UNROLL)


def _dispatch(pos, h):
    tm = ROUTE_TM
    return pl.pallas_call(
        _dispatch_kernel,
        grid=(T // tm,),
        in_specs=[
            pl.BlockSpec((1, 1, TOP_K * tm), lambda i: (i, 0, 0), memory_space=pltpu.SMEM),
            pl.BlockSpec((tm, D), lambda i: (i, 0)),
        ],
        out_specs=pl.BlockSpec(memory_space=pl.ANY),
        out_shape=jax.ShapeDtypeStruct((N_PAIRS, D), F32),
        scratch_shapes=[pltpu.SemaphoreType.DMA(())],
        compiler_params=_cparams(1),
        name="dispatch",
    )(_tile_positions(pos, tm), h)


def _combine_kernel(pos_ref, pos_next_ref, x_ref, sh_ref, w_ref, gate_ref, ng_ref, nsh_ref, nsc_ref,
                    ys_ref, *rest, final):
    out_refs, (buf_ref, sem) = rest[:-2], rest[-2:]
    tm = COMBINE_TM
    i = pl.program_id(0)
    n = pl.num_programs(0)
    slot = i % 2

    def start_gathers(p_ref, dst_slot):
        def issue(j, carry):
            for k in range(TOP_K):
                _row_copy(ys_ref, p_ref[0, 0, k * tm + j], buf_ref.at[dst_slot, k], j,
                          sem.at[dst_slot]).start(priority=k % 2)
            return carry
        lax.fori_loop(0, tm, issue, 0, unroll=ROW_UNROLL)

    @pl.when(i == 0)
    def _():
        start_gathers(pos_ref, 0)

    @pl.when(i + 1 < n)
    def _():
        start_gathers(pos_next_ref, 1 - slot)

    def drain(j, carry):
        for k in range(TOP_K):
            _row_copy(ys_ref, 0, buf_ref.at[slot, k], 0, sem.at[slot]).wait()
        return carry

    lax.fori_loop(0, tm, drain, 0, unroll=ROW_UNROLL)
    w = w_ref[...]
    acc = sh_ref[...]
    for k in range(TOP_K):
        acc = acc + w[:, k:k + 1] * buf_ref[slot, k]
    xn = x_ref[...] + gate_ref[...] * acc
    normed = _rms(xn, ng_ref[...])
    if not final:
        out_refs[0][...] = xn
        out_refs[1][...] = (normed * (1.0 + nsc_ref[...]) + nsh_ref[...]).astype(BF16)
    else:
        xn = normed

        @pl.when(i < T_PROMPT // tm)
        def _():
            out_refs[0][...] = xn

        @pl.when(i >= T_PROMPT // tm)
        def _():
            out_refs[1][...] = xn


def _combine(pos, x, shared, wts_t, mods, layer, next_norm_g, ys, final):
    tm = COMBINE_TM
    n_p = T_PROMPT // tm
    next_layer = layer if final else layer + 1
    if final:
        out_specs = [pl.BlockSpec((tm, D), lambda i: (jnp.minimum(i, n_p - 1), 0)),
                     pl.BlockSpec((tm, D), lambda i: (jnp.maximum(i - n_p, 0), 0))]
        out_shape = [jax.ShapeDtypeStruct((T_PROMPT, D), F32), jax.ShapeDtypeStruct((T_LAT, D), F32)]
    else:
        out_specs = [pl.BlockSpec((tm, D), lambda i: (i, 0)), pl.BlockSpec((tm, D), lambda i: (i, 0))]
        out_shape = [jax.ShapeDtypeStruct((T, D), F32), jax.ShapeDtypeStruct((T, D), BF16)]
    tiles = _tile_positions(pos, tm)
    return pl.pallas_call(
        functools.partial(_combine_kernel, final=final),
        grid=(T // tm,),
        in_specs=[
            pl.BlockSpec((1, 1, TOP_K * tm), lambda i: (i, 0, 0), memory_space=pltpu.SMEM),
            pl.BlockSpec((1, 1, TOP_K * tm), lambda i: (jnp.minimum(i + 1, T // tm - 1), 0, 0),
                         memory_space=pltpu.SMEM),
            pl.BlockSpec((tm, D), lambda i: (i, 0)),
            pl.BlockSpec((tm, D), lambda i: (i, 0)),
            pl.BlockSpec((tm, 8), lambda i: (i, 0)),
            pl.BlockSpec((None, None, 1, D), lambda i: (layer, _group_of_tile(i, tm), 0, 5)),
            pl.BlockSpec((1, D), lambda i: (0, 0)),
            _mod_spec(next_layer, 0, tm),
            _mod_spec(next_layer, 1, tm),
            pl.BlockSpec(memory_space=pl.ANY),
        ],
        out_specs=out_specs,
        out_shape=out_shape,
        scratch_shapes=[pltpu.VMEM((2, TOP_K, tm, D), F32), pltpu.SemaphoreType.DMA((2,))],
        compiler_params=_cparams(1),
        name="combine",
    )(tiles, tiles, x, shared, wts_t, mods, next_norm_g.reshape(1, D), mods, mods, ys)


def _swiglu(x, wgu_scr, wd_scr):
    gu = jnp.dot(x, wgu_scr[...], preferred_element_type=F32)
    gate, up = gu[:, :D_EXPERT], gu[:, D_EXPERT:]
    act = (gate * _sigmoid(gate) * up).astype(BF16)
    return jnp.dot(act, wd_scr[...], preferred_element_type=F32)


def _load_expert_weights(wg_ref, wu_ref, wd_ref, wgu_scr, wd_scr):
    wgu_scr[:, :D_EXPERT] = wg_ref[...].astype(BF16)
    wgu_scr[:, D_EXPERT:] = wu_ref[...].astype(BF16)
    wd_scr[...] = wd_ref[...].astype(BF16)


def _shared_kernel(h_ref, wg_ref, wu_ref, wd_ref, o_ref, wgu_scr, wd_scr):
    @pl.when(pl.program_id(0) == 0)
    def _():
        _load_expert_weights(wg_ref, wu_ref, wd_ref, wgu_scr, wd_scr)

    o_ref[...] = _swiglu(h_ref[...].astype(BF16), wgu_scr, wd_scr)


def _shared_expert(h, layer, w_gate, w_up, w_down):
    tm = 512
    return pl.pallas_call(
        _shared_kernel,
        grid=(T // tm,),
        in_specs=[
            pl.BlockSpec((tm, D), lambda i: (i, 0)),
            pl.BlockSpec((None, D, D_EXPERT), lambda i: (layer, 0, 0)),
            pl.BlockSpec((None, D, D_EXPERT), lambda i: (layer, 0, 0)),
            pl.BlockSpec((None, D_EXPERT, D), lambda i: (layer, 0, 0)),
        ],
        out_specs=pl.BlockSpec((tm, D), lambda i: (i, 0)),
        out_shape=jax.ShapeDtypeStruct((T, D), F32),
        scratch_shapes=[pltpu.VMEM((D, 2 * D_EXPERT), BF16), pltpu.VMEM((D_EXPERT, D), BF16)],
        compiler_params=_cparams(1),
        name="shared_expert",
    )(h, w_gate, w_up, w_down)


def _gmm_kernel(e_ref, t_ref, lo_ref, hi_ref, run_ref, next_e_ref, xs_ref, wg_hbm, wu_hbm, wd_hbm, ys_ref,
                wg_buf, wu_buf, wd_buf, wsem, wgu_scr, wd_scr, *, layer):
    v = pl.program_id(0)
    prev = jnp.maximum(v - 1, 0)
    new_expert = (v == 0) | (e_ref[v] != e_ref[prev])
    new_tile = (v == 0) | (t_ref[v] != t_ref[prev])
    lo, hi = lo_ref[v], hi_ref[v]
    slot = run_ref[v] % 2

    def weight_copies(expert, dst_slot):
        return [pltpu.make_async_copy(src.at[layer, expert], dst.at[dst_slot], wsem.at[dst_slot])
                for src, dst in ((wg_hbm, wg_buf), (wu_hbm, wu_buf), (wd_hbm, wd_buf))]

    @pl.when(v == 0)
    def _():
        for c in weight_copies(e_ref[0], 0):
            c.start(priority=1)

    @pl.when(new_expert)
    def _():
        for c in weight_copies(e_ref[v], slot):
            c.wait()

        @pl.when(next_e_ref[v] >= 0)
        def _():
            for c in weight_copies(next_e_ref[v], 1 - slot):
                c.start(priority=1)

        _load_expert_weights(wg_buf.at[slot], wu_buf.at[slot], wd_buf.at[slot], wgu_scr, wd_scr)

    @pl.when(hi > lo)
    def _():
        y = _swiglu(xs_ref[...].astype(BF16), wgu_scr, wd_scr)
        row = lax.broadcasted_iota(jnp.int32, (GMM_TM, 1), 0)
        mine = (row >= lo) & (row < hi)

        @pl.when(new_tile)
        def _():
            ys_ref[...] = jnp.where(mine, y, 0.0)

        @pl.when(jnp.logical_not(new_tile))
        def _():
            ys_ref[...] = jnp.where(mine, y, ys_ref[...])


def _gmm(sched, xs, layer, w_gate, w_up, w_down):
    return pl.pallas_call(
        functools.partial(_gmm_kernel, layer=layer),
        grid_spec=pltpu.PrefetchScalarGridSpec(
            num_scalar_prefetch=6,
            grid=(N_VISITS,),
            in_specs=[
                pl.BlockSpec((GMM_TM, D), lambda v, e, t, *_: (t[v], 0)),
                pl.BlockSpec(memory_space=pl.ANY),
                pl.BlockSpec(memory_space=pl.ANY),
                pl.BlockSpec(memory_space=pl.ANY),
            ],
            out_specs=pl.BlockSpec((GMM_TM, D), lambda v, e, t, *_: (t[v], 0)),
            scratch_shapes=[
                pltpu.VMEM((2, D, D_EXPERT), F32),
                pltpu.VMEM((2, D, D_EXPERT), F32),
                pltpu.VMEM((2, D_EXPERT, D), F32),
                pltpu.SemaphoreType.DMA((2,)),
                pltpu.VMEM((D, 2 * D_EXPERT), BF16),
                pltpu.VMEM((D_EXPERT, D), BF16),
            ],
        ),
        out_shape=jax.ShapeDtypeStruct((N_PAIRS, D), F32),
        compiler_params=_cparams(1),
        name="experts",
    )(*sched, xs, w_gate, w_up, w_down)


def _visit_schedule(counts):
    tm = GMM_TM
    ends = jnp.cumsum(counts)
    starts = ends - counts
    first_tile = starts // tm
    last_tile = jnp.maximum(ends - 1, 0) // tm
    n_vis = jnp.where(counts > 0, last_tile - first_tile + 1, 0)
    vis_end = jnp.cumsum(n_vis)
    vis_start = vis_end - n_vis
    total = vis_end[-1]
    v = jnp.minimum(jnp.arange(N_VISITS, dtype=jnp.int32), total - 1)
    e = jnp.sum((vis_end[None, :] <= v[:, None]).astype(jnp.int32), axis=1)
    tile = first_tile[e] + (v - vis_start[e])
    lo = jnp.clip(starts[e] - tile * tm, 0, tm)
    hi = jnp.clip(ends[e] - tile * tm, 0, tm)
    hi = jnp.where(jnp.arange(N_VISITS) < total, hi, lo)
    run = jnp.cumsum((e != jnp.concatenate([e[:1] - 1, e[:-1]])).astype(jnp.int32)) - 1
    later = jnp.where(e[None, :] > e[:, None], e[None, :], N_EXPERTS)
    next_e = jnp.min(later, axis=1)
    next_e = jnp.where(next_e < N_EXPERTS, next_e, -1)
    sched = (e, tile, lo, hi, run, next_e)
    return starts.astype(jnp.int32), tuple(a.astype(jnp.int32) for a in sched)


def _moe(x, mods, layer, norm_g, router_w, router_bias, w_gate, w_up, w_down,
         ws_gate, ws_up, ws_down, next_norm_g, final):
    h, eidx, rank, wts, counts = _route(x, norm_g, mods, layer, router_w, router_bias)
    starts, sched = _visit_schedule(counts.reshape(N_EXPERTS).astype(jnp.int32))
    eidx, rank = eidx[:TOP_K], rank[:TOP_K]
    expert_ids = jnp.arange(N_EXPERTS, dtype=jnp.int32)[:, None, None]
    pos = rank + jnp.sum(jnp.where(eidx[None] == expert_ids, starts[:, None, None], 0), axis=0)
    xs = _dispatch(pos, h)
    ys = _gmm(sched, xs, layer, w_gate, w_up, w_down)
    shared = _shared_expert(h, layer, ws_gate, ws_up, ws_down)
    return _combine(pos, x, shared, wts.T, mods, layer, next_norm_g, ys, final)


def _pool_diff_layer(x, h, mods, layer, j, cache_k, cache_v, cos, sin,
                     diff_w_in, pool_w, pool_scale, lq1, lk1, lq2, lk2, subln_g, w_out):
    lam_init = 0.8 - 0.6 * math.exp(-0.3 * layer)
    proj = _matmul([h], diff_w_in[j], [0], POOL_W + 3 * DIFF_W, F32, tn=1024, name="diff_in_proj")
    y_pool = _pool(proj, pool_w[j], pool_scale[j])
    lam_params = jnp.stack([lq1[j], lk1[j], lq2[j], lk2[j]])
    ck = cache_k[:, j].reshape(N_LAT_SEQ * PAST_LEN, DIFF_W)
    cv = cache_v[:, j].reshape(N_LAT_SEQ * PAST_LEN, DIFF_W)
    o = _diff_attn(proj, ck, cv, lam_params, subln_g[j], cos, sin, lam_init)
    x = _matmul([y_pool, o], w_out[j], [0, 1], D, F32, tn=1024, epilogue="residual",
                residual=x, mods=mods, layer=layer, gate_chunk=2, name="diff_out_proj")
    k_new = proj[:T_PROMPT, POOL_W + DIFF_W:POOL_W + 2 * DIFF_W]
    v_new = proj[:T_PROMPT, POOL_W + 2 * DIFF_W:]
    shape = (N_PROMPT_SEQ, PROMPT_LEN, DIFF_HEADS, 2 * DIFF_DH)
    return x, k_new.reshape(shape), v_new.reshape(shape)


def _mla_layer(x, h, mods, layer, j, cache_ckv, cache_kpe, cos, sin,
               w_dq, q_norm_g, w_uq, w_dkv, kv_norm_g, w_ukv, w_o):
    cq = _matmul([h], w_dq[j], [0], Q_LORA, BF16, epilogue="rms", gain=q_norm_g[j], name="mla_dq")
    ckv = _matmul([h], w_dkv[j], [0], KV_LORA, F32, epilogue="rms", gain=kv_norm_g[j], name="mla_dkv")
    w_kpe = w_dkv[j][:, KV_LORA:]
    kp_dup = _matmul([h], jnp.concatenate([w_kpe, w_kpe], axis=1), [0], LANES, F32, name="mla_kpe")
    w_uq3 = w_uq[j].reshape(Q_LORA, MLA_HEADS, QK_NOPE + QK_ROPE)
    w_uq_nope = w_uq3[:, :, :QK_NOPE].reshape(Q_LORA, MLA_HEADS * QK_NOPE)
    w_uq_pe = w_uq3[:, :, QK_NOPE:].reshape(Q_LORA, MLA_HEADS * QK_ROPE)
    q_nope = _matmul([cq], w_uq_nope, [0], MLA_HEADS * QK_NOPE, BF16, tn=1024, name="mla_uq_nope")
    q_pe = _matmul([cq], w_uq_pe, [0], MLA_HEADS * QK_ROPE, F32, tn=1024, name="mla_uq_pe")
    n_kv = MLA_HEADS * (QK_NOPE + V_DIM)
    kv = _matmul([ckv], w_ukv[j], [0], n_kv, BF16, tn=1024, name="mla_ukv")
    ckv_ctx = cache_ckv[:, j].reshape(N_LAT_SEQ * PAST_LEN, KV_LORA)
    kv_ctx = _matmul([ckv_ctx], w_ukv[j], [0], n_kv, BF16, tn=1024, name="mla_ukv_ctx")
    kpe_ctx = cache_kpe[:, j].reshape(N_LAT_SEQ * PAST_LEN, QK_ROPE)
    kp_ctx_dup = jnp.concatenate([kpe_ctx, kpe_ctx], axis=1)
    o = _mla_attn(q_nope, q_pe, kv, kp_dup, kv_ctx, kp_ctx_dup, cos, sin)
    x = _matmul([o], w_o[j], [0], D, F32, tn=1024, epilogue="residual",
                residual=x, mods=mods, layer=layer, gate_chunk=2, name="mla_out_proj")
    new_ckv = ckv[:T_PROMPT].reshape(N_PROMPT_SEQ, PROMPT_LEN, KV_LORA)
    new_kpe = kp_dup[:T_PROMPT, :QK_ROPE].reshape(N_PROMPT_SEQ, PROMPT_LEN, QK_ROPE)
    return x, new_ckv, new_kpe


def kernel(x_prompt, x_sample, cache_diff_k, cache_diff_v, cache_mla_ckv, cache_mla_kpe, c, c_ctx,
           ada_w, ada_b, norm_mix_g, norm_ffn_g, final_norm_g,
           diff_w_in, pool_w, pool_scale, diff_lambda_q1, diff_lambda_k1, diff_lambda_q2, diff_lambda_k2,
           diff_subln_g, even_w_out,
           mla_w_dq, mla_q_norm_g, mla_w_uq, mla_w_dkv, mla_kv_norm_g, mla_w_ukv, mla_w_o,
           router_w, router_bias, expert_w_gate, expert_w_up, expert_w_down,
           shared_w_gate, shared_w_up, shared_w_down):
    depth = ada_w.shape[0]
    x = (x_prompt.reshape(T_PROMPT, D), x_sample.reshape(T_LAT, D))
    cond = jnp.concatenate(
        [c_ctx[None, :], c, jnp.zeros((N_GROUPS_PAD - 1 - N_LAT_SEQ, D), F32)], axis=0)
    mods = _ada_params(cond, ada_w, ada_b)
    cos, sin = _rope_tables()

    new_dk, new_dv, new_ckv, new_kpe = [], [], [], []
    h = _norm_mod(x, norm_mix_g[0], mods, 0, 0, 1)
    for i in range(depth):
        j = i // 2
        last = i == depth - 1
        if i % 2 == 0:
            x, k_new, v_new = _pool_diff_layer(
                x, h, mods, i, j, cache_diff_k, cache_diff_v, cos, sin,
                diff_w_in, pool_w, pool_scale, diff_lambda_q1, diff_lambda_k1, diff_lambda_q2,
                diff_lambda_k2, diff_subln_g, even_w_out)
            new_dk.append(k_new)
            new_dv.append(v_new)
        else:
            x, ckv, kpe = _mla_layer(
                x, h, mods, i, j, cache_mla_ckv, cache_mla_kpe, cos, sin,
                mla_w_dq, mla_q_norm_g, mla_w_uq, mla_w_dkv, mla_kv_norm_g, mla_w_ukv, mla_w_o)
            new_ckv.append(ckv)
            new_kpe.append(kpe)
        outs = _moe(x, mods, i, norm_ffn_g[i], router_w[i], router_bias[i],
                    expert_w_gate, expert_w_up, expert_w_down,
                    shared_w_gate, shared_w_up, shared_w_down,
                    final_norm_g if last else norm_mix_g[i + 1], final=last)
        if not last:
            x, h = outs
    y_prompt = outs[0].reshape(N_PROMPT_SEQ, PROMPT_LEN, D)
    y_sample = outs[1].reshape(N_LAT_SEQ, LAT_LEN, D)
    return (y_prompt, y_sample, jnp.stack(new_dk, axis=1), jnp.stack(new_dv, axis=1),
            jnp.stack(new_ckv, axis=1), jnp.stack(new_kpe, axis=1))
```

```python
import functools
import math

import jax
import jax.numpy as jnp
from jax import lax
from jax.experimental import pallas as pl
from jax.experimental.pallas import tpu as pltpu

F32 = jnp.float32
BF16 = jnp.bfloat16

D = 2048
N_PROMPT_SEQ = 16
PROMPT_LEN = 256
N_LAT_SEQ = 4
LAT_LEN = 1024
PAST_LEN = 256
T_PROMPT = N_PROMPT_SEQ * PROMPT_LEN
T_LAT = N_LAT_SEQ * LAT_LEN
T = T_PROMPT + T_LAT
UNIT = 1024
N_UNITS = T // UNIT
N_PROMPT_UNITS = T_PROMPT // UNIT
SEQ_PER_PROMPT_UNIT = UNIT // PROMPT_LEN
N_GROUPS_PAD = 8

GRID_W = 64
ROPE_BASE = 10000.0
EPS = 1e-6
POOL_W = 1024
POOL_WINDOWS = (2, 4, 8, 16)
POOL_G = 256
POOL_HALO = 8
DIFF_W = 1024
DIFF_DH = 64
DIFF_HEADS = 8
DIFF_SCALE = DIFF_DH ** -0.5
MLA_HEADS = 16
Q_LORA = 512
KV_LORA = 512
QK_NOPE = 128
QK_ROPE = 64
V_DIM = 128
MLA_SCALE = (QK_NOPE + QK_ROPE) ** -0.5
N_EXPERTS = 64
TOP_K = 6
N_EGROUPS = 8
EGROUP = N_EXPERTS // N_EGROUPS
TOPK_GROUPS = 4
D_EXPERT = 512
ROUTED_SCALE = 2.5

LANES = 128
ATT_QB = 256
ROUTE_TM = 256
COMBINE_TM = 256
GMM_TM = 256
ROW_UNROLL = 4
XS_RING = 4
N_PAIRS = T * TOP_K
N_ROW_TILES = N_PAIRS // GMM_TM
N_VISITS = N_ROW_TILES + N_EXPERTS
VMEM_LIMIT = 56 * 1024 * 1024

_NT = (((1,), (1,)), ((), ()))
_NN = (((1,), (0,)), ((), ()))


def _cparams(n_axes, vmem=VMEM_LIMIT):
    return pltpu.CompilerParams(dimension_semantics=("arbitrary",) * n_axes, vmem_limit_bytes=vmem)


def _group_of_tile(i, tm):
    n_p = T_PROMPT // tm
    per_seq = LAT_LEN // tm
    return jnp.where(i < n_p, 0, 1 + (i - n_p) // per_seq)


def _sigmoid(x):
    return 1.0 / (1.0 + jnp.exp(-x))


def _rms(x, g):
    return x * lax.rsqrt(jnp.mean(x * x, axis=-1, keepdims=True) + EPS) * g


def _dot_3pass(a, b, dims):
    a_hi = a.astype(BF16)
    a_lo = (a - a_hi.astype(F32)).astype(BF16)
    b_hi = b.astype(BF16)
    b_lo = (b - b_hi.astype(F32)).astype(BF16)

    def dot(p, q):
        return lax.dot_general(p, q, dims, preferred_element_type=F32)

    return dot(a_hi, b_hi) + (dot(a_hi, b_lo) + dot(a_lo, b_hi))


def _ada_kernel(c_ref, w_ref, b_ref, o_ref):
    c = c_ref[...]
    s = c * _sigmoid(c)
    o_ref[...] = _dot_3pass(s, w_ref[...], _NN) + b_ref[...]


def _ada_params(cond, ada_w, ada_b):
    depth, _, n = ada_w.shape
    tn = 1024
    out = pl.pallas_call(
        _ada_kernel,
        grid=(depth, n // tn),
        in_specs=[
            pl.BlockSpec((N_GROUPS_PAD, D), lambda l, j: (0, 0)),
            pl.BlockSpec((None, D, tn), lambda l, j: (l, 0, j)),
            pl.BlockSpec((None, 1, tn), lambda l, j: (l, 0, j)),
        ],
        out_specs=pl.BlockSpec((None, N_GROUPS_PAD, tn), lambda l, j: (l, 0, j)),
        out_shape=jax.ShapeDtypeStruct((depth, N_GROUPS_PAD, n), F32),
        compiler_params=_cparams(2),
        name="ada_params",
    )(cond, ada_w, ada_b.reshape(depth, 1, n))
    return out.reshape(depth, N_GROUPS_PAD, 1, n)


def _mod_spec(layer, chunk, tm):
    return pl.BlockSpec((None, None, 1, D), lambda i: (layer, _group_of_tile(i, tm), 0, chunk))


def _row_source(x):
    return list(x) if isinstance(x, (tuple, list)) else [x]


def _row_source_specs(parts, tm, width, col_of, row_axis):
    if len(parts) == 1:
        return [pl.BlockSpec((tm, width), lambda *ids: (ids[row_axis], col_of(*ids)))]
    n_p = T_PROMPT // tm
    return [pl.BlockSpec((tm, width), lambda *ids: (jnp.minimum(ids[row_axis], n_p - 1), col_of(*ids))),
            pl.BlockSpec((tm, width), lambda *ids: (jnp.maximum(ids[row_axis] - n_p, 0), col_of(*ids)))]


def _read_rows(refs, row_tile, tm):
    if len(refs) == 1:
        return refs[0][...]
    return jnp.where(row_tile < T_PROMPT // tm, refs[0][...], refs[1][...])


def _norm_mod_kernel(*refs, n_x, tm):
    x_refs = refs[:n_x]
    g_ref, sh_ref, sc_ref, o_ref = refs[n_x:]
    y = _rms(_read_rows(x_refs, pl.program_id(0), tm), g_ref[...])
    o_ref[...] = (y * (1.0 + sc_ref[...]) + sh_ref[...]).astype(o_ref.dtype)


def _norm_mod(x, g, mods, layer, shift_chunk, scale_chunk):
    tm = 512
    parts = _row_source(x)
    return pl.pallas_call(
        functools.partial(_norm_mod_kernel, n_x=len(parts), tm=tm),
        grid=(T // tm,),
        in_specs=_row_source_specs(parts, tm, D, lambda i: 0, 0) + [
            pl.BlockSpec((1, D), lambda i: (0, 0)),
            _mod_spec(layer, shift_chunk, tm),
            _mod_spec(layer, scale_chunk, tm),
        ],
        out_specs=pl.BlockSpec((tm, D), lambda i: (i, 0)),
        out_shape=jax.ShapeDtypeStruct((T, D), BF16),
        compiler_params=_cparams(1),
        name="norm_mod",
    )(*parts, g.reshape(1, D), mods, mods)


def _mm_kernel(*refs, n_a, epilogue, n_res, tm):
    a_refs = refs[:n_a]
    w_refs = refs[n_a:2 * n_a]
    pos = 2 * n_a
    if epilogue == "rms":
        g_ref = refs[pos]
        pos += 1
    elif epilogue == "residual":
        res_refs, gate_ref = refs[pos:pos + n_res], refs[pos + n_res]
        pos += n_res + 1
    o_ref = refs[pos]
    w_scr = refs[pos + 1:]

    @pl.when(pl.program_id(1) == 0)
    def _():
        for w_ref, scr in zip(w_refs, w_scr):
            scr[...] = w_ref[...].astype(BF16)

    acc = None
    for a_ref, scr in zip(a_refs, w_scr):
        part = jnp.dot(a_ref[...].astype(BF16), scr[...], preferred_element_type=F32)
        acc = part if acc is None else acc + part
    if epilogue == "rms":
        acc = _rms(acc, g_ref[...])
    elif epilogue == "residual":
        acc = _read_rows(res_refs, pl.program_id(1), tm) + gate_ref[...] * acc
    o_ref[...] = acc.astype(o_ref.dtype)


def _matmul(a_list, w, w_row_blocks, n_out, out_dtype, *, tm=512, tn=None, epilogue=None,
            gain=None, residual=None, mods=None, layer=None, gate_chunk=None, name="matmul"):
    m = a_list[0].shape[0]
    tn = n_out if tn is None else tn
    n_a = len(a_list)
    in_specs, args = [], []
    for a in a_list:
        k = a.shape[1]
        in_specs.append(pl.BlockSpec((tm, k), lambda j, i: (i, 0)))
        args.append(a)
    scratch = []
    for a, rb in zip(a_list, w_row_blocks):
        k = a.shape[1]
        in_specs.append(pl.BlockSpec((k, tn), lambda j, i, rb=rb: (rb, j)))
        args.append(w)
        scratch.append(pltpu.VMEM((k, tn), BF16))
    if epilogue == "rms":
        assert tn == n_out
        in_specs.append(pl.BlockSpec((1, tn), lambda j, i: (0, 0)))
        args.append(gain.reshape(1, n_out))
    elif epilogue == "residual":
        res_parts = _row_source(residual)
        in_specs += _row_source_specs(res_parts, tm, tn, lambda j, i: j, 1)
        args += res_parts
        in_specs.append(pl.BlockSpec(
            (None, None, 1, tn),
            lambda j, i: (layer, _group_of_tile(i, tm), 0, gate_chunk * (D // tn) + j)))
        args.append(mods)
    return pl.pallas_call(
        functools.partial(_mm_kernel, n_a=n_a, epilogue=epilogue, tm=tm,
                          n_res=len(_row_source(residual)) if epilogue == "residual" else 0),
        grid=(n_out // tn, m // tm),
        in_specs=in_specs,
        out_specs=pl.BlockSpec((tm, tn), lambda j, i: (i, j)),
        out_shape=jax.ShapeDtypeStruct((m, n_out), out_dtype),
        scratch_shapes=scratch,
        compiler_params=_cparams(2),
        name=name,
    )(*args)


def _pool_seq(u_ref, row0, seq_len, pw_ref, scale_ref, o_ref, pad_ref):
    t = lax.broadcasted_iota(jnp.int32, (seq_len, 1), 0)
    zeros = jnp.zeros((POOL_HALO, POOL_G), F32)
    for g, w in enumerate(POOL_WINDOWS):
        cols = slice(g * POOL_G, (g + 1) * POOL_G)
        ug = u_ref[row0:row0 + seq_len, cols]
        pad_ref[0:POOL_HALO, :] = zeros
        pad_ref[POOL_HALO:POOL_HALO + seq_len, :] = ug
        pad_ref[POOL_HALO + seq_len:2 * POOL_HALO + seq_len, :] = zeros
        total = None
        for off in range(-(w // 2), w // 2):
            part = pad_ref[POOL_HALO + off:POOL_HALO + off + seq_len, :]
            total = part if total is None else total + part
        cnt = (jnp.minimum(t + w // 2, seq_len) - jnp.maximum(t - w // 2, 0)).astype(F32)
        d = (total / cnt - ug).astype(BF16)
        y = jnp.dot(d, pw_ref[g].astype(BF16), preferred_element_type=F32)
        o_ref[row0:row0 + seq_len, cols] = (y * scale_ref[:, cols]).astype(o_ref.dtype)


def _pool_kernel(u_ref, pw_ref, scale_ref, o_ref, pad_ref):
    unit = pl.program_id(0)

    @pl.when(unit < N_PROMPT_UNITS)
    def _():
        for s in range(SEQ_PER_PROMPT_UNIT):
            _pool_seq(u_ref, s * PROMPT_LEN, PROMPT_LEN, pw_ref, scale_ref, o_ref, pad_ref)

    @pl.when(unit >= N_PROMPT_UNITS)
    def _():
        _pool_seq(u_ref, 0, LAT_LEN, pw_ref, scale_ref, o_ref, pad_ref)


def _pool(proj, pool_w, pool_scale):
    return pl.pallas_call(
        _pool_kernel,
        grid=(N_UNITS,),
        in_specs=[
            pl.BlockSpec((UNIT, POOL_W), lambda u: (u, 0)),
            pl.BlockSpec((len(POOL_WINDOWS), POOL_G, POOL_G), lambda u: (0, 0, 0)),
            pl.BlockSpec((1, POOL_W), lambda u: (0, 0)),
        ],
        out_specs=pl.BlockSpec((UNIT, POOL_W), lambda u: (u, 0)),
        out_shape=jax.ShapeDtypeStruct((T, POOL_W), BF16),
        scratch_shapes=[pltpu.VMEM((LAT_LEN + 2 * POOL_HALO, POOL_G), F32)],
        compiler_params=_cparams(1),
        name="pool",
    )(proj, pool_w, pool_scale.reshape(1, POOL_W))


def _rope_tables():
    pos = jnp.arange(LAT_LEN)
    row = (pos // GRID_W).astype(F32)
    col = (pos % GRID_W).astype(F32)
    quarter = QK_ROPE // 4
    inv = ROPE_BASE ** (-jnp.arange(quarter, dtype=F32) / quarter)
    lane = jnp.arange(LANES)
    axis = (lane % QK_ROPE) // (QK_ROPE // 2)
    freq = inv[lane % quarter]
    p = jnp.where(axis[None, :] == 0, row[:, None], col[:, None])
    ang = p * freq[None, :]
    sign = jnp.where((lane % (QK_ROPE // 2)) < quarter, -1.0, 1.0).astype(F32)
    return jnp.cos(ang), jnp.sin(ang) * sign[None, :]


def _rope(x, cos, sin_signed):
    quarter = QK_ROPE // 4
    lane = lax.broadcasted_iota(jnp.int32, x.shape, 1)
    first_half = (lane % (QK_ROPE // 2)) < quarter
    partner = jnp.where(first_half, pltpu.roll(x, LANES - quarter, 1), pltpu.roll(x, quarter, 1))
    return x * cos + partner * sin_signed


def _diff_core(q, k, v, lam, g, lam_init):
    lane = lax.broadcasted_iota(jnp.int32, q.shape, 1)
    low = lane < DIFF_DH
    q1 = jnp.where(low, q, 0.0).astype(BF16)
    q2 = jnp.where(low, 0.0, q).astype(BF16)

    def probs(qm):
        s = lax.dot_general(qm, k, _NT, preferred_element_type=F32)
        e = jnp.exp(s - jnp.max(s, axis=-1, keepdims=True))
        return e * (1.0 / jnp.sum(e, axis=-1, keepdims=True))

    w = (probs(q1) - lam * probs(q2)).astype(BF16)
    o = jnp.dot(w, v, preferred_element_type=F32)
    return _rms(o, g) * (1.0 - lam_init)


def _diff_attn_kernel(lp_ref, q_ref, k_ref, v_ref, ck_ref, cv_ref, cos_ref, sin_ref, g_ref, o_ref,
                      *, lam_init):
    unit = pl.program_id(0)
    lp = lp_ref[...]
    lam = (jnp.exp(jnp.sum(lp[0:1] * lp[1:2], axis=-1, keepdims=True))
           - jnp.exp(jnp.sum(lp[2:3] * lp[3:4], axis=-1, keepdims=True)) + lam_init)
    g = g_ref[...]

    @pl.when(unit < N_PROMPT_UNITS)
    def _():
        for s in range(SEQ_PER_PROMPT_UNIT):
            rows = slice(s * PROMPT_LEN, (s + 1) * PROMPT_LEN)
            q = q_ref[rows, :] * DIFF_SCALE
            k = k_ref[rows, :].astype(BF16)
            v = v_ref[rows, :].astype(BF16)
            o_ref[rows, :] = _diff_core(q, k, v, lam, g, lam_init).astype(o_ref.dtype)

    @pl.when(unit >= N_PROMPT_UNITS)
    def _():
        cos, sin = cos_ref[...], sin_ref[...]
        k = jnp.concatenate([ck_ref[...].astype(BF16),
                             _rope(k_ref[...], cos, sin).astype(BF16)], axis=0)
        v = jnp.concatenate([cv_ref[...].astype(BF16), v_ref[...].astype(BF16)], axis=0)
        for b in range(LAT_LEN // ATT_QB):
            rows = slice(b * ATT_QB, (b + 1) * ATT_QB)
            q = _rope(q_ref[rows, :], cos[rows], sin[rows]) * DIFF_SCALE
            o_ref[rows, :] = _diff_core(q, k, v, lam, g, lam_init).astype(o_ref.dtype)


def _diff_attn(proj, cache_k, cache_v, lam_params, subln_g, cos, sin, lam_init):
    hd = 2 * DIFF_DH
    q0, k0, v0 = POOL_W // hd, (POOL_W + DIFF_W) // hd, (POOL_W + 2 * DIFF_W) // hd

    def ctx_map(u, h):
        return (jnp.maximum(u - N_PROMPT_UNITS, 0), h)

    return pl.pallas_call(
        functools.partial(_diff_attn_kernel, lam_init=lam_init),
        grid=(N_UNITS, DIFF_HEADS),
        in_specs=[
            pl.BlockSpec((4, DIFF_DH), lambda u, h: (0, 0)),
            pl.BlockSpec((UNIT, hd), lambda u, h: (u, q0 + h)),
            pl.BlockSpec((UNIT, hd), lambda u, h: (u, k0 + h)),
            pl.BlockSpec((UNIT, hd), lambda u, h: (u, v0 + h)),
            pl.BlockSpec((PAST_LEN, hd), ctx_map),
            pl.BlockSpec((PAST_LEN, hd), ctx_map),
            pl.BlockSpec((LAT_LEN, LANES), lambda u, h: (0, 0)),
            pl.BlockSpec((LAT_LEN, LANES), lambda u, h: (0, 0)),
            pl.BlockSpec((1, hd), lambda u, h: (0, 0)),
        ],
        out_specs=pl.BlockSpec((UNIT, hd), lambda u, h: (u, h)),
        out_shape=jax.ShapeDtypeStruct((T, DIFF_W), BF16),
        compiler_params=_cparams(2),
        name="diff_attn",
    )(lam_params, proj, proj, proj, cache_k, cache_v, cos, sin, subln_g.reshape(1, hd))


def _mla_core(qc, kc, v):
    s = lax.dot_general(qc, kc, _NT, preferred_element_type=F32) * MLA_SCALE
    e = jnp.exp(s - jnp.max(s, axis=-1, keepdims=True))
    p = (e * (1.0 / jnp.sum(e, axis=-1, keepdims=True))).astype(BF16)
    return jnp.dot(p, v, preferred_element_type=F32)


def _mla_attn_kernel(qn_ref, qp_ref, kn_ref, v_ref, kp_ref, ckn_ref, cv_ref, ckp_ref, cos_ref, sin_ref,
                     o_ref):
    unit = pl.program_id(0)
    head = pl.program_id(1)
    lane = lax.broadcasted_iota(jnp.int32, (ATT_QB, LANES), 1)
    mine = (lane // QK_ROPE) == (head % 2)

    @pl.when(unit < N_PROMPT_UNITS)
    def _():
        for s in range(SEQ_PER_PROMPT_UNIT):
            rows = slice(s * PROMPT_LEN, (s + 1) * PROMPT_LEN)
            qp = jnp.where(mine, qp_ref[rows, :], 0.0).astype(BF16)
            qc = jnp.concatenate([qn_ref[rows, :], qp], axis=1)
            kc = jnp.concatenate([kn_ref[rows, :], kp_ref[rows, :].astype(BF16)], axis=1)
            o_ref[rows, :] = _mla_core(qc, kc, v_ref[rows, :]).astype(o_ref.dtype)

    @pl.when(unit >= N_PROMPT_UNITS)
    def _():
        cos, sin = cos_ref[...], sin_ref[...]
        k_ctx = jnp.concatenate([ckn_ref[...], ckp_ref[...].astype(BF16)], axis=1)
        k_new = jnp.concatenate([kn_ref[...], _rope(kp_ref[...], cos, sin).astype(BF16)], axis=1)
        kc = jnp.concatenate([k_ctx, k_new], axis=0)
        v = jnp.concatenate([cv_ref[...], v_ref[...]], axis=0)
        for b in range(LAT_LEN // ATT_QB):
            rows = slice(b * ATT_QB, (b + 1) * ATT_QB)
            qp = jnp.where(mine, _rope(qp_ref[rows, :], cos[rows], sin[rows]), 0.0).astype(BF16)
            qc = jnp.concatenate([qn_ref[rows, :], qp], axis=1)
            o_ref[rows, :] = _mla_core(qc, kc, v).astype(o_ref.dtype)


def _mla_attn(q_nope, q_pe, kv, kp_dup, kv_ctx, kp_ctx_dup, cos, sin):
    def ctx_row(u):
        return jnp.maximum(u - N_PROMPT_UNITS, 0)

    return pl.pallas_call(
        _mla_attn_kernel,
        grid=(N_UNITS, MLA_HEADS),
        in_specs=[
            pl.BlockSpec((UNIT, QK_NOPE), lambda u, h: (u, h)),
            pl.BlockSpec((UNIT, LANES), lambda u, h: (u, h // 2)),
            pl.BlockSpec((UNIT, QK_NOPE), lambda u, h: (u, 2 * h)),
            pl.BlockSpec((UNIT, V_DIM), lambda u, h: (u, 2 * h + 1)),
            pl.BlockSpec((UNIT, LANES), lambda u, h: (u, 0)),
            pl.BlockSpec((PAST_LEN, QK_NOPE), lambda u, h: (ctx_row(u), 2 * h)),
            pl.BlockSpec((PAST_LEN, V_DIM), lambda u, h: (ctx_row(u), 2 * h + 1)),
            pl.BlockSpec((PAST_LEN, LANES), lambda u, h: (ctx_row(u), 0)),
            pl.BlockSpec((LAT_LEN, LANES), lambda u, h: (0, 0)),
            pl.BlockSpec((LAT_LEN, LANES), lambda u, h: (0, 0)),
        ],
        out_specs=pl.BlockSpec((UNIT, V_DIM), lambda u, h: (u, h)),
        out_shape=jax.ShapeDtypeStruct((T, MLA_HEADS * V_DIM), BF16),
        compiler_params=_cparams(2),
        name="mla_attn",
    )(q_nope, q_pe, kv, kv, kp_dup, kv_ctx, kv_ctx, kp_ctx_dup, cos, sin)


def _route_kernel(x_ref, g_ref, sh_ref, sc_ref, rw_ref, rb_ref,
                  h_ref, eidx_ref, rank_ref, wts_ref, cnt_ref, carry_ref):
    tm = ROUTE_TM

    @pl.when(pl.program_id(0) == 0)
    def _():
        carry_ref[...] = jnp.zeros_like(carry_ref)

    h = _rms(x_ref[...], g_ref[...]) * (1.0 + sc_ref[...]) + sh_ref[...]
    h_ref[...] = h
    logits = _dot_3pass(rw_ref[...], h, _NT)
    scores = _sigmoid(logits)
    choice = scores + rb_ref[...]
    neg = -jnp.inf

    def take_max(vals, iota, n):
        m = jnp.max(vals, axis=0, keepdims=True)
        idx = jnp.min(jnp.where(vals == m, iota, n), axis=0, keepdims=True)
        return iota == idx, m, idx

    iota8 = lax.broadcasted_iota(jnp.int32, (8, tm), 0)

    def stack_rows(rows, dtype):
        out = jnp.zeros((8, tm), dtype)
        for k, r in enumerate(rows):
            out = jnp.where(iota8 == k, r.astype(dtype), out)
        return out

    gscore = []
    for gi in range(N_EGROUPS):
        grp = choice[gi * EGROUP:(gi + 1) * EGROUP]
        oh, m1, _ = take_max(grp, iota8, EGROUP)
        m2 = jnp.max(jnp.where(oh, neg, grp), axis=0, keepdims=True)
        gscore.append(m1 + m2)
    cur = stack_rows(gscore, F32)
    gsel = jnp.zeros((N_EGROUPS, tm), F32)
    for _ in range(TOPK_GROUPS):
        oh, _, _ = take_max(cur, iota8, N_EGROUPS)
        gsel = jnp.where(oh, 1.0, gsel)
        cur = jnp.where(oh, neg, cur)
    emask = jnp.concatenate(
        [jnp.broadcast_to(gsel[gi:gi + 1], (EGROUP, tm)) for gi in range(N_EGROUPS)], axis=0)
    masked = jnp.where(emask > 0.0, choice, neg)

    iota_e = lax.broadcasted_iota(jnp.int32, (N_EXPERTS, tm), 0)
    onehots, idxs, wsel = [], [], []
    sel = jnp.zeros((N_EXPERTS, tm), F32)
    for _ in range(TOP_K):
        oh, _, idx = take_max(masked, iota_e, N_EXPERTS)
        onehots.append(oh)
        idxs.append(idx)
        wsel.append(jnp.sum(jnp.where(oh, scores, 0.0), axis=0, keepdims=True))
        masked = jnp.where(oh, neg, masked)
        sel = jnp.where(oh, 1.0, sel)
    wsum = wsel[0]
    for w in wsel[1:]:
        wsum = wsum + w

    before = jnp.where(lax.broadcasted_iota(jnp.int32, (tm, tm), 0)
                       < lax.broadcasted_iota(jnp.int32, (tm, tm), 1), 1.0, 0.0).astype(BF16)
    rank_all = carry_ref[...] + jnp.dot(sel.astype(BF16), before, preferred_element_type=F32)
    carry_ref[...] = carry_ref[...] + jnp.sum(sel, axis=1, keepdims=True)
    cnt_ref[...] = carry_ref[...]

    ranks = [jnp.sum(jnp.where(oh, rank_all, 0.0), axis=0, keepdims=True) for oh in onehots]
    eidx_ref[...] = stack_rows(idxs, jnp.int32)
    rank_ref[...] = stack_rows(ranks, jnp.int32)
    wts_ref[...] = stack_rows([w / wsum * ROUTED_SCALE for w in wsel], F32)


def _route(x, g, mods, layer, router_w, router_bias):
    tm = ROUTE_TM
    return pl.pallas_call(
        _route_kernel,
        grid=(T // tm,),
        in_specs=[
            pl.BlockSpec((tm, D), lambda i: (i, 0)),
            pl.BlockSpec((1, D), lambda i: (0, 0)),
            _mod_spec(layer, 3, tm),
            _mod_spec(layer, 4, tm),
            pl.BlockSpec((N_EXPERTS, D), lambda i: (0, 0)),
            pl.BlockSpec((N_EXPERTS, 1), lambda i: (0, 0)),
        ],
        out_specs=[
            pl.BlockSpec((tm, D), lambda i: (i, 0)),
            pl.BlockSpec((8, tm), lambda i: (0, i)),
            pl.BlockSpec((8, tm), lambda i: (0, i)),
            pl.BlockSpec((8, tm), lambda i: (0, i)),
            pl.BlockSpec((N_EXPERTS, 1), lambda i: (0, 0)),
        ],
        out_shape=[
            jax.ShapeDtypeStruct((T, D), F32),
            jax.ShapeDtypeStruct((8, T), jnp.int32),
            jax.ShapeDtypeStruct((8, T), jnp.int32),
            jax.ShapeDtypeStruct((8, T), F32),
            jax.ShapeDtypeStruct((N_EXPERTS, 1), F32),
        ],
        scratch_shapes=[pltpu.VMEM((N_EXPERTS, 1), F32)],
        compiler_params=_cparams(1),
        name="route",
    )(x, g.reshape(1, D), mods, mods, router_w.T, router_bias.reshape(N_EXPERTS, 1))


def _row_copy(src, src_row, dst, dst_row, sem):
    return pltpu.make_async_copy(src.at[pl.ds(src_row, 1), :], dst.at[pl.ds(dst_row, 1), :], sem)


def _tile_positions(pos, tm):
    return pos.reshape(TOP_K, T // tm, tm).transpose(1, 0, 2).reshape(T // tm, 1, TOP_K * tm)


def _dispatch_kernel(pos_ref, h_ref, xs_ref, sem):
    tm = ROUTE_TM

    def issue(j, carry):
        for k in range(TOP_K):
            _row_copy(h_ref, j, xs_ref, pos_ref[0, 0, k * tm + j], sem).start(priority=k % 2)
        return carry

    def drain(j, carry):
        for _ in range(TOP_K):
            _row_copy(h_ref, 0, xs_ref, 0, sem).wait()
        return carry

    lax.fori_loop(0, tm, issue, 0, unroll=ROW_UNROLL)
    lax.fori_loop(0, tm, drain, 0, unroll=ROW_UNROLL)


def _dispatch(pos, h):
    tm = ROUTE_TM
    return pl.pallas_call(
        _dispatch_kernel,
        grid=(T // tm,),
        in_specs=[
            pl.BlockSpec((1, 1, TOP_K * tm), lambda i: (i, 0, 0), memory_space=pltpu.SMEM),
            pl.BlockSpec((tm, D), lambda i: (i, 0)),
        ],
        out_specs=pl.BlockSpec(memory_space=pl.ANY),
        out_shape=jax.ShapeDtypeStruct((N_PAIRS, D), F32),
        scratch_shapes=[pltpu.SemaphoreType.DMA(())],
        compiler_params=_cparams(1),
        name="dispatch",
    )(_tile_positions(pos, tm), h)


def _combine_kernel(pos_ref, pos_next_ref, x_ref, sh_ref, w_ref, gate_ref, ng_ref, nsh_ref, nsc_ref,
                    ys_ref, *rest, final):
    out_refs, (buf_ref, sem) = rest[:-2], rest[-2:]
    tm = COMBINE_TM
    i = pl.program_id(0)
    n = pl.num_programs(0)
    slot = i % 2

    def start_gathers(p_ref, dst_slot):
        def issue(j, carry):
            for k in range(TOP_K):
                _row_copy(ys_ref, p_ref[0, 0, k * tm + j], buf_ref.at[dst_slot, k], j,
                          sem.at[dst_slot]).start(priority=k % 2)
            return carry
        lax.fori_loop(0, tm, issue, 0, unroll=ROW_UNROLL)

    @pl.when(i == 0)
    def _():
        start_gathers(pos_ref, 0)

    @pl.when(i + 1 < n)
    def _():
        start_gathers(pos_next_ref, 1 - slot)

    def drain(j, carry):
        for k in range(TOP_K):
            _row_copy(ys_ref, 0, buf_ref.at[slot, k], 0, sem.at[slot]).wait()
        return carry

    lax.fori_loop(0, tm, drain, 0, unroll=ROW_UNROLL)
    w = w_ref[...]
    acc = sh_ref[...]
    for k in range(TOP_K):
        acc = acc + w[:, k:k + 1] * buf_ref[slot, k]
    xn = x_ref[...] + gate_ref[...] * acc
    normed = _rms(xn, ng_ref[...])
    if not final:
        out_refs[0][...] = xn
        out_refs[1][...] = (normed * (1.0 + nsc_ref[...]) + nsh_ref[...]).astype(BF16)
    else:
        xn = normed

        @pl.when(i < T_PROMPT // tm)
        def _():
            out_refs[0][...] = xn

        @pl.when(i >= T_PROMPT // tm)
        def _():
            out_refs[1][...] = xn


def _combine(pos, x, shared, wts_t, mods, layer, next_norm_g, ys, final):
    tm = COMBINE_TM
    n_p = T_PROMPT // tm
    next_layer = layer if final else layer + 1
    if final:
        out_specs = [pl.BlockSpec((tm, D), lambda i: (jnp.minimum(i, n_p - 1), 0)),
                     pl.BlockSpec((tm, D), lambda i: (jnp.maximum(i - n_p, 0), 0))]
        out_shape = [jax.ShapeDtypeStruct((T_PROMPT, D), F32), jax.ShapeDtypeStruct((T_LAT, D), F32)]
    else:
        out_specs = [pl.BlockSpec((tm, D), lambda i: (i, 0)), pl.BlockSpec((tm, D), lambda i: (i, 0))]
        out_shape = [jax.ShapeDtypeStruct((T, D), F32), jax.ShapeDtypeStruct((T, D), BF16)]
    tiles = _tile_positions(pos, tm)
    return pl.pallas_call(
        functools.partial(_combine_kernel, final=final),
        grid=(T // tm,),
        in_specs=[
            pl.BlockSpec((1, 1, TOP_K * tm), lambda i: (i, 0, 0), memory_space=pltpu.SMEM),
            pl.BlockSpec((1, 1, TOP_K * tm), lambda i: (jnp.minimum(i + 1, T // tm - 1), 0, 0),
                         memory_space=pltpu.SMEM),
            pl.BlockSpec((tm, D), lambda i: (i, 0)),
            pl.BlockSpec((tm, D), lambda i: (i, 0)),
            pl.BlockSpec((tm, 8), lambda i: (i, 0)),
            pl.BlockSpec((None, None, 1, D), lambda i: (layer, _group_of_tile(i, tm), 0, 5)),
            pl.BlockSpec((1, D), lambda i: (0, 0)),
            _mod_spec(next_layer, 0, tm),
            _mod_spec(next_layer, 1, tm),
            pl.BlockSpec(memory_space=pl.ANY),
        ],
        out_specs=out_specs,
        out_shape=out_shape,
        scratch_shapes=[pltpu.VMEM((2, TOP_K, tm, D), F32), pltpu.SemaphoreType.DMA((2,))],
        compiler_params=_cparams(1),
        name="combine",
    )(tiles, tiles, x, shared, wts_t, mods, next_norm_g.reshape(1, D), mods, mods, ys)


def _swiglu(x, wgu_scr, wd_scr):
    gu = jnp.dot(x, wgu_scr[...], preferred_element_type=F32)
    gate, up = gu[:, :D_EXPERT], gu[:, D_EXPERT:]
    act = (gate * _sigmoid(gate) * up).astype(BF16)
    return jnp.dot(act, wd_scr[...], preferred_element_type=F32)


def _load_expert_weights(wg_ref, wu_ref, wd_ref, wgu_scr, wd_scr):
    wgu_scr[:, :D_EXPERT] = wg_ref[...].astype(BF16)
    wgu_scr[:, D_EXPERT:] = wu_ref[...].astype(BF16)
    wd_scr[...] = wd_ref[...].astype(BF16)


def _shared_kernel(h_ref, wg_ref, wu_ref, wd_ref, o_ref, wgu_scr, wd_scr):
    @pl.when(pl.program_id(0) == 0)
    def _():
        _load_expert_weights(wg_ref, wu_ref, wd_ref, wgu_scr, wd_scr)

    o_ref[...] = _swiglu(h_ref[...].astype(BF16), wgu_scr, wd_scr)


def _shared_expert(h, layer, w_gate, w_up, w_down):
    tm = 512
    return pl.pallas_call(
        _shared_kernel,
        grid=(T // tm,),
        in_specs=[
            pl.BlockSpec((tm, D), lambda i: (i, 0)),
            pl.BlockSpec((None, D, D_EXPERT), lambda i: (layer, 0, 0)),
            pl.BlockSpec((None, D, D_EXPERT), lambda i: (layer, 0, 0)),
            pl.BlockSpec((None, D_EXPERT, D), lambda i: (layer, 0, 0)),
        ],
        out_specs=pl.BlockSpec((tm, D), lambda i: (i, 0)),
        out_shape=jax.ShapeDtypeStruct((T, D), F32),
        scratch_shapes=[pltpu.VMEM((D, 2 * D_EXPERT), BF16), pltpu.VMEM((D_EXPERT, D), BF16)],
        compiler_params=_cparams(1),
        name="shared_expert",
    )(h, w_gate, w_up, w_down)


def _gmm_kernel(e_ref, t_ref, lo_ref, hi_ref, run_ref, next_e_ref, xs_hbm, wg_hbm, wu_hbm, wd_hbm, ys_ref,
                xs_buf, xs_sem, wg_buf, wu_buf, wd_buf, wsem, wgu_scr, wd_scr, *, layer):
    v = pl.program_id(0)
    prev = jnp.maximum(v - 1, 0)
    new_expert = (v == 0) | (e_ref[v] != e_ref[prev])
    tile = t_ref[v]
    new_tile = (v == 0) | (tile != t_ref[prev])
    lo, hi = lo_ref[v], hi_ref[v]
    slot = run_ref[v] % 2

    def tile_copy(t):
        rows = pl.ds(pl.multiple_of(t * GMM_TM, GMM_TM), GMM_TM)
        return pltpu.make_async_copy(xs_hbm.at[rows, :], xs_buf.at[t % XS_RING], xs_sem.at[t % XS_RING])

    @pl.when(v == 0)
    def _():
        for t in range(XS_RING - 1):
            tile_copy(t).start()

    @pl.when(new_tile)
    def _():
        tile_copy(tile).wait()

        @pl.when(tile + XS_RING - 1 < N_ROW_TILES)
        def _():
            tile_copy(tile + XS_RING - 1).start()

    def weight_copies(expert, dst_slot):
        return [pltpu.make_async_copy(src.at[layer, expert], dst.at[dst_slot], wsem.at[dst_slot])
                for src, dst in ((wg_hbm, wg_buf), (wu_hbm, wu_buf), (wd_hbm, wd_buf))]

    @pl.when(v == 0)
    def _():
        for c in weight_copies(e_ref[0], 0):
            c.start(priority=1)

    @pl.when(new_expert)
    def _():
        for c in weight_copies(e_ref[v], slot):
            c.wait()

        @pl.when(next_e_ref[v] >= 0)
        def _():
            for c in weight_copies(next_e_ref[v], 1 - slot):
                c.start(priority=1)

        _load_expert_weights(wg_buf.at[slot], wu_buf.at[slot], wd_buf.at[slot], wgu_scr, wd_scr)

    @pl.when(hi > lo)
    def _():
        y = _swiglu(xs_buf[tile % XS_RING].astype(BF16), wgu_scr, wd_scr)
        row = lax.broadcasted_iota(jnp.int32, (GMM_TM, 1), 0)
        mine = (row >= lo) & (row < hi)

        @pl.when(new_tile)
        def _():
            ys_ref[...] = jnp.where(mine, y, 0.0)

        @pl.when(jnp.logical_not(new_tile))
        def _():
            ys_ref[...] = jnp.where(mine, y, ys_ref[...])


def _gmm(sched, xs, layer, w_gate, w_up, w_down):
    return pl.pallas_call(
        functools.partial(_gmm_kernel, layer=layer),
        grid_spec=pltpu.PrefetchScalarGridSpec(
            num_scalar_prefetch=6,
            grid=(N_VISITS,),
            in_specs=[
                pl.BlockSpec(memory_space=pl.ANY),
                pl.BlockSpec(memory_space=pl.ANY),
                pl.BlockSpec(memory_space=pl.ANY),
                pl.BlockSpec(memory_space=pl.ANY),
            ],
            out_specs=pl.BlockSpec((GMM_TM, D), lambda v, e, t, *_: (t[v], 0)),
            scratch_shapes=[
                pltpu.VMEM((XS_RING, GMM_TM, D), F32),
                pltpu.SemaphoreType.DMA((XS_RING,)),
                pltpu.VMEM((2, D, D_EXPERT), F32),
                pltpu.VMEM((2, D, D_EXPERT), F32),
                pltpu.VMEM((2, D_EXPERT, D), F32),
                pltpu.SemaphoreType.DMA((2,)),
                pltpu.VMEM((D, 2 * D_EXPERT), BF16),
                pltpu.VMEM((D_EXPERT, D), BF16),
            ],
        ),
        out_shape=jax.ShapeDtypeStruct((N_PAIRS, D), F32),
        compiler_params=_cparams(1),
        name="experts",
    )(*sched, xs, w_gate, w_up, w_down)


def _visit_schedule(counts):
    tm = GMM_TM
    ends = jnp.cumsum(counts)
    starts = ends - counts
    first_tile = starts // tm
    last_tile = jnp.maximum(ends - 1, 0) // tm
    n_vis = jnp.where(counts > 0, last_tile - first_tile + 1, 0)
    vis_end = jnp.cumsum(n_vis)
    vis_start = vis_end - n_vis
    total = vis_end[-1]
    v = jnp.minimum(jnp.arange(N_VISITS, dtype=jnp.int32), total - 1)
    e = jnp.sum((vis_end[None, :] <= v[:, None]).astype(jnp.int32), axis=1)
    tile = first_tile[e] + (v - vis_start[e])
    lo = jnp.clip(starts[e] - tile * tm, 0, tm)
    hi = jnp.clip(ends[e] - tile * tm, 0, tm)
    hi = jnp.where(jnp.arange(N_VISITS) < total, hi, lo)
    run = jnp.cumsum((e != jnp.concatenate([e[:1] - 1, e[:-1]])).astype(jnp.int32)) - 1
    later = jnp.where(e[None, :] > e[:, None], e[None, :], N_EXPERTS)
    next_e = jnp.min(later, axis=1)
    next_e = jnp.where(next_e < N_EXPERTS, next_e, -1)
    sched = (e, tile, lo, hi, run, next_e)
    return starts.astype(jnp.int32), tuple(a.astype(jnp.int32) for a in sched)


def _moe(x, mods, layer, norm_g, router_w, router_bias, w_gate, w_up, w_down,
         ws_gate, ws_up, ws_down, next_norm_g, final):
    h, eidx, rank, wts, counts = _route(x, norm_g, mods, layer, router_w, router_bias)
    starts, sched = _visit_schedule(counts.reshape(N_EXPERTS).astype(jnp.int32))
    eidx, rank = eidx[:TOP_K], rank[:TOP_K]
    expert_ids = jnp.arange(N_EXPERTS, dtype=jnp.int32)[:, None, None]
    pos = rank + jnp.sum(jnp.where(eidx[None] == expert_ids, starts[:, None, None], 0), axis=0)
    xs = _dispatch(pos, h)
    ys = _gmm(sched, xs, layer, w_gate, w_up, w_down)
    shared = _shared_expert(h, layer, ws_gate, ws_up, ws_down)
    return _combine(pos, x, shared, wts.T, mods, layer, next_norm_g, ys, final)


def _pool_diff_layer(x, h, mods, layer, j, cache_k, cache_v, cos, sin,
                     diff_w_in, pool_w, pool_scale, lq1, lk1, lq2, lk2, subln_g, w_out):
    lam_init = 0.8 - 0.6 * math.exp(-0.3 * layer)
    proj = _matmul([h], diff_w_in[j], [0], POOL_W + 3 * DIFF_W, F32, tn=1024, name="diff_in_proj")
    y_pool = _pool(proj, pool_w[j], pool_scale[j])
    lam_params = jnp.stack([lq1[j], lk1[j], lq2[j], lk2[j]])
    ck = cache_k[:, j].reshape(N_LAT_SEQ * PAST_LEN, DIFF_W)
    cv = cache_v[:, j].reshape(N_LAT_SEQ * PAST_LEN, DIFF_W)
    o = _diff_attn(proj, ck, cv, lam_params, subln_g[j], cos, sin, lam_init)
    x = _matmul([y_pool, o], w_out[j], [0, 1], D, F32, tn=1024, epilogue="residual",
                residual=x, mods=mods, layer=layer, gate_chunk=2, name="diff_out_proj")
    k_new = proj[:T_PROMPT, POOL_W + DIFF_W:POOL_W + 2 * DIFF_W]
    v_new = proj[:T_PROMPT, POOL_W + 2 * DIFF_W:]
    shape = (N_PROMPT_SEQ, PROMPT_LEN, DIFF_HEADS, 2 * DIFF_DH)
    return x, k_new.reshape(shape), v_new.reshape(shape)


def _mla_layer(x, h, mods, layer, j, cache_ckv, cache_kpe, cos, sin,
               w_dq, q_norm_g, w_uq, w_dkv, kv_norm_g, w_ukv, w_o):
    cq = _matmul([h], w_dq[j], [0], Q_LORA, BF16, epilogue="rms", gain=q_norm_g[j], name="mla_dq")
    ckv = _matmul([h], w_dkv[j], [0], KV_LORA, F32, epilogue="rms", gain=kv_norm_g[j], name="mla_dkv")
    w_kpe = w_dkv[j][:, KV_LORA:]
    kp_dup = _matmul([h], jnp.concatenate([w_kpe, w_kpe], axis=1), [0], LANES, F32, name="mla_kpe")
    w_uq3 = w_uq[j].reshape(Q_LORA, MLA_HEADS, QK_NOPE + QK_ROPE)
    w_uq_nope = w_uq3[:, :, :QK_NOPE].reshape(Q_LORA, MLA_HEADS * QK_NOPE)
    w_uq_pe = w_uq3[:, :, QK_NOPE:].reshape(Q_LORA, MLA_HEADS * QK_ROPE)
    q_nope = _matmul([cq], w_uq_nope, [0], MLA_HEADS * QK_NOPE, BF16, tn=1024, name="mla_uq_nope")
    q_pe = _matmul([cq], w_uq_pe, [0], MLA_HEADS * QK_ROPE, F32, tn=1024, name="mla_uq_pe")
    n_kv = MLA_HEADS * (QK_NOPE + V_DIM)
    kv = _matmul([ckv], w_ukv[j], [0], n_kv, BF16, tn=1024, name="mla_ukv")
    ckv_ctx = cache_ckv[:, j].reshape(N_LAT_SEQ * PAST_LEN, KV_LORA)
    kv_ctx = _matmul([ckv_ctx], w_ukv[j], [0], n_kv, BF16, tn=1024, name="mla_ukv_ctx")
    kpe_ctx = cache_kpe[:, j].reshape(N_LAT_SEQ * PAST_LEN, QK_ROPE)
    kp_ctx_dup = jnp.concatenate([kpe_ctx, kpe_ctx], axis=1)
    o = _mla_attn(q_nope, q_pe, kv, kp_dup, kv_ctx, kp_ctx_dup, cos, sin)
    x = _matmul([o], w_o[j], [0], D, F32, tn=1024, epilogue="residual",
                residual=x, mods=mods, layer=layer, gate_chunk=2, name="mla_out_proj")
    new_ckv = ckv[:T_PROMPT].reshape(N_PROMPT_SEQ, PROMPT_LEN, KV_LORA)
    new_kpe = kp_dup[:T_PROMPT, :QK_ROPE].reshape(N_PROMPT_SEQ, PROMPT_LEN, QK_ROPE)
    return x, new_ckv, new_kpe


def kernel(x_prompt, x_sample, cache_diff_k, cache_diff_v, cache_mla_ckv, cache_mla_kpe, c, c_ctx,
           ada_w, ada_b, norm_mix_g, norm_ffn_g, final_norm_g,
           diff_w_in, pool_w, pool_scale, diff_lambda_q1, diff_lambda_k1, diff_lambda_q2, diff_lambda_k2,
           diff_subln_g, even_w_out,
           mla_w_dq, mla_q_norm_g, mla_w_uq, mla_w_dkv, mla_kv_norm_g, mla_w_ukv, mla_w_o,
           router_w, router_bias, expert_w_gate, expert_w_up, expert_w_down,
           shared_w_gate, shared_w_up, shared_w_down):
    depth = ada_w.shape[0]
    x = (x_prompt.reshape(T_PROMPT, D), x_sample.reshape(T_LAT, D))
    cond = jnp.concatenate(
        [c_ctx[None, :], c, jnp.zeros((N_GROUPS_PAD - 1 - N_LAT_SEQ, D), F32)], axis=0)
    mods = _ada_params(cond, ada_w, ada_b)
    cos, sin = _rope_tables()

    new_dk, new_dv, new_ckv, new_kpe = [], [], [], []
    h = _norm_mod(x, norm_mix_g[0], mods, 0, 0, 1)
    for i in range(depth):
        j = i // 2
        last = i == depth - 1
        if i % 2 == 0:
            x, k_new, v_new = _pool_diff_layer(
                x, h, mods, i, j, cache_diff_k, cache_diff_v, cos, sin,
                diff_w_in, pool_w, pool_scale, diff_lambda_q1, diff_lambda_k1, diff_lambda_q2,
                diff_lambda_k2, diff_subln_g, even_w_out)
            new_dk.append(k_new)
            new_dv.append(v_new)
        else:
            x, ckv, kpe = _mla_layer(
                x, h, mods, i, j, cache_mla_ckv, cache_mla_kpe, cos, sin,
                mla_w_dq, mla_q_norm_g, mla_w_uq, mla_w_dkv, mla_kv_norm_g, mla_w_ukv, mla_w_o)
            new_ckv.append(ckv)
            new_kpe.append(kpe)
        outs = _moe(x, mods, i, norm_ffn_g[i], router_w[i], router_bias[i],
                    expert_w_gate, expert_w_up, expert_w_down,
                    shared_w_gate, shared_w_up, shared_w_down,
                    final_norm_g if last else norm_mix_g[i + 1], final=last)
        if not last:
            x, h = outs
    y_prompt = outs[0].reshape(N_PROMPT_SEQ, PROMPT_LEN, D)
    y_sample = outs[1].reshape(N_LAT_SEQ, LAT_LEN, D)
    return (y_prompt, y_sample, jnp.stack(new_dk, axis=1), jnp.stack(new_dv, axis=1),
            jnp.stack(new_ckv, axis=1), jnp.stack(new_kpe, axis=1))
```

```python
import functools
import math

import jax
import jax.numpy as jnp
from jax import lax
from jax.experimental import pallas as pl
from jax.experimental.pallas import tpu as pltpu

F32 = jnp.float32
BF16 = jnp.bfloat16

D = 2048
N_PROMPT_SEQ = 16
PROMPT_LEN = 256
N_LAT_SEQ = 4
LAT_LEN = 1024
PAST_LEN = 256
T_PROMPT = N_PROMPT_SEQ * PROMPT_LEN
T_LAT = N_LAT_SEQ * LAT_LEN
T = T_PROMPT + T_LAT
UNIT = 1024
N_UNITS = T // UNIT
N_PROMPT_UNITS = T_PROMPT // UNIT
SEQ_PER_PROMPT_UNIT = UNIT // PROMPT_LEN
N_GROUPS_PAD = 8

GRID_W = 64
ROPE_BASE = 10000.0
EPS = 1e-6
POOL_W = 1024
POOL_WINDOWS = (2, 4, 8, 16)
POOL_G = 256
POOL_HALO = 8
DIFF_W = 1024
DIFF_DH = 64
DIFF_HEADS = 8
DIFF_SCALE = DIFF_DH ** -0.5
MLA_HEADS = 16
Q_LORA = 512
KV_LORA = 512
QK_NOPE = 128
QK_ROPE = 64
V_DIM = 128
MLA_SCALE = (QK_NOPE + QK_ROPE) ** -0.5
N_EXPERTS = 64
TOP_K = 6
N_EGROUPS = 8
EGROUP = N_EXPERTS // N_EGROUPS
TOPK_GROUPS = 4
D_EXPERT = 512
ROUTED_SCALE = 2.5

LANES = 128
ATT_QB = 256
ROUTE_TM = 256
COMBINE_TM = 256
GMM_TM = 256
ROW_UNROLL = 4
XS_RING = 6
N_PAIRS = T * TOP_K
N_ROW_TILES = N_PAIRS // GMM_TM
N_VISITS = N_ROW_TILES + N_EXPERTS
VMEM_LIMIT = 56 * 1024 * 1024

_NT = (((1,), (1,)), ((), ()))
_NN = (((1,), (0,)), ((), ()))


def _cparams(n_axes, vmem=VMEM_LIMIT):
    return pltpu.CompilerParams(dimension_semantics=("arbitrary",) * n_axes, vmem_limit_bytes=vmem)


def _group_of_tile(i, tm):
    n_p = T_PROMPT // tm
    per_seq = LAT_LEN // tm
    return jnp.where(i < n_p, 0, 1 + (i - n_p) // per_seq)


def _sigmoid(x):
    return 1.0 / (1.0 + jnp.exp(-x))


def _rms(x, g):
    return x * lax.rsqrt(jnp.mean(x * x, axis=-1, keepdims=True) + EPS) * g


def _dot_3pass(a, b, dims):
    a_hi = a.astype(BF16)
    a_lo = (a - a_hi.astype(F32)).astype(BF16)
    b_hi = b.astype(BF16)
    b_lo = (b - b_hi.astype(F32)).astype(BF16)

    def dot(p, q):
        return lax.dot_general(p, q, dims, preferred_element_type=F32)

    return dot(a_hi, b_hi) + (dot(a_hi, b_lo) + dot(a_lo, b_hi))


def _ada_kernel(c_ref, w_ref, b_ref, o_ref):
    c = c_ref[...]
    s = c * _sigmoid(c)
    o_ref[...] = _dot_3pass(s, w_ref[...], _NN) + b_ref[...]


def _ada_params(cond, ada_w, ada_b):
    depth, _, n = ada_w.shape
    tn = 1024
    out = pl.pallas_call(
        _ada_kernel,
        grid=(depth, n // tn),
        in_specs=[
            pl.BlockSpec((N_GROUPS_PAD, D), lambda l, j: (0, 0)),
            pl.BlockSpec((None, D, tn), lambda l, j: (l, 0, j)),
            pl.BlockSpec((None, 1, tn), lambda l, j: (l, 0, j)),
        ],
        out_specs=pl.BlockSpec((None, N_GROUPS_PAD, tn), lambda l, j: (l, 0, j)),
        out_shape=jax.ShapeDtypeStruct((depth, N_GROUPS_PAD, n), F32),
        compiler_params=_cparams(2),
        name="ada_params",
    )(cond, ada_w, ada_b.reshape(depth, 1, n))
    return out.reshape(depth, N_GROUPS_PAD, 1, n)


def _mod_spec(layer, chunk, tm):
    return pl.BlockSpec((None, None, 1, D), lambda i: (layer, _group_of_tile(i, tm), 0, chunk))


def _row_source(x):
    return list(x) if isinstance(x, (tuple, list)) else [x]


def _row_source_specs(parts, tm, width, col_of, row_axis):
    if len(parts) == 1:
        return [pl.BlockSpec((tm, width), lambda *ids: (ids[row_axis], col_of(*ids)))]
    n_p = T_PROMPT // tm
    return [pl.BlockSpec((tm, width), lambda *ids: (jnp.minimum(ids[row_axis], n_p - 1), col_of(*ids))),
            pl.BlockSpec((tm, width), lambda *ids: (jnp.maximum(ids[row_axis] - n_p, 0), col_of(*ids)))]


def _read_rows(refs, row_tile, tm):
    if len(refs) == 1:
        return refs[0][...]
    return jnp.where(row_tile < T_PROMPT // tm, refs[0][...], refs[1][...])


def _norm_mod_kernel(*refs, n_x, tm):
    x_refs = refs[:n_x]
    g_ref, sh_ref, sc_ref, o_ref = refs[n_x:]
    y = _rms(_read_rows(x_refs, pl.program_id(0), tm), g_ref[...])
    o_ref[...] = (y * (1.0 + sc_ref[...]) + sh_ref[...]).astype(o_ref.dtype)


def _norm_mod(x, g, mods, layer, shift_chunk, scale_chunk):
    tm = 512
    parts = _row_source(x)
    return pl.pallas_call(
        functools.partial(_norm_mod_kernel, n_x=len(parts), tm=tm),
        grid=(T // tm,),
        in_specs=_row_source_specs(parts, tm, D, lambda i: 0, 0) + [
            pl.BlockSpec((1, D), lambda i: (0, 0)),
            _mod_spec(layer, shift_chunk, tm),
            _mod_spec(layer, scale_chunk, tm),
        ],
        out_specs=pl.BlockSpec((tm, D), lambda i: (i, 0)),
        out_shape=jax.ShapeDtypeStruct((T, D), BF16),
        compiler_params=_cparams(1),
        name="norm_mod",
    )(*parts, g.reshape(1, D), mods, mods)


def _mm_kernel(*refs, n_a, epilogue, n_res, tm):
    a_refs = refs[:n_a]
    w_refs = refs[n_a:2 * n_a]
    pos = 2 * n_a
    if epilogue == "rms":
        g_ref = refs[pos]
        pos += 1
    elif epilogue == "residual":
        res_refs, gate_ref = refs[pos:pos + n_res], refs[pos + n_res]
        pos += n_res + 1
    o_ref = refs[pos]
    w_scr = refs[pos + 1:]

    @pl.when(pl.program_id(1) == 0)
    def _():
        for w_ref, scr in zip(w_refs, w_scr):
            scr[...] = w_ref[...].astype(BF16)

    acc = None
    for a_ref, scr in zip(a_refs, w_scr):
        part = jnp.dot(a_ref[...].astype(BF16), scr[...], preferred_element_type=F32)
        acc = part if acc is None else acc + part
    if epilogue == "rms":
        acc = _rms(acc, g_ref[...])
    elif epilogue == "residual":
        acc = _read_rows(res_refs, pl.program_id(1), tm) + gate_ref[...] * acc
    o_ref[...] = acc.astype(o_ref.dtype)


def _matmul(a_list, w, w_row_blocks, n_out, out_dtype, *, tm=512, tn=None, epilogue=None,
            gain=None, residual=None, mods=None, layer=None, gate_chunk=None, name="matmul"):
    m = a_list[0].shape[0]
    tn = n_out if tn is None else tn
    n_a = len(a_list)
    in_specs, args = [], []
    for a in a_list:
        k = a.shape[1]
        in_specs.append(pl.BlockSpec((tm, k), lambda j, i: (i, 0)))
        args.append(a)
    scratch = []
    for a, rb in zip(a_list, w_row_blocks):
        k = a.shape[1]
        in_specs.append(pl.BlockSpec((k, tn), lambda j, i, rb=rb: (rb, j)))
        args.append(w)
        scratch.append(pltpu.VMEM((k, tn), BF16))
    if epilogue == "rms":
        assert tn == n_out
        in_specs.append(pl.BlockSpec((1, tn), lambda j, i: (0, 0)))
        args.append(gain.reshape(1, n_out))
    elif epilogue == "residual":
        res_parts = _row_source(residual)
        in_specs += _row_source_specs(res_parts, tm, tn, lambda j, i: j, 1)
        args += res_parts
        in_specs.append(pl.BlockSpec(
            (None, None, 1, tn),
            lambda j, i: (layer, _group_of_tile(i, tm), 0, gate_chunk * (D // tn) + j)))
        args.append(mods)
    return pl.pallas_call(
        functools.partial(_mm_kernel, n_a=n_a, epilogue=epilogue, tm=tm,
                          n_res=len(_row_source(residual)) if epilogue == "residual" else 0),
        grid=(n_out // tn, m // tm),
        in_specs=in_specs,
        out_specs=pl.BlockSpec((tm, tn), lambda j, i: (i, j)),
        out_shape=jax.ShapeDtypeStruct((m, n_out), out_dtype),
        scratch_shapes=scratch,
        compiler_params=_cparams(2),
        name=name,
    )(*args)


def _pool_seq(u_ref, row0, seq_len, pw_ref, scale_ref, o_ref, pad_ref):
    t = lax.broadcasted_iota(jnp.int32, (seq_len, 1), 0)
    zeros = jnp.zeros((POOL_HALO, POOL_G), F32)
    for g, w in enumerate(POOL_WINDOWS):
        cols = slice(g * POOL_G, (g + 1) * POOL_G)
        ug = u_ref[row0:row0 + seq_len, cols]
        pad_ref[0:POOL_HALO, :] = zeros
        pad_ref[POOL_HALO:POOL_HALO + seq_len, :] = ug
        pad_ref[POOL_HALO + seq_len:2 * POOL_HALO + seq_len, :] = zeros
        total = None
        for off in range(-(w // 2), w // 2):
            part = pad_ref[POOL_HALO + off:POOL_HALO + off + seq_len, :]
            total = part if total is None else total + part
        cnt = (jnp.minimum(t + w // 2, seq_len) - jnp.maximum(t - w // 2, 0)).astype(F32)
        d = (total / cnt - ug).astype(BF16)
        y = jnp.dot(d, pw_ref[g].astype(BF16), preferred_element_type=F32)
        o_ref[row0:row0 + seq_len, cols] = (y * scale_ref[:, cols]).astype(o_ref.dtype)


def _pool_kernel(u_ref, pw_ref, scale_ref, o_ref, pad_ref):
    unit = pl.program_id(0)

    @pl.when(unit < N_PROMPT_UNITS)
    def _():
        for s in range(SEQ_PER_PROMPT_UNIT):
            _pool_seq(u_ref, s * PROMPT_LEN, PROMPT_LEN, pw_ref, scale_ref, o_ref, pad_ref)

    @pl.when(unit >= N_PROMPT_UNITS)
    def _():
        _pool_seq(u_ref, 0, LAT_LEN, pw_ref, scale_ref, o_ref, pad_ref)


def _pool(proj, pool_w, pool_scale):
    return pl.pallas_call(
        _pool_kernel,
        grid=(N_UNITS,),
        in_specs=[
            pl.BlockSpec((UNIT, POOL_W), lambda u: (u, 0)),
            pl.BlockSpec((len(POOL_WINDOWS), POOL_G, POOL_G), lambda u: (0, 0, 0)),
            pl.BlockSpec((1, POOL_W), lambda u: (0, 0)),
        ],
        out_specs=pl.BlockSpec((UNIT, POOL_W), lambda u: (u, 0)),
        out_shape=jax.ShapeDtypeStruct((T, POOL_W), BF16),
        scratch_shapes=[pltpu.VMEM((LAT_LEN + 2 * POOL_HALO, POOL_G), F32)],
        compiler_params=_cparams(1),
        name="pool",
    )(proj, pool_w, pool_scale.reshape(1, POOL_W))


def _rope_tables():
    pos = jnp.arange(LAT_LEN)
    row = (pos // GRID_W).astype(F32)
    col = (pos % GRID_W).astype(F32)
    quarter = QK_ROPE // 4
    inv = ROPE_BASE ** (-jnp.arange(quarter, dtype=F32) / quarter)
    lane = jnp.arange(LANES)
    axis = (lane % QK_ROPE) // (QK_ROPE // 2)
    freq = inv[lane % quarter]
    p = jnp.where(axis[None, :] == 0, row[:, None], col[:, None])
    ang = p * freq[None, :]
    sign = jnp.where((lane % (QK_ROPE // 2)) < quarter, -1.0, 1.0).astype(F32)
    return jnp.cos(ang), jnp.sin(ang) * sign[None, :]


def _rope(x, cos, sin_signed):
    quarter = QK_ROPE // 4
    lane = lax.broadcasted_iota(jnp.int32, x.shape, 1)
    first_half = (lane % (QK_ROPE // 2)) < quarter
    partner = jnp.where(first_half, pltpu.roll(x, LANES - quarter, 1), pltpu.roll(x, quarter, 1))
    return x * cos + partner * sin_signed


def _diff_core(q, k, v, lam, g, lam_init):
    lane = lax.broadcasted_iota(jnp.int32, q.shape, 1)
    low = lane < DIFF_DH
    q1 = jnp.where(low, q, 0.0).astype(BF16)
    q2 = jnp.where(low, 0.0, q).astype(BF16)

    def probs(qm):
        s = lax.dot_general(qm, k, _NT, preferred_element_type=F32)
        e = jnp.exp(s - jnp.max(s, axis=-1, keepdims=True))
        return e * (1.0 / jnp.sum(e, axis=-1, keepdims=True))

    w = (probs(q1) - lam * probs(q2)).astype(BF16)
    o = jnp.dot(w, v, preferred_element_type=F32)
    return _rms(o, g) * (1.0 - lam_init)


def _diff_attn_kernel(lp_ref, q_ref, k_ref, v_ref, ck_ref, cv_ref, cos_ref, sin_ref, g_ref, o_ref,
                      *, lam_init):
    unit = pl.program_id(0)
    lp = lp_ref[...]
    lam = (jnp.exp(jnp.sum(lp[0:1] * lp[1:2], axis=-1, keepdims=True))
           - jnp.exp(jnp.sum(lp[2:3] * lp[3:4], axis=-1, keepdims=True)) + lam_init)
    g = g_ref[...]

    @pl.when(unit < N_PROMPT_UNITS)
    def _():
        for s in range(SEQ_PER_PROMPT_UNIT):
            rows = slice(s * PROMPT_LEN, (s + 1) * PROMPT_LEN)
            q = q_ref[rows, :] * DIFF_SCALE
            k = k_ref[rows, :].astype(BF16)
            v = v_ref[rows, :].astype(BF16)
            o_ref[rows, :] = _diff_core(q, k, v, lam, g, lam_init).astype(o_ref.dtype)

    @pl.when(unit >= N_PROMPT_UNITS)
    def _():
        cos, sin = cos_ref[...], sin_ref[...]
        k = jnp.concatenate([ck_ref[...].astype(BF16),
                             _rope(k_ref[...], cos, sin).astype(BF16)], axis=0)
        v = jnp.concatenate([cv_ref[...].astype(BF16), v_ref[...].astype(BF16)], axis=0)
        for b in range(LAT_LEN // ATT_QB):
            rows = slice(b * ATT_QB, (b + 1) * ATT_QB)
            q = _rope(q_ref[rows, :], cos[rows], sin[rows]) * DIFF_SCALE
            o_ref[rows, :] = _diff_core(q, k, v, lam, g, lam_init).astype(o_ref.dtype)


def _diff_attn(proj, cache_k, cache_v, lam_params, subln_g, cos, sin, lam_init):
    hd = 2 * DIFF_DH
    q0, k0, v0 = POOL_W // hd, (POOL_W + DIFF_W) // hd, (POOL_W + 2 * DIFF_W) // hd

    def ctx_map(u, h):
        return (jnp.maximum(u - N_PROMPT_UNITS, 0), h)

    return pl.pallas_call(
        functools.partial(_diff_attn_kernel, lam_init=lam_init),
        grid=(N_UNITS, DIFF_HEADS),
        in_specs=[
            pl.BlockSpec((4, DIFF_DH), lambda u, h: (0, 0)),
            pl.BlockSpec((UNIT, hd), lambda u, h: (u, q0 + h)),
            pl.BlockSpec((UNIT, hd), lambda u, h: (u, k0 + h)),
            pl.BlockSpec((UNIT, hd), lambda u, h: (u, v0 + h)),
            pl.BlockSpec((PAST_LEN, hd), ctx_map),
            pl.BlockSpec((PAST_LEN, hd), ctx_map),
            pl.BlockSpec((LAT_LEN, LANES), lambda u, h: (0, 0)),
            pl.BlockSpec((LAT_LEN, LANES), lambda u, h: (0, 0)),
            pl.BlockSpec((1, hd), lambda u, h: (0, 0)),
        ],
        out_specs=pl.BlockSpec((UNIT, hd), lambda u, h: (u, h)),
        out_shape=jax.ShapeDtypeStruct((T, DIFF_W), BF16),
        compiler_params=_cparams(2),
        name="diff_attn",
    )(lam_params, proj, proj, proj, cache_k, cache_v, cos, sin, subln_g.reshape(1, hd))


def _mla_core(qc, kc, v):
    s = lax.dot_general(qc, kc, _NT, preferred_element_type=F32) * MLA_SCALE
    e = jnp.exp(s - jnp.max(s, axis=-1, keepdims=True))
    p = (e * (1.0 / jnp.sum(e, axis=-1, keepdims=True))).astype(BF16)
    return jnp.dot(p, v, preferred_element_type=F32)


def _mla_attn_kernel(qn_ref, qp_ref, kn_ref, v_ref, kp_ref, ckn_ref, cv_ref, ckp_ref, cos_ref, sin_ref,
                     o_ref):
    unit = pl.program_id(0)
    head = pl.program_id(1)
    lane = lax.broadcasted_iota(jnp.int32, (ATT_QB, LANES), 1)
    mine = (lane // QK_ROPE) == (head % 2)

    @pl.when(unit < N_PROMPT_UNITS)
    def _():
        for s in range(SEQ_PER_PROMPT_UNIT):
            rows = slice(s * PROMPT_LEN, (s + 1) * PROMPT_LEN)
            qp = jnp.where(mine, qp_ref[rows, :], 0.0).astype(BF16)
            qc = jnp.concatenate([qn_ref[rows, :], qp], axis=1)
            kc = jnp.concatenate([kn_ref[rows, :], kp_ref[rows, :].astype(BF16)], axis=1)
            o_ref[rows, :] = _mla_core(qc, kc, v_ref[rows, :]).astype(o_ref.dtype)

    @pl.when(unit >= N_PROMPT_UNITS)
    def _():
        cos, sin = cos_ref[...], sin_ref[...]
        k_ctx = jnp.concatenate([ckn_ref[...], ckp_ref[...].astype(BF16)], axis=1)
        k_new = jnp.concatenate([kn_ref[...], _rope(kp_ref[...], cos, sin).astype(BF16)], axis=1)
        kc = jnp.concatenate([k_ctx, k_new], axis=0)
        v = jnp.concatenate([cv_ref[...], v_ref[...]], axis=0)
        for b in range(LAT_LEN // ATT_QB):
            rows = slice(b * ATT_QB, (b + 1) * ATT_QB)
            qp = jnp.where(mine, _rope(qp_ref[rows, :], cos[rows], sin[rows]), 0.0).astype(BF16)
            qc = jnp.concatenate([qn_ref[rows, :], qp], axis=1)
            o_ref[rows, :] = _mla_core(qc, kc, v).astype(o_ref.dtype)


def _mla_attn(q_nope, q_pe, kv, kp_dup, kv_ctx, kp_ctx_dup, cos, sin):
    def ctx_row(u):
        return jnp.maximum(u - N_PROMPT_UNITS, 0)

    return pl.pallas_call(
        _mla_attn_kernel,
        grid=(N_UNITS, MLA_HEADS),
        in_specs=[
            pl.BlockSpec((UNIT, QK_NOPE), lambda u, h: (u, h)),
            pl.BlockSpec((UNIT, LANES), lambda u, h: (u, h // 2)),
            pl.BlockSpec((UNIT, QK_NOPE), lambda u, h: (u, 2 * h)),
            pl.BlockSpec((UNIT, V_DIM), lambda u, h: (u, 2 * h + 1)),
            pl.BlockSpec((UNIT, LANES), lambda u, h: (u, 0)),
            pl.BlockSpec((PAST_LEN, QK_NOPE), lambda u, h: (ctx_row(u), 2 * h)),
            pl.BlockSpec((PAST_LEN, V_DIM), lambda u, h: (ctx_row(u), 2 * h + 1)),
            pl.BlockSpec((PAST_LEN, LANES), lambda u, h: (ctx_row(u), 0)),
            pl.BlockSpec((LAT_LEN, LANES), lambda u, h: (0, 0)),
            pl.BlockSpec((LAT_LEN, LANES), lambda u, h: (0, 0)),
        ],
        out_specs=pl.BlockSpec((UNIT, V_DIM), lambda u, h: (u, h)),
        out_shape=jax.ShapeDtypeStruct((T, MLA_HEADS * V_DIM), BF16),
        compiler_params=_cparams(2),
        name="mla_attn",
    )(q_nope, q_pe, kv, kv, kp_dup, kv_ctx, kv_ctx, kp_ctx_dup, cos, sin)


def _route_kernel(x_ref, g_ref, sh_ref, sc_ref, rw_ref, rb_ref,
                  h_ref, eidx_ref, rank_ref, wts_ref, cnt_ref, carry_ref):
    tm = ROUTE_TM

    @pl.when(pl.program_id(0) == 0)
    def _():
        carry_ref[...] = jnp.zeros_like(carry_ref)

    h = _rms(x_ref[...], g_ref[...]) * (1.0 + sc_ref[...]) + sh_ref[...]
    h_ref[...] = h
    logits = _dot_3pass(rw_ref[...], h, _NT)
    scores = _sigmoid(logits)
    choice = scores + rb_ref[...]
    neg = -jnp.inf

    def take_max(vals, iota, n):
        m = jnp.max(vals, axis=0, keepdims=True)
        idx = jnp.min(jnp.where(vals == m, iota, n), axis=0, keepdims=True)
        return iota == idx, m, idx

    iota8 = lax.broadcasted_iota(jnp.int32, (8, tm), 0)

    def stack_rows(rows, dtype):
        out = jnp.zeros((8, tm), dtype)
        for k, r in enumerate(rows):
            out = jnp.where(iota8 == k, r.astype(dtype), out)
        return out

    gscore = []
    for gi in range(N_EGROUPS):
        grp = choice[gi * EGROUP:(gi + 1) * EGROUP]
        oh, m1, _ = take_max(grp, iota8, EGROUP)
        m2 = jnp.max(jnp.where(oh, neg, grp), axis=0, keepdims=True)
        gscore.append(m1 + m2)
    cur = stack_rows(gscore, F32)
    gsel = jnp.zeros((N_EGROUPS, tm), F32)
    for _ in range(TOPK_GROUPS):
        oh, _, _ = take_max(cur, iota8, N_EGROUPS)
        gsel = jnp.where(oh, 1.0, gsel)
        cur = jnp.where(oh, neg, cur)
    emask = jnp.concatenate(
        [jnp.broadcast_to(gsel[gi:gi + 1], (EGROUP, tm)) for gi in range(N_EGROUPS)], axis=0)
    masked = jnp.where(emask > 0.0, choice, neg)

    iota_e = lax.broadcasted_iota(jnp.int32, (N_EXPERTS, tm), 0)
    onehots, idxs, wsel = [], [], []
    sel = jnp.zeros((N_EXPERTS, tm), F32)
    for _ in range(TOP_K):
        oh, _, idx = take_max(masked, iota_e, N_EXPERTS)
        onehots.append(oh)
        idxs.append(idx)
        wsel.append(jnp.sum(jnp.where(oh, scores, 0.0), axis=0, keepdims=True))
        masked = jnp.where(oh, neg, masked)
        sel = jnp.where(oh, 1.0, sel)
    wsum = wsel[0]
    for w in wsel[1:]:
        wsum = wsum + w

    before = jnp.where(lax.broadcasted_iota(jnp.int32, (tm, tm), 0)
                       < lax.broadcasted_iota(jnp.int32, (tm, tm), 1), 1.0, 0.0).astype(BF16)
    rank_all = carry_ref[...] + jnp.dot(sel.astype(BF16), before, preferred_element_type=F32)
    carry_ref[...] = carry_ref[...] + jnp.sum(sel, axis=1, keepdims=True)
    cnt_ref[...] = carry_ref[...]

    ranks = [jnp.sum(jnp.where(oh, rank_all, 0.0), axis=0, keepdims=True) for oh in onehots]
    eidx_ref[...] = stack_rows(idxs, jnp.int32)
    rank_ref[...] = stack_rows(ranks, jnp.int32)
    wts_ref[...] = stack_rows([w / wsum * ROUTED_SCALE for w in wsel], F32)


def _route(x, g, mods, layer, router_w, router_bias):
    tm = ROUTE_TM
    return pl.pallas_call(
        _route_kernel,
        grid=(T // tm,),
        in_specs=[
            pl.BlockSpec((tm, D), lambda i: (i, 0)),
            pl.BlockSpec((1, D), lambda i: (0, 0)),
            _mod_spec(layer, 3, tm),
            _mod_spec(layer, 4, tm),
            pl.BlockSpec((N_EXPERTS, D), lambda i: (0, 0)),
            pl.BlockSpec((N_EXPERTS, 1), lambda i: (0, 0)),
        ],
        out_specs=[
            pl.BlockSpec((tm, D), lambda i: (i, 0)),
            pl.BlockSpec((8, tm), lambda i: (0, i)),
            pl.BlockSpec((8, tm), lambda i: (0, i)),
            pl.BlockSpec((8, tm), lambda i: (0, i)),
            pl.BlockSpec((N_EXPERTS, 1), lambda i: (0, 0)),
        ],
        out_shape=[
            jax.ShapeDtypeStruct((T, D), F32),
            jax.ShapeDtypeStruct((8, T), jnp.int32),
            jax.ShapeDtypeStruct((8, T), jnp.int32),
            jax.ShapeDtypeStruct((8, T), F32),
            jax.ShapeDtypeStruct((N_EXPERTS, 1), F32),
        ],
        scratch_shapes=[pltpu.VMEM((N_EXPERTS, 1), F32)],
        compiler_params=_cparams(1),
        name="route",
    )(x, g.reshape(1, D), mods, mods, router_w.T, router_bias.reshape(N_EXPERTS, 1))


def _row_copy(src, src_row, dst, dst_row, sem):
    return pltpu.make_async_copy(src.at[pl.ds(src_row, 1), :], dst.at[pl.ds(dst_row, 1), :], sem)


def _tile_positions(pos, tm):
    return pos.reshape(TOP_K, T // tm, tm).transpose(1, 0, 2).reshape(T // tm, 1, TOP_K * tm)


def _dispatch_kernel(pos_ref, h_ref, xs_ref, sem):
    tm = ROUTE_TM

    def issue(j, carry):
        for k in range(TOP_K):
            _row_copy(h_ref, j, xs_ref, pos_ref[0, 0, k * tm + j], sem).start(priority=k % 2)
        return carry

    def drain(j, carry):
        for _ in range(TOP_K):
            _row_copy(h_ref, 0, xs_ref, 0, sem).wait()
        return carry

    lax.fori_loop(0, tm, issue, 0, unroll=ROW_UNROLL)
    lax.fori_loop(0, tm, drain, 0, unroll=ROW_UNROLL)


def _dispatch(pos, h):
    tm = ROUTE_TM
    return pl.pallas_call(
        _dispatch_kernel,
        grid=(T // tm,),
        in_specs=[
            pl.BlockSpec((1, 1, TOP_K * tm), lambda i: (i, 0, 0), memory_space=pltpu.SMEM),
            pl.BlockSpec((tm, D), lambda i: (i, 0)),
        ],
        out_specs=pl.BlockSpec(memory_space=pl.ANY),
        out_shape=jax.ShapeDtypeStruct((N_PAIRS, D), F32),
        scratch_shapes=[pltpu.SemaphoreType.DMA(())],
        compiler_params=_cparams(1),
        name="dispatch",
    )(_tile_positions(pos, tm), h)


def _combine_kernel(pos_ref, pos_next_ref, x_ref, sh_ref, w_ref, gate_ref, ng_ref, nsh_ref, nsc_ref,
                    ys_ref, *rest, final):
    out_refs, (buf_ref, sem) = rest[:-2], rest[-2:]
    tm = COMBINE_TM
    i = pl.program_id(0)
    n = pl.num_programs(0)
    slot = i % 2

    def start_gathers(p_ref, dst_slot):
        def issue(j, carry):
            for k in range(TOP_K):
                _row_copy(ys_ref, p_ref[0, 0, k * tm + j], buf_ref.at[dst_slot, k], j,
                          sem.at[dst_slot]).start(priority=k % 2)
            return carry
        lax.fori_loop(0, tm, issue, 0, unroll=ROW_UNROLL)

    @pl.when(i == 0)
    def _():
        start_gathers(pos_ref, 0)

    @pl.when(i + 1 < n)
    def _():
        start_gathers(pos_next_ref, 1 - slot)

    def drain(j, carry):
        for k in range(TOP_K):
            _row_copy(ys_ref, 0, buf_ref.at[slot, k], 0, sem.at[slot]).wait()
        return carry

    lax.fori_loop(0, tm, drain, 0, unroll=ROW_UNROLL)
    w = w_ref[...]
    acc = sh_ref[...].astype(F32)
    for k in range(TOP_K):
        acc = acc + w[:, k:k + 1] * buf_ref[slot, k]
    xn = x_ref[...] + gate_ref[...] * acc
    normed = _rms(xn, ng_ref[...])
    if not final:
        out_refs[0][...] = xn
        out_refs[1][...] = (normed * (1.0 + nsc_ref[...]) + nsh_ref[...]).astype(BF16)
    else:
        xn = normed

        @pl.when(i < T_PROMPT // tm)
        def _():
            out_refs[0][...] = xn

        @pl.when(i >= T_PROMPT // tm)
        def _():
            out_refs[1][...] = xn


def _combine(pos, x, shared, wts_t, mods, layer, next_norm_g, ys, final):
    tm = COMBINE_TM
    n_p = T_PROMPT // tm
    next_layer = layer if final else layer + 1
    if final:
        out_specs = [pl.BlockSpec((tm, D), lambda i: (jnp.minimum(i, n_p - 1), 0)),
                     pl.BlockSpec((tm, D), lambda i: (jnp.maximum(i - n_p, 0), 0))]
        out_shape = [jax.ShapeDtypeStruct((T_PROMPT, D), F32), jax.ShapeDtypeStruct((T_LAT, D), F32)]
    else:
        out_specs = [pl.BlockSpec((tm, D), lambda i: (i, 0)), pl.BlockSpec((tm, D), lambda i: (i, 0))]
        out_shape = [jax.ShapeDtypeStruct((T, D), F32), jax.ShapeDtypeStruct((T, D), BF16)]
    tiles = _tile_positions(pos, tm)
    return pl.pallas_call(
        functools.partial(_combine_kernel, final=final),
        grid=(T // tm,),
        in_specs=[
            pl.BlockSpec((1, 1, TOP_K * tm), lambda i: (i, 0, 0), memory_space=pltpu.SMEM),
            pl.BlockSpec((1, 1, TOP_K * tm), lambda i: (jnp.minimum(i + 1, T // tm - 1), 0, 0),
                         memory_space=pltpu.SMEM),
            pl.BlockSpec((tm, D), lambda i: (i, 0)),
            pl.BlockSpec((tm, D), lambda i: (i, 0)),
            pl.BlockSpec((tm, 8), lambda i: (i, 0)),
            pl.BlockSpec((None, None, 1, D), lambda i: (layer, _group_of_tile(i, tm), 0, 5)),
            pl.BlockSpec((1, D), lambda i: (0, 0)),
            _mod_spec(next_layer, 0, tm),
            _mod_spec(next_layer, 1, tm),
            pl.BlockSpec(memory_space=pl.ANY),
        ],
        out_specs=out_specs,
        out_shape=out_shape,
        scratch_shapes=[pltpu.VMEM((2, TOP_K, tm, D), F32), pltpu.SemaphoreType.DMA((2,))],
        compiler_params=_cparams(1),
        name="combine",
    )(tiles, tiles, x, shared, wts_t, mods, next_norm_g.reshape(1, D), mods, mods, ys)


def _swiglu(x, wgu_scr, wd_scr):
    gu = jnp.dot(x, wgu_scr[...], preferred_element_type=F32)
    gate, up = gu[:, :D_EXPERT], gu[:, D_EXPERT:]
    act = (gate * _sigmoid(gate) * up).astype(BF16)
    return jnp.dot(act, wd_scr[...], preferred_element_type=F32)


def _load_expert_weights(wg_ref, wu_ref, wd_ref, wgu_scr, wd_scr):
    wgu_scr[:, :D_EXPERT] = wg_ref[...].astype(BF16)
    wgu_scr[:, D_EXPERT:] = wu_ref[...].astype(BF16)
    wd_scr[...] = wd_ref[...].astype(BF16)


def _shared_kernel(h_ref, wg_ref, wu_ref, wd_ref, o_ref, wgu_scr, wd_scr):
    @pl.when(pl.program_id(0) == 0)
    def _():
        _load_expert_weights(wg_ref, wu_ref, wd_ref, wgu_scr, wd_scr)

    o_ref[...] = _swiglu(h_ref[...].astype(BF16), wgu_scr, wd_scr).astype(o_ref.dtype)


def _shared_expert(h, layer, w_gate, w_up, w_down):
    tm = 512
    return pl.pallas_call(
        _shared_kernel,
        grid=(T // tm,),
        in_specs=[
            pl.BlockSpec((tm, D), lambda i: (i, 0)),
            pl.BlockSpec((None, D, D_EXPERT), lambda i: (layer, 0, 0)),
            pl.BlockSpec((None, D, D_EXPERT), lambda i: (layer, 0, 0)),
            pl.BlockSpec((None, D_EXPERT, D), lambda i: (layer, 0, 0)),
        ],
        out_specs=pl.BlockSpec((tm, D), lambda i: (i, 0)),
        out_shape=jax.ShapeDtypeStruct((T, D), BF16),
        scratch_shapes=[pltpu.VMEM((D, 2 * D_EXPERT), BF16), pltpu.VMEM((D_EXPERT, D), BF16)],
        compiler_params=_cparams(1),
        name="shared_expert",
    )(h, w_gate, w_up, w_down)


def _gmm_kernel(e_ref, t_ref, lo_ref, hi_ref, run_ref, next_e_ref, xs_hbm, wg_hbm, wu_hbm, wd_hbm, ys_ref,
                xs_buf, xs_sem, wg_buf, wu_buf, wd_buf, wsem, wgu_scr, wd_scr, *, layer):
    v = pl.program_id(0)
    prev = jnp.maximum(v - 1, 0)
    new_expert = (v == 0) | (e_ref[v] != e_ref[prev])
    tile = t_ref[v]
    new_tile = (v == 0) | (tile != t_ref[prev])
    lo, hi = lo_ref[v], hi_ref[v]
    slot = run_ref[v] % 2

    def tile_copy(t):
        rows = pl.ds(pl.multiple_of(t * GMM_TM, GMM_TM), GMM_TM)
        return pltpu.make_async_copy(xs_hbm.at[rows, :], xs_buf.at[t % XS_RING], xs_sem.at[t % XS_RING])

    @pl.when(v == 0)
    def _():
        for t in range(XS_RING - 1):
            tile_copy(t).start()

    @pl.when(new_tile)
    def _():
        tile_copy(tile).wait()

        @pl.when(tile + XS_RING - 1 < N_ROW_TILES)
        def _():
            tile_copy(tile + XS_RING - 1).start()

    def weight_copies(expert, dst_slot):
        return [pltpu.make_async_copy(src.at[layer, expert], dst.at[dst_slot], wsem.at[dst_slot])
                for src, dst in ((wg_hbm, wg_buf), (wu_hbm, wu_buf), (wd_hbm, wd_buf))]

    @pl.when(v == 0)
    def _():
        for c in weight_copies(e_ref[0], 0):
            c.start(priority=1)

    @pl.when(new_expert)
    def _():
        for c in weight_copies(e_ref[v], slot):
            c.wait()

        @pl.when(next_e_ref[v] >= 0)
        def _():
            for c in weight_copies(next_e_ref[v], 1 - slot):
                c.start(priority=1)

        _load_expert_weights(wg_buf.at[slot], wu_buf.at[slot], wd_buf.at[slot], wgu_scr, wd_scr)

    @pl.when(hi > lo)
    def _():
        y = _swiglu(xs_buf[tile % XS_RING].astype(BF16), wgu_scr, wd_scr)
        row = lax.broadcasted_iota(jnp.int32, (GMM_TM, 1), 0)
        mine = (row >= lo) & (row < hi)

        @pl.when(new_tile)
        def _():
            ys_ref[...] = jnp.where(mine, y, 0.0)

        @pl.when(jnp.logical_not(new_tile))
        def _():
            ys_ref[...] = jnp.where(mine, y, ys_ref[...])


def _gmm(sched, xs, layer, w_gate, w_up, w_down):
    return pl.pallas_call(
        functools.partial(_gmm_kernel, layer=layer),
        grid_spec=pltpu.PrefetchScalarGridSpec(
            num_scalar_prefetch=6,
            grid=(N_VISITS,),
            in_specs=[
                pl.BlockSpec(memory_space=pl.ANY),
                pl.BlockSpec(memory_space=pl.ANY),
                pl.BlockSpec(memory_space=pl.ANY),
                pl.BlockSpec(memory_space=pl.ANY),
            ],
            out_specs=pl.BlockSpec((GMM_TM, D), lambda v, e, t, *_: (t[v], 0)),
            scratch_shapes=[
                pltpu.VMEM((XS_RING, GMM_TM, D), F32),
                pltpu.SemaphoreType.DMA((XS_RING,)),
                pltpu.VMEM((2, D, D_EXPERT), F32),
                pltpu.VMEM((2, D, D_EXPERT), F32),
                pltpu.VMEM((2, D_EXPERT, D), F32),
                pltpu.SemaphoreType.DMA((2,)),
                pltpu.VMEM((D, 2 * D_EXPERT), BF16),
                pltpu.VMEM((D_EXPERT, D), BF16),
            ],
        ),
        out_shape=jax.ShapeDtypeStruct((N_PAIRS, D), F32),
        compiler_params=_cparams(1),
        name="experts",
    )(*sched, xs, w_gate, w_up, w_down)


def _visit_schedule(counts):
    tm = GMM_TM
    ends = jnp.cumsum(counts)
    starts = ends - counts
    first_tile = starts // tm
    last_tile = jnp.maximum(ends - 1, 0) // tm
    n_vis = jnp.where(counts > 0, last_tile - first_tile + 1, 0)
    vis_end = jnp.cumsum(n_vis)
    vis_start = vis_end - n_vis
    total = vis_end[-1]
    v = jnp.minimum(jnp.arange(N_VISITS, dtype=jnp.int32), total - 1)
    e = jnp.sum((vis_end[None, :] <= v[:, None]).astype(jnp.int32), axis=1)
    tile = first_tile[e] + (v - vis_start[e])
    lo = jnp.clip(starts[e] - tile * tm, 0, tm)
    hi = jnp.clip(ends[e] - tile * tm, 0, tm)
    hi = jnp.where(jnp.arange(N_VISITS) < total, hi, lo)
    run = jnp.cumsum((e != jnp.concatenate([e[:1] - 1, e[:-1]])).astype(jnp.int32)) - 1
    later = jnp.where(e[None, :] > e[:, None], e[None, :], N_EXPERTS)
    next_e = jnp.min(later, axis=1)
    next_e = jnp.where(next_e < N_EXPERTS, next_e, -1)
    sched = (e, tile, lo, hi, run, next_e)
    return starts.astype(jnp.int32), tuple(a.astype(jnp.int32) for a in sched)


def _moe(x, mods, layer, norm_g, router_w, router_bias, w_gate, w_up, w_down,
         ws_gate, ws_up, ws_down, next_norm_g, final):
    h, eidx, rank, wts, counts = _route(x, norm_g, mods, layer, router_w, router_bias)
    starts, sched = _visit_schedule(counts.reshape(N_EXPERTS).astype(jnp.int32))
    eidx, rank = eidx[:TOP_K], rank[:TOP_K]
    expert_ids = jnp.arange(N_EXPERTS, dtype=jnp.int32)[:, None, None]
    pos = rank + jnp.sum(jnp.where(eidx[None] == expert_ids, starts[:, None, None], 0), axis=0)
    xs = _dispatch(pos, h)
    ys = _gmm(sched, xs, layer, w_gate, w_up, w_down)
    shared = _shared_expert(h, layer, ws_gate, ws_up, ws_down)
    return _combine(pos, x, shared, wts.T, mods, layer, next_norm_g, ys, final)


def _pool_diff_layer(x, h, mods, layer, j, cache_k, cache_v, cos, sin,
                     diff_w_in, pool_w, pool_scale, lq1, lk1, lq2, lk2, subln_g, w_out):
    lam_init = 0.8 - 0.6 * math.exp(-0.3 * layer)
    proj = _matmul([h], diff_w_in[j], [0], POOL_W + 3 * DIFF_W, F32, tn=1024, name="diff_in_proj")
    y_pool = _pool(proj, pool_w[j], pool_scale[j])
    lam_params = jnp.stack([lq1[j], lk1[j], lq2[j], lk2[j]])
    ck = cache_k[:, j].reshape(N_LAT_SEQ * PAST_LEN, DIFF_W)
    cv = cache_v[:, j].reshape(N_LAT_SEQ * PAST_LEN, DIFF_W)
    o = _diff_attn(proj, ck, cv, lam_params, subln_g[j], cos, sin, lam_init)
    x = _matmul([y_pool, o], w_out[j], [0, 1], D, F32, tn=1024, epilogue="residual",
                residual=x, mods=mods, layer=layer, gate_chunk=2, name="diff_out_proj")
    k_new = proj[:T_PROMPT, POOL_W + DIFF_W:POOL_W + 2 * DIFF_W]
    v_new = proj[:T_PROMPT, POOL_W + 2 * DIFF_W:]
    shape = (N_PROMPT_SEQ, PROMPT_LEN, DIFF_HEADS, 2 * DIFF_DH)
    return x, k_new.reshape(shape), v_new.reshape(shape)


def _mla_layer(x, h, mods, layer, j, cache_ckv, cache_kpe, cos, sin,
               w_dq, q_norm_g, w_uq, w_dkv, kv_norm_g, w_ukv, w_o):
    cq = _matmul([h], w_dq[j], [0], Q_LORA, BF16, epilogue="rms", gain=q_norm_g[j], name="mla_dq")
    ckv = _matmul([h], w_dkv[j], [0], KV_LORA, F32, epilogue="rms", gain=kv_norm_g[j], name="mla_dkv")
    w_kpe = w_dkv[j][:, KV_LORA:]
    kp_dup = _matmul([h], jnp.concatenate([w_kpe, w_kpe], axis=1), [0], LANES, F32, name="mla_kpe")
    w_uq3 = w_uq[j].reshape(Q_LORA, MLA_HEADS, QK_NOPE + QK_ROPE)
    w_uq_nope = w_uq3[:, :, :QK_NOPE].reshape(Q_LORA, MLA_HEADS * QK_NOPE)
    w_uq_pe = w_uq3[:, :, QK_NOPE:].reshape(Q_LORA, MLA_HEADS * QK_ROPE)
    q_nope = _matmul([cq], w_uq_nope, [0], MLA_HEADS * QK_NOPE, BF16, tn=1024, name="mla_uq_nope")
    q_pe = _matmul([cq], w_uq_pe, [0], MLA_HEADS * QK_ROPE, F32, tn=1024, name="mla_uq_pe")
    n_kv = MLA_HEADS * (QK_NOPE + V_DIM)
    kv = _matmul([ckv], w_ukv[j], [0], n_kv, BF16, tn=1024, name="mla_ukv")
    ckv_ctx = cache_ckv[:, j].reshape(N_LAT_SEQ * PAST_LEN, KV_LORA)
    kv_ctx = _matmul([ckv_ctx], w_ukv[j], [0], n_kv, BF16, tn=1024, name="mla_ukv_ctx")
    kpe_ctx = cache_kpe[:, j].reshape(N_LAT_SEQ * PAST_LEN, QK_ROPE)
    kp_ctx_dup = jnp.concatenate([kpe_ctx, kpe_ctx], axis=1)
    o = _mla_attn(q_nope, q_pe, kv, kp_dup, kv_ctx, kp_ctx_dup, cos, sin)
    x = _matmul([o], w_o[j], [0], D, F32, tn=1024, epilogue="residual",
                residual=x, mods=mods, layer=layer, gate_chunk=2, name="mla_out_proj")
    new_ckv = ckv[:T_PROMPT].reshape(N_PROMPT_SEQ, PROMPT_LEN, KV_LORA)
    new_kpe = kp_dup[:T_PROMPT, :QK_ROPE].reshape(N_PROMPT_SEQ, PROMPT_LEN, QK_ROPE)
    return x, new_ckv, new_kpe


def kernel(x_prompt, x_sample, cache_diff_k, cache_diff_v, cache_mla_ckv, cache_mla_kpe, c, c_ctx,
           ada_w, ada_b, norm_mix_g, norm_ffn_g, final_norm_g,
           diff_w_in, pool_w, pool_scale, diff_lambda_q1, diff_lambda_k1, diff_lambda_q2, diff_lambda_k2,
           diff_subln_g, even_w_out,
           mla_w_dq, mla_q_norm_g, mla_w_uq, mla_w_dkv, mla_kv_norm_g, mla_w_ukv, mla_w_o,
           router_w, router_bias, expert_w_gate, expert_w_up, expert_w_down,
           shared_w_gate, shared_w_up, shared_w_down):
    depth = ada_w.shape[0]
    x = (x_prompt.reshape(T_PROMPT, D), x_sample.reshape(T_LAT, D))
    cond = jnp.concatenate(
        [c_ctx[None, :], c, jnp.zeros((N_GROUPS_PAD - 1 - N_LAT_SEQ, D), F32)], axis=0)
    mods = _ada_params(cond, ada_w, ada_b)
    cos, sin = _rope_tables()

    new_dk, new_dv, new_ckv, new_kpe = [], [], [], []
    h = _norm_mod(x, norm_mix_g[0], mods, 0, 0, 1)
    for i in range(depth):
        j = i // 2
        last = i == depth - 1
        if i % 2 == 0:
            x, k_new, v_new = _pool_diff_layer(
                x, h, mods, i, j, cache_diff_k, cache_diff_v, cos, sin,
                diff_w_in, pool_w, pool_scale, diff_lambda_q1, diff_lambda_k1, diff_lambda_q2,
                diff_lambda_k2, diff_subln_g, even_w_out)
            new_dk.append(k_new)
            new_dv.append(v_new)
        else:
            x, ckv, kpe = _mla_layer(
                x, h, mods, i, j, cache_mla_ckv, cache_mla_kpe, cos, sin,
                mla_w_dq, mla_q_norm_g, mla_w_uq, mla_w_dkv, mla_kv_norm_g, mla_w_ukv, mla_w_o)
            new_ckv.append(ckv)
            new_kpe.append(kpe)
        outs = _moe(x, mods, i, norm_ffn_g[i], router_w[i], router_bias[i],
                    expert_w_gate, expert_w_up, expert_w_down,
                    shared_w_gate, shared_w_up, shared_w_down,
                    final_norm_g if last else norm_mix_g[i + 1], final=last)
        if not last:
            x, h = outs
    y_prompt = outs[0].reshape(N_PROMPT_SEQ, PROMPT_LEN, D)
    y_sample = outs[1].reshape(N_LAT_SEQ, LAT_LEN, D)
    return (y_prompt, y_sample, jnp.stack(new_dk, axis=1), jnp.stack(new_dv, axis=1),
            jnp.stack(new_ckv, axis=1), jnp.stack(new_kpe, axis=1))
```
